```python
import jax
import jax.numpy as jnp
from jax import lax
import numpy as np

D_MODEL = 1024
BATCH = 8
SEQ = 2048
DEPTH = 2

GRID_W = 64
CTX_LEN = 256

A_HEADS = 8
A_HEAD_DIM = 64
D_A = A_HEADS * A_HEAD_DIM
LORA_W = 64
LORA_A = 64
LORA_G = 128

B_HEADS = 8
B_HEAD_DIM = 64
D_B = B_HEADS * B_HEAD_DIM
WIN_H = 8
WIN_W = 16

N_EXPERTS = 16
N_GROUPS = 4
EXPERTS_PER_GROUP = N_EXPERTS // N_GROUPS
TOP_K = 2
D_EXPERT = 512

RMS_EPS = 1e-6
GN_EPS = 64e-5

IN_SPLITS = (D_A, D_A, D_A, 2 * LORA_W, 2 * LORA_A, LORA_G, D_B, D_B, D_B, D_MODEL, D_MODEL)
IN_COLS = sum(IN_SPLITS)
IN_OFFSETS = tuple(int(o) for o in np.cumsum(IN_SPLITS)[:-1])

kernel_name = 'hybrid_rwkv7_natten_moe_dit'


def rms_norm(x, g):
    xf = x.astype(jnp.float32)
    y = xf * lax.rsqrt(jnp.mean(xf * xf, axis=-1, keepdims=True) + RMS_EPS)
    return (y * g.astype(jnp.float32)).astype(x.dtype)


def modulate(h, shift, scale):
    return h * (1.0 + scale) + shift


def split_cols(z):
    return jnp.split(z, IN_OFFSETS, axis=-1)


def rwkv_prep(z, w0, w_lora_b, a0, a_lora_b, k_k, k_a):
    f = jnp.float32
    k, v, lw, la = z[1].astype(f), z[2].astype(f), z[3].astype(f), z[4].astype(f)
    bsz, t = k.shape[:2]
    heads = lambda u: u.reshape(bsz, t, A_HEADS, A_HEAD_DIM)
    lw = jnp.tanh(lw).reshape(bsz, t, 2, LORA_W)
    wl = w0.astype(f) + jnp.einsum('btzr,zrc->btzc', lw, w_lora_b.astype(f))
    decay = jnp.exp(-jnp.exp(-jax.nn.softplus(-wl) - 0.5))
    la = la.reshape(bsz, t, 2, LORA_A)
    a = jax.nn.sigmoid(a0.astype(f) + jnp.einsum('btzr,zrc->btzc', la, a_lora_b.astype(f)))
    kk = heads(k * k_k.astype(f))
    kk = kk * lax.rsqrt(jnp.maximum(jnp.sum(kk * kk, axis=-1, keepdims=True), 1e-12))
    k_dir = k[:, :, None, :] * (1.0 + (a - 1.0) * k_a.astype(f))
    per_dir = [(heads(decay[:, :, d]), heads(k_dir[:, :, d]), heads(a[:, :, d])) for d in range(2)]
    return heads(k), heads(v), kk, per_dir


def rwkv_scan(s0, r, decay, k, v, kk, a, reverse, emit):
    def step(s, inp):
        w_t, k_t, v_t, kk_t, a_t = inp[:5]
        s_kk = jnp.einsum('bhvk,bhk->bhv', s, kk_t)
        s = (s * w_t[:, :, None, :]
             - s_kk[..., None] * (kk_t * a_t)[:, :, None, :]
             + v_t[..., None] * k_t[:, :, None, :])
        y = jnp.einsum('bhvk,bhk->bhv', s, inp[5]) if emit else None
        return s, y
    seq = (decay, k, v, kk, a) + ((r,) if emit else ())
    xs = tuple(jnp.moveaxis(u, 1, 0) for u in seq)
    s_final, ys = lax.scan(step, s0, xs, reverse=reverse)
    return s_final, (jnp.moveaxis(ys, 0, 1) if emit else None)


def rwkv_out(y, r, k, v, lg, r_k, g_lora_b, ln_g, ln_b, out_dtype):
    f = jnp.float32
    bsz, t = y.shape[:2]
    mu = jnp.mean(y, axis=-1, keepdims=True)
    var = jnp.mean(jnp.square(y - mu), axis=-1, keepdims=True)
    yn = ((y - mu) * lax.rsqrt(var + GN_EPS)).reshape(bsz, t, D_A) * ln_g.astype(f) + ln_b.astype(f)
    bonus = (jnp.sum(r * k * r_k.astype(f), axis=-1, keepdims=True) * v).reshape(bsz, t, D_A)
    g = jax.nn.sigmoid(lg.astype(f)) @ g_lora_b.astype(f)
    return ((yn + bonus) * g).astype(out_dtype)


def rwkv_mixer(z_c, z_x, w0, w_lora_b, a0, a_lora_b, g_lora_b, k_k, k_a, r_k, ln_g, ln_b, emit_ctx):
    f = jnp.float32
    k_c, v_c, kk_c, dir_c = rwkv_prep(z_c, w0, w_lora_b, a0, a_lora_b, k_k, k_a)
    k_x, v_x, kk_x, dir_x = rwkv_prep(z_x, w0, w_lora_b, a0, a_lora_b, k_k, k_a)
    bsz, seq = z_x[0].shape[:2]
    r_x = z_x[0].astype(f).reshape(bsz, seq, A_HEADS, A_HEAD_DIM)
    r_c = z_c[0].astype(f).reshape(bsz, z_c[0].shape[1], A_HEADS, A_HEAD_DIM) if emit_ctx else None
    y_x, y_c = 0.0, 0.0
    for d in range(2):
        rev = d == 1
        s0 = jnp.zeros((bsz, A_HEADS, A_HEAD_DIM, A_HEAD_DIM), f)
        w_c, kd_c, a_c = dir_c[d]
        s_c, yc_d = rwkv_scan(s0, r_c, w_c, kd_c, v_c, kk_c, a_c, rev, emit_ctx)
        w_x, kd_x, a_x = dir_x[d]
        _, yx_d = rwkv_scan(s_c, r_x, w_x, kd_x, v_x, kk_x, a_x, rev, True)
        y_x = y_x + yx_d
        if emit_ctx:
            y_c = y_c + yc_d
    out_x = rwkv_out(y_x, r_x, k_x, v_x, z_x[5], r_k, g_lora_b, ln_g, ln_b, z_x[0].dtype)
    out_c = rwkv_out(y_c, r_c, k_c, v_c, z_c[5], r_k, g_lora_b, ln_g, ln_b, z_c[0].dtype) if emit_ctx else None
    return out_c, out_x


def na_mixer(z_c, z_x, q_g, k_g, rpb, need_ctx):
    f = jnp.float32
    scale = B_HEAD_DIM ** -0.5

    def heads(u):
        bsz, t = u.shape[:2]
        return u.reshape(bsz, t, B_HEADS, B_HEAD_DIM).transpose(0, 2, 1, 3)

    k_c, v_c = rms_norm(heads(z_c[7]), k_g), heads(z_c[8])
    q_x = rms_norm(heads(z_x[6]), q_g) * scale
    k_x, v_x = rms_norm(heads(z_x[7]), k_g), heads(z_x[8])
    bsz, _, seq, _ = q_x.shape
    rows = seq // GRID_W
    kh = min(WIN_H, rows)
    kw = WIN_W
    grid = lambda u: u.reshape(bsz, B_HEADS, rows, GRID_W, B_HEAD_DIM)
    q_grid, k_grid, v_grid = grid(q_x), grid(k_x), grid(v_x)

    i = jnp.arange(rows)
    key_rows = jnp.clip(i - kh // 2, 0, rows - kh)[:, None] + jnp.arange(kh)[None, :]
    k_blk = k_grid[:, :, key_rows]
    v_blk = v_grid[:, :, key_rows]
    s_lat = jnp.einsum('bhiqd,bhiakd->bhiqak', q_grid, k_blk).astype(f)

    j = jnp.arange(GRID_W)
    col_start = jnp.clip(j - kw // 2, 0, GRID_W - kw)
    in_win = (j[None, :] >= col_start[:, None]) & (j[None, :] < col_start[:, None] + kw)
    d_row = key_rows - i[:, None] + (WIN_H - 1)
    d_col = jnp.clip(j[None, :] - j[:, None], -(WIN_W - 1), WIN_W - 1) + (WIN_W - 1)
    bias = rpb[:, d_row[:, None, :, None], d_col[None, :, None, :]].astype(f)
    s_lat = jnp.where(in_win[:, None, :], s_lat + bias[None], -jnp.inf)

    s_ctx = jnp.einsum('bhiqd,bhcd->bhiqc', q_grid, k_c).astype(f)
    n_lat = kh * GRID_W
    logits = jnp.concatenate([s_lat.reshape(bsz, B_HEADS, rows, GRID_W, n_lat), s_ctx], axis=-1)
    p = jax.nn.softmax(logits, axis=-1).astype(v_x.dtype)
    p_lat = p[..., :n_lat].reshape(bsz, B_HEADS, rows, GRID_W, kh, GRID_W)
    o = (jnp.einsum('bhiqak,bhiakd->bhiqd', p_lat, v_blk)
         + jnp.einsum('bhiqc,bhcd->bhiqd', p[..., n_lat:], v_c))
    y_x = o.transpose(0, 2, 3, 1, 4).reshape(bsz, seq, D_B)

    y_c = None
    if need_ctx:
        q_c = rms_norm(heads(z_c[6]), q_g) * scale
        p_c = jax.nn.softmax(jnp.einsum('bhqd,bhkd->bhqk', q_c, k_c).astype(f), axis=-1).astype(v_c.dtype)
        y_c = jnp.einsum('bhqk,bhkd->bhqd', p_c, v_c).transpose(0, 2, 1, 3).reshape(bsz, -1, D_B)
    return y_c, y_x


def branch_merge(gate_a, gate_b, y_a, y_b, proj_a, proj_b, w_out):
    merged = jax.nn.sigmoid(gate_a) * (y_a @ proj_a) + jax.nn.sigmoid(gate_b) * (y_b @ proj_b)
    return merged @ w_out


def moe(h, router_w, router_bias, w1, w3, w2):
    f = jnp.float32
    shp = h.shape
    t = h.reshape(-1, shp[-1])
    n = t.shape[0]
    scores = jax.nn.sigmoid(t.astype(f) @ router_w.astype(f))
    sel = scores + router_bias.astype(f)
    grp_score = lax.top_k(sel.reshape(n, N_GROUPS, EXPERTS_PER_GROUP), TOP_K)[0].sum(-1)
    best = jnp.argmax(grp_score, axis=-1)
    in_group = (jnp.arange(N_EXPERTS) // EXPERTS_PER_GROUP)[None, :] == best[:, None]
    _, top_idx = lax.top_k(jnp.where(in_group, sel, -jnp.inf), TOP_K)
    top_s = jnp.take_along_axis(scores, top_idx, axis=-1)
    wts = top_s / jnp.sum(top_s, axis=-1, keepdims=True)
    gate = jnp.sum(jax.nn.one_hot(top_idx, N_EXPERTS, dtype=f) * wts[..., None], axis=1).astype(t.dtype)
    out = jnp.zeros_like(t)
    for e in range(N_EXPERTS):
        hid = jax.nn.silu(t @ w1[e]) * (t @ w3[e])
        out = out + gate[:, e:e + 1] * (hid @ w2[e])
    return out.reshape(shp)


def setup_inputs(seed: int = 0) -> dict:
    key = jax.random.key(seed)
    ks = iter(jax.random.split(key, 40))
    D = D_MODEL

    def nrm(shape, s):
        return jax.random.normal(next(ks), shape, jnp.float32) * s

    inp = {}
    inp['x'] = nrm((BATCH, SEQ, D), 1.0)
    inp['c'] = nrm((BATCH, D), 1.0)
    inp['ctx'] = nrm((BATCH, CTX_LEN, D), 1.0)
    inp['c_ctx'] = nrm((D,), 1.0)
    inp['mod_w'] = nrm((DEPTH, D, 6 * D), 0.5 * D ** -0.5)
    inp['mod_b'] = nrm((DEPTH, 6 * D), 0.02)
    inp['norm1_g'] = 1.0 + nrm((DEPTH, D), 0.05)
    inp['norm2_g'] = 1.0 + nrm((DEPTH, D), 0.05)
    inp['w_in'] = nrm((DEPTH, D, IN_COLS), D ** -0.5)
    inp['rw_w0'] = jax.random.uniform(next(ks), (DEPTH, 2, D_A), jnp.float32, -6.0, 1.0)
    inp['rw_w_lora_b'] = nrm((DEPTH, 2, LORA_W, D_A), 0.5 * LORA_W ** -0.5)
    inp['rw_a0'] = nrm((DEPTH, 2, D_A), 0.5)
    inp['rw_a_lora_b'] = nrm((DEPTH, 2, LORA_A, D_A), 0.5 * LORA_A ** -0.5)
    inp['rw_g_lora_b'] = nrm((DEPTH, LORA_G, D_A), LORA_G ** -0.5)
    inp['rw_k_k'] = 0.85 + nrm((DEPTH, D_A), 0.05)
    inp['rw_k_a'] = 1.0 + nrm((DEPTH, D_A), 0.05)
    inp['rw_r_k'] = nrm((DEPTH, A_HEADS, A_HEAD_DIM), 0.1)
    inp['rw_ln_g'] = 1.0 + nrm((DEPTH, D_A), 0.05)
    inp['rw_ln_b'] = nrm((DEPTH, D_A), 0.02)
    inp['na_q_g'] = 1.0 + nrm((DEPTH, B_HEAD_DIM), 0.1)
    inp['na_k_g'] = 1.0 + nrm((DEPTH, B_HEAD_DIM), 0.1)
    inp['na_rpb'] = nrm((DEPTH, B_HEADS, 2 * WIN_H - 1, 2 * WIN_W - 1), 0.5)
    inp['proj_a'] = nrm((DEPTH, D_A, D), D_A ** -0.5)
    inp['proj_b'] = nrm((DEPTH, D_B, D), D_B ** -0.5)
    inp['w_out'] = nrm((DEPTH, D, D), D ** -0.5)
    inp['router_w'] = nrm((D, N_EXPERTS), D ** -0.5)
    inp['router_bias'] = nrm((N_EXPERTS,), 0.01)
    inp['moe_w1'] = nrm((DEPTH, N_EXPERTS, D, D_EXPERT), D ** -0.5)
    inp['moe_w3'] = nrm((DEPTH, N_EXPERTS, D, D_EXPERT), D ** -0.5)
    inp['moe_w2'] = nrm((DEPTH, N_EXPERTS, D_EXPERT, D), D_EXPERT ** -0.5)
    return inp


def reference(x, c, ctx, c_ctx, mod_w, mod_b, norm1_g, norm2_g, w_in,
              rw_w0, rw_w_lora_b, rw_a0, rw_a_lora_b, rw_g_lora_b, rw_k_k, rw_k_a, rw_r_k,
              rw_ln_g, rw_ln_b, na_q_g, na_k_g, na_rpb, proj_a, proj_b, w_out,
              router_w, router_bias, moe_w1, moe_w3, moe_w2):
    for l in range(DEPTH):
        last = l == DEPTH - 1
        m_x = [m[:, None, :] for m in jnp.split(jax.nn.silu(c) @ mod_w[l] + mod_b[l], 6, axis=-1)]
        m_c = jnp.split(jax.nn.silu(c_ctx) @ mod_w[l] + mod_b[l], 6, axis=-1)

        h_x = modulate(rms_norm(x, norm1_g[l]), m_x[0], m_x[1])
        h_c = modulate(rms_norm(ctx, norm1_g[l]), m_c[0], m_c[1])
        z_x = split_cols(h_x @ w_in[l])
        z_c = split_cols(h_c @ w_in[l])
        ya_c, ya_x = rwkv_mixer(z_c, z_x, rw_w0[l], rw_w_lora_b[l], rw_a0[l], rw_a_lora_b[l],
                                rw_g_lora_b[l], rw_k_k[l], rw_k_a[l], rw_r_k[l], rw_ln_g[l], rw_ln_b[l],
                                emit_ctx=not last)
        yb_c, yb_x = na_mixer(z_c, z_x, na_q_g[l], na_k_g[l], na_rpb[l], need_ctx=not last)
        x = x + m_x[2] * branch_merge(z_x[9], z_x[10], ya_x, yb_x, proj_a[l], proj_b[l], w_out[l])

        h2_x = modulate(rms_norm(x, norm2_g[l]), m_x[3], m_x[4])
        x = x + m_x[5] * moe(h2_x, router_w, router_bias, moe_w1[l], moe_w3[l], moe_w2[l])

        if not last:
            ctx = ctx + m_c[2] * branch_merge(z_c[9], z_c[10], ya_c, yb_c, proj_a[l], proj_b[l], w_out[l])
            h2_c = modulate(rms_norm(ctx, norm2_g[l]), m_c[3], m_c[4])
            ctx = ctx + m_c[5] * moe(h2_c, router_w, router_bias, moe_w1[l], moe_w3[l], moe_w2[l])
    return x
```

```python
import functools
import math

import jax
import jax.numpy as jnp
from jax import lax
from jax.experimental import pallas as pl
from jax.experimental.pallas import tpu as pltpu

F32 = jnp.float32
BF16 = jnp.bfloat16

HEAD_DIM = 64
GRID_W = 64
WIN_H = 8
WIN_W = 16
N_GROUPS = 4
TOP_K = 2
RMS_EPS = 1e-6
GN_EPS = 64e-5
LANE = 128
CHUNK = 64
SUB = 16
NEG_BIG = -1e30
DECAY_SCALE = math.exp(-0.5)
VMEM_LIMIT = 48 * 1024 * 1024


def _bdot(a, b):
    return jnp.dot(a.astype(BF16), b.astype(BF16), preferred_element_type=F32)


def _bdot_nt(a, b):
    return lax.dot_general(a.astype(BF16), b.astype(BF16), (((1,), (1,)), ((), ())),
                           preferred_element_type=F32)


def _bdot_tn(a, b):
    return lax.dot_general(a.astype(BF16), b.astype(BF16), (((0,), (0,)), ((), ())),
                           preferred_element_type=F32)


def _split(x):
    hi = x.astype(BF16)
    lo = (x - hi.astype(F32)).astype(BF16)
    return hi, lo


def _dot3(a, b):
    ah, al = _split(a)
    bh, bl = _split(b)
    return (jnp.dot(ah, bh, preferred_element_type=F32)
            + jnp.dot(al, bh, preferred_element_type=F32)
            + jnp.dot(ah, bl, preferred_element_type=F32))


def _sigmoid(x):
    return 1.0 / (1.0 + jnp.exp(-x))


def _group_sum(x, g128):
    parts = [_bdot(x[:, j * LANE:(j + 1) * LANE], g128) for j in range(x.shape[1] // LANE)]
    return jnp.concatenate(parts, axis=1)


def _group_sum2(x, g128):
    hi, lo = _split(x)
    parts = []
    for j in range(x.shape[1] // LANE):
        sl = slice(j * LANE, (j + 1) * LANE)
        parts.append(jnp.dot(hi[:, sl], g128, preferred_element_type=F32)
                     + jnp.dot(lo[:, sl], g128, preferred_element_type=F32))
    return jnp.concatenate(parts, axis=1)


def _mod_row(mod_ref, idx, is_ctx):
    mx = mod_ref[0, 1, idx:idx + 1, :]
    if is_ctx is None:
        return mx
    return jnp.where(is_ctx, mod_ref[0, 0, idx:idx + 1, :], mx)


def _ctx_rows(tile, tm, n_ctx):
    if n_ctx == 0:
        return None
    rows = tile * tm + lax.broadcasted_iota(jnp.int32, (tm, 1), 0)
    return rows < n_ctx


def _rms_mod(x, gain, shift, scale):
    xn = x * lax.rsqrt(jnp.mean(x * x, axis=-1, keepdims=True) + RMS_EPS) * gain
    return xn * (1.0 + scale) + shift


def _mod_kernel(c_ref, w_ref, b_ref, o_ref):
    c = c_ref[...]
    o_ref[0] = _dot3(c * _sigmoid(c), w_ref[0]) + b_ref[0]


def _mod_call(cs, mod_w, mod_b):
    depth, d, n = mod_w.shape
    tn = n // 4
    return pl.pallas_call(
        _mod_kernel,
        out_shape=jax.ShapeDtypeStruct((depth, cs.shape[0], n), F32),
        grid=(depth, n // tn),
        in_specs=[pl.BlockSpec(cs.shape, lambda l, j: (0, 0)),
                  pl.BlockSpec((1, d, tn), lambda l, j: (l, 0, j)),
                  pl.BlockSpec((1, 1, tn), lambda l, j: (l, 0, j))],
        out_specs=pl.BlockSpec((1, cs.shape[0], tn), lambda l, j: (l, 0, j)),
        compiler_params=pltpu.CompilerParams(dimension_semantics=("parallel", "parallel"),
                                             vmem_limit_bytes=VMEM_LIMIT),
        name="adaln_vectors",
    )(cs, mod_w, mod_b.reshape(depth, 1, n))


def _in_proj_kernel(x_ref, mod_ref, g_ref, w_ref, kkg_ref, qg_ref, kg_ref, gs_ref,
                    rkv_ref, kk_ref, lora_ref, qkv_ref, gates_ref, *, tm, n_ctx, d_a, d_b, n_lora):
    t = pl.program_id(1)
    is_ctx = _ctx_rows(t, tm, n_ctx)
    h = _rms_mod(x_ref[0], g_ref[...], _mod_row(mod_ref, 0, is_ctx), _mod_row(mod_ref, 1, is_ctx))
    h = h.astype(BF16)
    gs = gs_ref[...]

    o = 0
    rkv = jnp.dot(h, w_ref[:, o:o + 3 * d_a], preferred_element_type=F32)
    rkv_ref[0] = rkv
    kkv = rkv[:, d_a:2 * d_a] * kkg_ref[...]
    ss = _group_sum2(kkv * kkv, gs)
    kk_ref[0] = kkv * lax.rsqrt(jnp.maximum(ss, 1e-12))
    o += 3 * d_a

    lora_ref[0] = jnp.dot(h, w_ref[:, o:o + n_lora], preferred_element_type=F32)
    o += n_lora

    qkv = jnp.dot(h, w_ref[:, o:o + 3 * d_b], preferred_element_type=F32)
    q, k = qkv[:, :d_b], qkv[:, d_b:2 * d_b]
    inv_n = 1.0 / HEAD_DIM
    qn = q * lax.rsqrt(_group_sum(q * q, gs) * inv_n + RMS_EPS) * qg_ref[...] * (HEAD_DIM ** -0.5)
    kn = k * lax.rsqrt(_group_sum(k * k, gs) * inv_n + RMS_EPS) * kg_ref[...]
    qkv_ref[0] = jnp.concatenate([qn, kn, qkv[:, 2 * d_b:]], axis=1).astype(BF16)
    o += 3 * d_b

    gates_ref[0] = _sigmoid(jnp.dot(h, w_ref[:, o:], preferred_element_type=F32)).astype(BF16)


def _in_proj_call(xs, modsel, norm_g, w_in, k_k, q_g, k_g, gs, *, n_ctx, d_a, d_b, n_lora, tm=256):
    bsz, tt, d = xs.shape
    n_gate = w_in.shape[1] - 3 * d_a - n_lora - 3 * d_b
    row = lambda b, t: (b, t, 0)
    const = lambda b, t: (0, 0)
    return pl.pallas_call(
        functools.partial(_in_proj_kernel, tm=tm, n_ctx=n_ctx, d_a=d_a, d_b=d_b, n_lora=n_lora),
        out_shape=(jax.ShapeDtypeStruct((bsz, tt, 3 * d_a), F32),
                   jax.ShapeDtypeStruct((bsz, tt, d_a), F32),
                   jax.ShapeDtypeStruct((bsz, tt, n_lora), F32),
                   jax.ShapeDtypeStruct((bsz, tt, 3 * d_b), BF16),
                   jax.ShapeDtypeStruct((bsz, tt, n_gate), BF16)),
        grid=(bsz, tt // tm),
        in_specs=[pl.BlockSpec((1, tm, d), row),
                  pl.BlockSpec((1, 2, 8, d), lambda b, t: (b, 0, 0, 0)),
                  pl.BlockSpec((1, d), const),
                  pl.BlockSpec(w_in.shape, const),
                  pl.BlockSpec((1, d_a), const),
                  pl.BlockSpec((1, d_b), const),
                  pl.BlockSpec((1, d_b), const),
                  pl.BlockSpec((LANE, LANE), const)],
        out_specs=(pl.BlockSpec((1, tm, 3 * d_a), row),
                   pl.BlockSpec((1, tm, d_a), row),
                   pl.BlockSpec((1, tm, n_lora), row),
                   pl.BlockSpec((1, tm, 3 * d_b), row),
                   pl.BlockSpec((1, tm, n_gate), row)),
        compiler_params=pltpu.CompilerParams(dimension_semantics=("parallel", "parallel"),
                                             vmem_limit_bytes=VMEM_LIMIT),
        name="in_proj",
    )(xs, modsel, norm_g, w_in, k_k, q_g, k_g, gs)


def _tri_inverse(a, eye, blk):
    ad = jnp.where(blk, a, 0.0)
    ao = a - ad
    x = ad
    td = eye + ad
    for _ in range(int(math.log2(SUB)) - 1):
        x = _dot3(x, x)
        td = td + _dot3(td, x)
    n1 = _dot3(td, ao)
    tf = eye + n1
    x = n1
    for _ in range(int(math.log2(CHUNK // SUB)) - 1):
        x = _dot3(x, x)
        tf = tf + _dot3(tf, x)
    return _dot3(tf, td)


def _rwkv_kernel(rkv_ref, kk_ref, lora_ref, w0_ref, wlb_ref, a0_ref, alb_ref, ka_ref,
                 y_ref, st_ref, *, d_a, lora_w, lora_a):
    d = pl.program_id(1)
    j = pl.program_id(2)
    n = HEAD_DIM
    heads = d_a // n
    c = CHUNK

    @pl.when(j == 0)
    def _():
        st_ref[...] = jnp.zeros_like(st_ref)

    fwd = d == 0
    sgn = jnp.where(fwd, 1, -1)
    rkv = rkv_ref[0]
    r, k, v = rkv[:, :d_a], rkv[:, d_a:2 * d_a], rkv[:, 2 * d_a:]
    kk = kk_ref[0]
    lora = lora_ref[0]
    lw = jnp.where(fwd, lora[:, :lora_w], lora[:, lora_w:2 * lora_w])
    la = jnp.where(fwd, lora[:, 2 * lora_w:2 * lora_w + lora_a],
                   lora[:, 2 * lora_w + lora_a:2 * lora_w + 2 * lora_a])
    wl = w0_ref[0] + _dot3(jnp.tanh(lw), wlb_ref[0])
    ld = -DECAY_SCALE * _sigmoid(wl)
    al = _sigmoid(a0_ref[0] + _dot3(la, alb_ref[0]))
    kd = k * (1.0 + (al - 1.0) * ka_ref[...])

    row = lax.broadcasted_iota(jnp.int32, (c, c), 0)
    col = lax.broadcasted_iota(jnp.int32, (c, c), 1)
    rel = (row - col) * sgn
    incl = rel >= 0
    strict = rel > 0
    eye = jnp.where(row == col, 1.0, 0.0)
    sub_bits = int(math.log2(SUB))
    blk = (row >> sub_bits) == (col >> sub_bits)

    tri = jnp.where(incl, 1.0, 0.0).astype(BF16)
    ld_hi, ld_lo = _split(ld)
    cum = (jnp.dot(tri, ld_hi, preferred_element_type=F32)
           + jnp.dot(tri, ld_lo, preferred_element_type=F32))
    p_in = jnp.exp(cum)
    p_ex = jnp.exp(cum - ld)
    p_inv = jnp.exp(-cum)
    p_end = jnp.exp(jnp.where(fwd, cum[c - 1:c, :], cum[0:1, :]))

    at = -kk * p_ex
    rt = r * p_in
    bt = kk * al * p_inv
    kt = kd * p_inv
    ar = jnp.concatenate([at, rt], axis=0).astype(BF16)
    bt_end = (bt * p_end).astype(BF16)
    kt_end = (kt * p_end).astype(BF16)
    bt = bt.astype(BF16)
    kt = kt.astype(BF16)
    vb = v.astype(BF16)

    ys = []
    for h in range(heads):
        sl = slice(h * n, (h + 1) * n)
        ar_h = ar[:, sl]
        mb = _bdot_nt(ar_h, bt[:, sl])
        mk = _bdot_nt(ar_h, kt[:, sl])
        a_ab = jnp.where(strict, mb[:c], 0.0)
        a_rb = jnp.where(incl, mb[c:], 0.0)
        a_ak = jnp.where(strict, mk[:c], 0.0)
        a_rk = jnp.where(incl, mk[c:], 0.0)
        t_inv = _tri_inverse(a_ab, eye, blk)

        s = st_ref[h]
        v_h = vb[:, sl]
        ms = _bdot_nt(ar_h, s)
        u = _bdot(t_inv, ms[:c] + _bdot(a_ak, v_h))
        ys.append(ms[c:] + _bdot(a_rb, u) + _bdot(a_rk, v_h))
        st_ref[h] = (s * p_end[:, sl] + _bdot_tn(u, bt_end[:, sl]) + _bdot_tn(v_h, kt_end[:, sl]))
    y_ref[0, 0] = jnp.concatenate(ys, axis=1)


def _rwkv_call(rkv, kk, lora, w0, wlb, a0, alb, k_a, *, n_ctx, lora_w, lora_a):
    bsz, tt, d3 = rkv.shape
    d_a = d3 // 3
    n_chunks = tt // CHUNK
    nc_ctx = n_ctx // CHUNK

    def chunk(d, j):
        rev = jnp.where(j < nc_ctx, nc_ctx - 1 - j, n_chunks - 1 - (j - nc_ctx))
        return jnp.where(d == 0, j, rev)

    tok = lambda b, d, j: (b, chunk(d, j), 0)
    per_dir = lambda b, d, j: (d, 0, 0)
    return pl.pallas_call(
        functools.partial(_rwkv_kernel, d_a=d_a, lora_w=lora_w, lora_a=lora_a),
        out_shape=jax.ShapeDtypeStruct((2, bsz, tt, d_a), F32),
        grid=(bsz, 2, n_chunks),
        in_specs=[pl.BlockSpec((1, CHUNK, d3), tok),
                  pl.BlockSpec((1, CHUNK, d_a), tok),
                  pl.BlockSpec((1, CHUNK, lora.shape[2]), tok),
                  pl.BlockSpec((1, 1, d_a), per_dir),
                  pl.BlockSpec((1, lora_w, d_a), per_dir),
                  pl.BlockSpec((1, 1, d_a), per_dir),
                  pl.BlockSpec((1, lora_a, d_a), per_dir),
                  pl.BlockSpec((1, d_a), lambda b, d, j: (0, 0))],
        out_specs=pl.BlockSpec((1, 1, CHUNK, d_a), lambda b, d, j: (d, b, chunk(d, j), 0)),
        scratch_shapes=[pltpu.VMEM((d_a // HEAD_DIM, HEAD_DIM, HEAD_DIM), F32)],
        compiler_params=pltpu.CompilerParams(
            dimension_semantics=("parallel", "parallel", "arbitrary"),
            vmem_limit_bytes=VMEM_LIMIT),
        name="rwkv7_chunk_scan",
    )(rkv, kk, lora, w0.reshape(2, 1, d_a), wlb, a0.reshape(2, 1, d_a), alb, k_a)


def _softmax_pv(parts):
    m = functools.reduce(jnp.maximum, [jnp.max(s, axis=-1, keepdims=True) for s, _ in parts])
    num, den = 0.0, 0.0
    for s, vals in parts:
        p = jnp.exp(s - m)
        den = den + jnp.sum(p, axis=-1, keepdims=True)
        num = num + jnp.dot(p.astype(BF16), vals, preferred_element_type=F32)
    return num / den


def _attn_kernel(q_ref, k_ref, v_ref, bias_ref, o_ref, *, n_ctx, rows, q_off, d_b):
    n = HEAD_DIM
    heads = d_b // n
    nq_ctx = n_ctx // GRID_W
    i = pl.program_id(1) + q_off
    q = q_ref[0]
    kc = k_ref[0, 0:n_ctx, :]
    vc = v_ref[0, 0:n_ctx, :]

    def latent():
        li = i - nq_ctx
        r0 = jnp.clip(li - WIN_H // 2, 0, rows - WIN_H)
        start = pl.multiple_of(n_ctx + r0 * GRID_W, GRID_W)
        kl = k_ref[0, pl.ds(start, WIN_H * GRID_W), :]
        vl = v_ref[0, pl.ds(start, WIN_H * GRID_W), :]
        outs = []
        for h in range(heads):
            sl = slice(h * n, (h + 1) * n)
            s_l = _bdot_nt(q[:, sl], kl[:, sl]) + bias_ref[0, h]
            s_c = _bdot_nt(q[:, sl], kc[:, sl])
            outs.append(_softmax_pv([(s_l, vl[:, sl]), (s_c, vc[:, sl])]))
        o_ref[0] = jnp.concatenate(outs, axis=1).astype(o_ref.dtype)

    def context():
        outs = []
        for h in range(heads):
            sl = slice(h * n, (h + 1) * n)
            outs.append(_softmax_pv([(_bdot_nt(q[:, sl], kc[:, sl]), vc[:, sl])]))
        o_ref[0] = jnp.concatenate(outs, axis=1).astype(o_ref.dtype)

    if q_off == 0:
        pl.when(i >= nq_ctx)(latent)
        pl.when(i < nq_ctx)(context)
    else:
        latent()


def _bias_table(rpb):
    j = jnp.arange(GRID_W)
    col_start = jnp.clip(j - WIN_W // 2, 0, GRID_W - WIN_W)
    in_win = (j[None, :] >= col_start[:, None]) & (j[None, :] < col_start[:, None] + WIN_W)
    d_col = jnp.clip(j[None, :] - j[:, None], -(WIN_W - 1), WIN_W - 1) + (WIN_W - 1)
    tz = jnp.where(in_win[None, None], rpb[:, :, d_col].astype(F32), NEG_BIG)
    idx = jnp.arange(WIN_H)[:, None] + jnp.arange(WIN_H)[None, :]
    bt = tz[:, idx]
    heads = rpb.shape[0]
    return bt.transpose(1, 0, 3, 2, 4).reshape(WIN_H, heads, GRID_W, WIN_H * GRID_W)


def _attn_call(qkv, bias_tab, *, n_ctx, with_ctx):
    bsz, tt, d3 = qkv.shape
    d_b = d3 // 3
    heads = d_b // HEAD_DIM
    rows = (tt - n_ctx) // GRID_W
    nq_ctx = n_ctx // GRID_W
    q_off = 0 if with_ctx else nq_ctx
    n_q = tt // GRID_W - q_off

    def variant(b, i):
        li = jnp.maximum(i + q_off - nq_ctx, 0)
        return (jnp.clip(li - WIN_H // 2, 0, rows - WIN_H) - li + WIN_H - 1, 0, 0, 0)

    return pl.pallas_call(
        functools.partial(_attn_kernel, n_ctx=n_ctx, rows=rows, q_off=q_off, d_b=d_b),
        out_shape=jax.ShapeDtypeStruct((bsz, n_q * GRID_W, d_b), BF16),
        grid=(bsz, n_q),
        in_specs=[pl.BlockSpec((1, GRID_W, d_b), lambda b, i: (b, i + q_off, 0)),
                  pl.BlockSpec((1, tt, d_b), lambda b, i: (b, 0, 1)),
                  pl.BlockSpec((1, tt, d_b), lambda b, i: (b, 0, 2)),
                  pl.BlockSpec((1, heads, GRID_W, WIN_H * GRID_W), variant)],
        out_specs=pl.BlockSpec((1, GRID_W, d_b), lambda b, i: (b, i, 0)),
        compiler_params=pltpu.CompilerParams(dimension_semantics=("parallel", "arbitrary"),
                                             vmem_limit_bytes=VMEM_LIMIT),
        name="neighbourhood_attention",
    )(qkv, qkv, qkv, bias_tab)


def _merge_kernel(x_ref, mod_ref, y0_ref, y1_ref, rkv_ref, lora_ref, yb_ref, gates_ref,
                  rk_ref, lng_ref, lnb_ref, glb_ref, pa_ref, pb_ref, wo_ref, gs_ref, o_ref,
                  *, tm, n_ctx, t_off, d_a, lora_g):
    t = pl.program_id(1) + t_off
    is_ctx = _ctx_rows(t, tm, n_ctx)
    gs = gs_ref[...]
    inv_n = 1.0 / HEAD_DIM
    y = y0_ref[0, 0] + y1_ref[0, 0]
    mu = _group_sum2(y, gs) * inv_n
    yc = y - mu
    var = _group_sum2(yc * yc, gs) * inv_n
    yn = yc * lax.rsqrt(var + GN_EPS) * lng_ref[...] + lnb_ref[...]
    rkv = rkv_ref[0]
    r, k, v = rkv[:, :d_a], rkv[:, d_a:2 * d_a], rkv[:, 2 * d_a:]
    bonus = _group_sum2(r * k * rk_ref[...], gs) * v
    lg = lora_ref[0][:, lora_ref.shape[2] - lora_g:]
    g = _bdot(_sigmoid(lg), glb_ref[...])
    ya = (yn + bonus) * g
    gates = gates_ref[0].astype(F32)
    dm = gates.shape[1] // 2
    merged = gates[:, :dm] * _bdot(ya, pa_ref[...]) + gates[:, dm:] * _bdot(yb_ref[0], pb_ref[...])
    o_ref[0] = x_ref[0] + _mod_row(mod_ref, 2, is_ctx) * _bdot(merged, wo_ref[...])


def _merge_call(xs, modsel, y2, rkv, lora, yb, gates, r_k, ln_g, ln_b, glb, pa, pb, wo, gs,
                *, n_ctx, with_ctx, lora_g, tm=256):
    bsz, tt, d = xs.shape
    d_a = rkv.shape[2] // 3
    t_off = 0 if with_ctx else n_ctx // tm
    n_t = tt // tm - t_off
    yb_off = 0 if with_ctx else -t_off
    row = lambda b, t: (b, t + t_off, 0)
    const = lambda b, t: (0, 0)
    return pl.pallas_call(
        functools.partial(_merge_kernel, tm=tm, n_ctx=n_ctx, t_off=t_off, d_a=d_a, lora_g=lora_g),
        out_shape=jax.ShapeDtypeStruct((bsz, n_t * tm, d), F32),
        grid=(bsz, n_t),
        in_specs=[pl.BlockSpec((1, tm, d), row),
                  pl.BlockSpec((1, 2, 8, d), lambda b, t: (b, 0, 0, 0)),
                  pl.BlockSpec((1, 1, tm, d_a), lambda b, t: (0, b, t + t_off, 0)),
                  pl.BlockSpec((1, 1, tm, d_a), lambda b, t: (1, b, t + t_off, 0)),
                  pl.BlockSpec((1, tm, 3 * d_a), row),
                  pl.BlockSpec((1, tm, lora.shape[2]), row),
                  pl.BlockSpec((1, tm, yb.shape[2]), lambda b, t: (b, t + t_off + yb_off, 0)),
                  pl.BlockSpec((1, tm, gates.shape[2]), row),
                  pl.BlockSpec((1, d_a), const),
                  pl.BlockSpec((1, d_a), const),
                  pl.BlockSpec((1, d_a), const),
                  pl.BlockSpec(glb.shape, const),
                  pl.BlockSpec(pa.shape, const),
                  pl.BlockSpec(pb.shape, const),
                  pl.BlockSpec(wo.shape, const),
                  pl.BlockSpec((LANE, LANE), const)],
        out_specs=pl.BlockSpec((1, tm, d), lambda b, t: (b, t, 0)),
        compiler_params=pltpu.CompilerParams(dimension_semantics=("parallel", "parallel"),
                                             vmem_limit_bytes=VMEM_LIMIT),
        name="branch_merge",
    )(xs, modsel, y2, y2, rkv, lora, yb, gates, r_k, ln_g, ln_b, glb, pa, pb, wo, gs)


def _route(sel, scores, n_experts):
    per = n_experts // N_GROUPS
    in_top = []
    for e in range(n_experts):
        g0 = (e // per) * per
        rank = 0.0
        for o in range(g0, g0 + per):
            if o == e:
                continue
            ahead = (sel[o] >= sel[e]) if o < e else (sel[o] > sel[e])
            rank = rank + jnp.where(ahead, 1.0, 0.0)
        in_top.append(rank < TOP_K)
    grp = [sum(jnp.where(in_top[e], sel[e], 0.0) for e in range(g * per, (g + 1) * per))
           for g in range(N_GROUPS)]
    best = jnp.zeros_like(grp[0], dtype=jnp.int32)
    best_s = grp[0]
    for g in range(1, N_GROUPS):
        better = grp[g] > best_s
        best = jnp.where(better, g, best)
        best_s = jnp.where(better, grp[g], best_s)
    chosen = [in_top[e] & (best == e // per) for e in range(n_experts)]
    den = sum(jnp.where(chosen[e], scores[e], 0.0) for e in range(n_experts))
    return [jnp.where(chosen[e], scores[e] / den, 0.0) for e in range(n_experts)]


def _moe_kernel(x_ref, mod_ref, g_ref, rw_ref, rb_ref, w1_ref, w3_ref, w2_ref, o_ref,
                h_ref, gate_ref, acc_ref, *, tm, n_ctx, n_experts):
    t = pl.program_id(1)
    e = pl.program_id(2)
    is_ctx = _ctx_rows(t, tm, n_ctx)

    @pl.when(e == 0)
    def _():
        h = _rms_mod(x_ref[0], g_ref[...], _mod_row(mod_ref, 3, is_ctx), _mod_row(mod_ref, 4, is_ctx))
        h_ref[...] = h.astype(BF16)
        hh, hl = _split(h)
        wh, wl = _split(rw_ref[...])
        nt = lambda a, b: lax.dot_general(a, b, (((1,), (1,)), ((), ())), preferred_element_type=F32)
        logits = nt(wh, hh) + nt(wh, hl) + nt(wl, hh)
        scores = _sigmoid(logits)
        sel = scores + rb_ref[...]
        gates = _route([sel[i:i + 1, :] for i in range(n_experts)],
                       [scores[i:i + 1, :] for i in range(n_experts)], n_experts)
        gt = jnp.concatenate(gates + [jnp.zeros((LANE - n_experts, tm), F32)], axis=0)
        gate_ref[...] = gt.T
        acc_ref[...] = jnp.zeros_like(acc_ref)

    h = h_ref[...]
    a = jnp.dot(h, w1_ref[0], preferred_element_type=F32)
    b = jnp.dot(h, w3_ref[0], preferred_element_type=F32)
    hid = (a * _sigmoid(a) * b).astype(BF16)
    lane = lax.broadcasted_iota(jnp.int32, (tm, LANE), 1)
    gcol = jnp.sum(jnp.where(lane == e, gate_ref[...], 0.0), axis=-1, keepdims=True)
    acc_ref[...] += gcol * jnp.dot(hid, w2_ref[0], preferred_element_type=F32)

    @pl.when(e == n_experts - 1)
    def _():
        o_ref[0] = x_ref[0] + _mod_row(mod_ref, 5, is_ctx) * acc_ref[...]


def _moe_call(xs, modsel, norm_g, router_wt, router_b, w1, w3, w2, *, n_ctx, tm):
    bsz, tt, d = xs.shape
    n_experts = router_wt.shape[0]
    row = lambda b, t, e: (b, t, 0)
    const = lambda b, t, e: (0, 0)
    wexp = lambda b, t, e: (e, 0, 0)
    return pl.pallas_call(
        functools.partial(_moe_kernel, tm=tm, n_ctx=n_ctx, n_experts=n_experts),
        out_shape=jax.ShapeDtypeStruct((bsz, tt, d), F32),
        grid=(bsz, tt // tm, n_experts),
        in_specs=[pl.BlockSpec((1, tm, d), row),
                  pl.BlockSpec((1, 2, 8, d), lambda b, t, e: (b, 0, 0, 0)),
                  pl.BlockSpec((1, d), const),
                  pl.BlockSpec(router_wt.shape, const),
                  pl.BlockSpec((n_experts, 1), const),
                  pl.BlockSpec((1,) + w1.shape[1:], wexp),
                  pl.BlockSpec((1,) + w3.shape[1:], wexp),
                  pl.BlockSpec((1,) + w2.shape[1:], wexp)],
        out_specs=pl.BlockSpec((1, tm, d), row),
        scratch_shapes=[pltpu.VMEM((tm, d), BF16),
                        pltpu.VMEM((tm, LANE), F32),
                        pltpu.VMEM((tm, d), F32)],
        compiler_params=pltpu.CompilerParams(
            dimension_semantics=("parallel", "parallel", "arbitrary"),
            vmem_limit_bytes=VMEM_LIMIT),
        name="moe",
    )(xs, modsel, norm_g, router_wt, router_b, w1, w3, w2)


def kernel(x, c, ctx, c_ctx, mod_w, mod_b, norm1_g, norm2_g, w_in, rw_w0, rw_w_lora_b, rw_a0, rw_a_lora_b,
           rw_g_lora_b, rw_k_k, rw_k_a, rw_r_k, rw_ln_g, rw_ln_b, na_q_g, na_k_g, na_rpb, proj_a, proj_b,
           w_out, router_w, router_bias, moe_w1, moe_w3, moe_w2):
    bsz, seq, d = x.shape
    n_ctx = ctx.shape[1]
    depth = mod_w.shape[0]
    d_a = rw_w0.shape[2]
    d_b = proj_b.shape[1]
    lora_w = rw_w_lora_b.shape[2]
    lora_a = rw_a_lora_b.shape[2]
    lora_g = rw_g_lora_b.shape[1]
    n_lora = 2 * lora_w + 2 * lora_a + lora_g
    heads_a = d_a // HEAD_DIM
    heads_b = d_b // HEAD_DIM
    assert seq % (GRID_W * WIN_H) == 0 and n_ctx % 256 == 0 and bsz + 1 <= 16

    cs = jnp.zeros((16, d), F32).at[:bsz].set(c).at[bsz].set(c_ctx)
    mod = _mod_call(cs, mod_w, mod_b).reshape(depth, 16, 6, d)

    lane = jnp.arange(LANE)
    gs = (lane[:, None] // HEAD_DIM == lane[None, :] // HEAD_DIM).astype(BF16)
    router_wt = router_w.T
    router_b = router_bias.reshape(-1, 1)

    xs = jnp.concatenate([ctx, x], axis=1)
    for l in range(depth):
        last = l == depth - 1
        m_c = jnp.broadcast_to(mod[l, bsz][None], (bsz, 6, d))
        modsel = jnp.pad(jnp.stack([m_c, mod[l, :bsz]], axis=1), ((0, 0), (0, 0), (0, 2), (0, 0)))

        rkv, kk, lora, qkv, gates = _in_proj_call(
            xs, modsel, norm1_g[l][None], w_in[l].astype(BF16), rw_k_k[l][None],
            jnp.tile(na_q_g[l], heads_b)[None], jnp.tile(na_k_g[l], heads_b)[None], gs,
            n_ctx=n_ctx, d_a=d_a, d_b=d_b, n_lora=n_lora)
        y2 = _rwkv_call(rkv, kk, lora, rw_w0[l], rw_w_lora_b[l], rw_a0[l], rw_a_lora_b[l],
                        rw_k_a[l][None], n_ctx=n_ctx, lora_w=lora_w, lora_a=lora_a)
        yb = _attn_call(qkv, _bias_table(na_rpb[l]), n_ctx=n_ctx, with_ctx=not last)
        xs = _merge_call(xs, modsel, y2, rkv, lora, yb, gates, rw_r_k[l].reshape(1, d_a),
                         rw_ln_g[l][None], rw_ln_b[l][None], rw_g_lora_b[l].astype(BF16),
                         proj_a[l].astype(BF16), proj_b[l].astype(BF16), w_out[l].astype(BF16), gs,
                         n_ctx=n_ctx, with_ctx=not last, lora_g=lora_g)
        moe_ctx = 0 if last else n_ctx
        tm = next(m for m in (1024, 768, 512, 256) if xs.shape[1] % m == 0)
        xs = _moe_call(xs, modsel, norm2_g[l][None], router_wt, router_b,
                       moe_w1[l].astype(BF16), moe_w3[l].astype(BF16), moe_w2[l].astype(BF16),
                       n_ctx=moe_ctx, tm=tm)
    return xs
```

```python
import functools
import math

import jax
import jax.numpy as jnp
from jax import lax
from jax.experimental import pallas as pl
from jax.experimental.pallas import tpu as pltpu

F32 = jnp.float32
BF16 = jnp.bfloat16

HEAD_DIM = 64
GRID_W = 64
WIN_H = 8
WIN_W = 16
N_GROUPS = 4
TOP_K = 2
RMS_EPS = 1e-6
GN_EPS = 64e-5
LANE = 128
CHUNK = 64
SUB = 16
NEG_BIG = -1e30
DECAY_SCALE = math.exp(-0.5)
VMEM_LIMIT = 48 * 1024 * 1024


def _bdot(a, b):
    return jnp.dot(a.astype(BF16), b.astype(BF16), preferred_element_type=F32)


def _bdot_nt(a, b):
    return lax.dot_general(a.astype(BF16), b.astype(BF16), (((1,), (1,)), ((), ())),
                           preferred_element_type=F32)


def _bdot_tn(a, b):
    return lax.dot_general(a.astype(BF16), b.astype(BF16), (((0,), (0,)), ((), ())),
                           preferred_element_type=F32)


def _split(x):
    hi = x.astype(BF16)
    lo = (x - hi.astype(F32)).astype(BF16)
    return hi, lo


def _dot3(a, b):
    ah, al = _split(a)
    bh, bl = _split(b)
    return (jnp.dot(ah, bh, preferred_element_type=F32)
            + jnp.dot(al, bh, preferred_element_type=F32)
            + jnp.dot(ah, bl, preferred_element_type=F32))


def _sigmoid(x):
    return 1.0 / (1.0 + jnp.exp(-x))


def _group_sum(x, g128):
    parts = [_bdot(x[:, j * LANE:(j + 1) * LANE], g128) for j in range(x.shape[1] // LANE)]
    return jnp.concatenate(parts, axis=1)


def _group_sum2(x, g128):
    hi, lo = _split(x)
    parts = []
    for j in range(x.shape[1] // LANE):
        sl = slice(j * LANE, (j + 1) * LANE)
        parts.append(jnp.dot(hi[:, sl], g128, preferred_element_type=F32)
                     + jnp.dot(lo[:, sl], g128, preferred_element_type=F32))
    return jnp.concatenate(parts, axis=1)


def _mod_row(mod_ref, idx, is_ctx):
    mx = mod_ref[0, 1, idx:idx + 1, :]
    if is_ctx is None:
        return mx
    return jnp.where(is_ctx, mod_ref[0, 0, idx:idx + 1, :], mx)


def _ctx_rows(tile, tm, n_ctx):
    if n_ctx == 0:
        return None
    rows = tile * tm + lax.broadcasted_iota(jnp.int32, (tm, 1), 0)
    return rows < n_ctx


def _rms_mod(x, gain, shift, scale):
    xn = x * lax.rsqrt(jnp.mean(x * x, axis=-1, keepdims=True) + RMS_EPS) * gain
    return xn * (1.0 + scale) + shift


def _mod_kernel(c_ref, w_ref, b_ref, o_ref):
    c = c_ref[...]
    o_ref[0] = _dot3(c * _sigmoid(c), w_ref[0]) + b_ref[0]


def _mod_call(cs, mod_w, mod_b):
    depth, d, n = mod_w.shape
    tn = n // 4
    return pl.pallas_call(
        _mod_kernel,
        out_shape=jax.ShapeDtypeStruct((depth, cs.shape[0], n), F32),
        grid=(depth, n // tn),
        in_specs=[pl.BlockSpec(cs.shape, lambda l, j: (0, 0)),
                  pl.BlockSpec((1, d, tn), lambda l, j: (l, 0, j)),
                  pl.BlockSpec((1, 1, tn), lambda l, j: (l, 0, j))],
        out_specs=pl.BlockSpec((1, cs.shape[0], tn), lambda l, j: (l, 0, j)),
        compiler_params=pltpu.CompilerParams(dimension_semantics=("parallel", "parallel"),
                                             vmem_limit_bytes=VMEM_LIMIT),
        name="adaln_vectors",
    )(cs, mod_w, mod_b.reshape(depth, 1, n))


def _in_proj_kernel(x_ref, mod_ref, g_ref, w_ref, kkg_ref, qg_ref, kg_ref, gs_ref,
                    rkv_ref, kk_ref, lora_ref, qkv_ref, gates_ref, *, tm, n_ctx, d_a, d_b, n_lora):
    t = pl.program_id(1)
    is_ctx = _ctx_rows(t, tm, n_ctx)
    h = _rms_mod(x_ref[0], g_ref[...], _mod_row(mod_ref, 0, is_ctx), _mod_row(mod_ref, 1, is_ctx))
    h = h.astype(BF16)
    gs = gs_ref[...]

    o = 0
    rkv = jnp.dot(h, w_ref[:, o:o + 3 * d_a], preferred_element_type=F32)
    rkv_ref[0] = rkv
    kkv = rkv[:, d_a:2 * d_a] * kkg_ref[...]
    ss = _group_sum2(kkv * kkv, gs)
    kk_ref[0] = kkv * lax.rsqrt(jnp.maximum(ss, 1e-12))
    o += 3 * d_a

    lora_ref[0] = jnp.dot(h, w_ref[:, o:o + n_lora], preferred_element_type=F32)
    o += n_lora

    qkv = jnp.dot(h, w_ref[:, o:o + 3 * d_b], preferred_element_type=F32)
    q, k = qkv[:, :d_b], qkv[:, d_b:2 * d_b]
    inv_n = 1.0 / HEAD_DIM
    qn = q * lax.rsqrt(_group_sum(q * q, gs) * inv_n + RMS_EPS) * qg_ref[...] * (HEAD_DIM ** -0.5)
    kn = k * lax.rsqrt(_group_sum(k * k, gs) * inv_n + RMS_EPS) * kg_ref[...]
    qkv_ref[0] = jnp.concatenate([qn, kn, qkv[:, 2 * d_b:]], axis=1).astype(BF16)
    o += 3 * d_b

    gates_ref[0] = _sigmoid(jnp.dot(h, w_ref[:, o:], preferred_element_type=F32)).astype(BF16)


def _in_proj_call(xs, modsel, norm_g, w_in, k_k, q_g, k_g, gs, *, n_ctx, d_a, d_b, n_lora, tm=256):
    bsz, tt, d = xs.shape
    n_gate = w_in.shape[1] - 3 * d_a - n_lora - 3 * d_b
    row = lambda b, t: (b, t, 0)
    const = lambda b, t: (0, 0)
    return pl.pallas_call(
        functools.partial(_in_proj_kernel, tm=tm, n_ctx=n_ctx, d_a=d_a, d_b=d_b, n_lora=n_lora),
        out_shape=(jax.ShapeDtypeStruct((bsz, tt, 3 * d_a), F32),
                   jax.ShapeDtypeStruct((bsz, tt, d_a), F32),
                   jax.ShapeDtypeStruct((bsz, tt, n_lora), F32),
                   jax.ShapeDtypeStruct((bsz, tt, 3 * d_b), BF16),
                   jax.ShapeDtypeStruct((bsz, tt, n_gate), BF16)),
        grid=(bsz, tt // tm),
        in_specs=[pl.BlockSpec((1, tm, d), row),
                  pl.BlockSpec((1, 2, 8, d), lambda b, t: (b, 0, 0, 0)),
                  pl.BlockSpec((1, d), const),
                  pl.BlockSpec(w_in.shape, const),
                  pl.BlockSpec((1, d_a), const),
                  pl.BlockSpec((1, d_b), const),
                  pl.BlockSpec((1, d_b), const),
                  pl.BlockSpec((LANE, LANE), const)],
        out_specs=(pl.BlockSpec((1, tm, 3 * d_a), row),
                   pl.BlockSpec((1, tm, d_a), row),
                   pl.BlockSpec((1, tm, n_lora), row),
                   pl.BlockSpec((1, tm, 3 * d_b), row),
                   pl.BlockSpec((1, tm, n_gate), row)),
        compiler_params=pltpu.CompilerParams(dimension_semantics=("parallel", "parallel"),
                                             vmem_limit_bytes=VMEM_LIMIT),
        name="in_proj",
    )(xs, modsel, norm_g, w_in, k_k, q_g, k_g, gs)


N_OPERANDS = 6


def _dot3_presplit(x, w_hi, w_lo):
    xh, xl = _split(x)
    return (jnp.dot(xh, w_hi, preferred_element_type=F32) + jnp.dot(xl, w_hi, preferred_element_type=F32)
            + jnp.dot(xh, w_lo, preferred_element_type=F32))


def _rwkv_prep_kernel(rkv_ref, kk_ref, lora_ref, w0_ref, wlbh_ref, wlbl_ref, a0_ref, albh_ref, albl_ref, ka_ref,
                      opf_ref, opr_ref, v_ref, pef_ref, per_ref, *, tm, d_a, lora_w, lora_a):
    rkv = rkv_ref[0]
    r, k, v = rkv[:, :d_a], rkv[:, d_a:2 * d_a], rkv[:, 2 * d_a:]
    kk = kk_ref[0]
    lora = lora_ref[0]
    v_ref[0] = v.astype(BF16)
    chunk_bits = int(math.log2(CHUNK))
    row = lax.broadcasted_iota(jnp.int32, (tm, tm), 0)
    col = lax.broadcasted_iota(jnp.int32, (tm, tm), 1)
    same = (row >> chunk_bits) == (col >> chunk_bits)
    ones_blk = jnp.where(same, 1.0, 0.0).astype(BF16)
    for d, (op_ref, pe_ref) in enumerate(((opf_ref, pef_ref), (opr_ref, per_ref))):
        lw = lora[:, d * lora_w:(d + 1) * lora_w]
        la = lora[:, 2 * lora_w + d * lora_a:2 * lora_w + (d + 1) * lora_a]
        wl = w0_ref[d] + _dot3_presplit(jnp.tanh(lw), wlbh_ref[d], wlbl_ref[d])
        ld = -DECAY_SCALE * _sigmoid(wl)
        al = _sigmoid(a0_ref[d] + _dot3_presplit(la, albh_ref[d], albl_ref[d]))
        kd = k * (1.0 + (al - 1.0) * ka_ref[...])
        before = (row <= col) if d else (row >= col)
        tri = jnp.where(same & before, 1.0, 0.0).astype(BF16)
        ld_hi, ld_lo = _split(ld)
        cum = jnp.dot(tri, ld_hi, preferred_element_type=F32) + jnp.dot(tri, ld_lo, preferred_element_type=F32)
        tot = (jnp.dot(ones_blk, ld_hi, preferred_element_type=F32)
               + jnp.dot(ones_blk, ld_lo, preferred_element_type=F32))
        p_end = jnp.exp(tot)
        p_inv = jnp.exp(-cum)
        bt = kk * al * p_inv
        kt = kd * p_inv
        op_ref[0] = jnp.concatenate([-kk * jnp.exp(cum - ld), r * jnp.exp(cum), bt, kt, bt * p_end, kt * p_end],
                                    axis=1).astype(BF16)
        for ci in range(tm // CHUNK):
            pe_ref[0, ci] = p_end[ci * CHUNK:ci * CHUNK + 8]


def _rwkv_prep_call(rkv, kk, lora, w0, wlb, a0, alb, k_a, *, lora_w, lora_a, tm=256):
    bsz, tt, d3 = rkv.shape
    d_a = d3 // 3
    row = lambda b, t: (b, t, 0)
    const3 = lambda b, t: (0, 0, 0)
    wlb_hi, wlb_lo = _split(wlb)
    alb_hi, alb_lo = _split(alb)
    op_shape = jax.ShapeDtypeStruct((bsz, tt, N_OPERANDS * d_a), BF16)
    pe_shape = jax.ShapeDtypeStruct((bsz, tt // CHUNK, 8, d_a), F32)
    pe_spec = pl.BlockSpec((1, tm // CHUNK, 8, d_a), lambda b, t: (b, t, 0, 0))
    return pl.pallas_call(
        functools.partial(_rwkv_prep_kernel, tm=tm, d_a=d_a, lora_w=lora_w, lora_a=lora_a),
        out_shape=(op_shape, op_shape, jax.ShapeDtypeStruct((bsz, tt, d_a), BF16), pe_shape, pe_shape),
        grid=(bsz, tt // tm),
        in_specs=[pl.BlockSpec((1, tm, d3), row),
                  pl.BlockSpec((1, tm, d_a), row),
                  pl.BlockSpec((1, tm, lora.shape[2]), row),
                  pl.BlockSpec((2, 1, d_a), const3),
                  pl.BlockSpec((2, lora_w, d_a), const3),
                  pl.BlockSpec((2, lora_w, d_a), const3),
                  pl.BlockSpec((2, 1, d_a), const3),
                  pl.BlockSpec((2, lora_a, d_a), const3),
                  pl.BlockSpec((2, lora_a, d_a), const3),
                  pl.BlockSpec((1, d_a), lambda b, t: (0, 0))],
        out_specs=(pl.BlockSpec((1, tm, N_OPERANDS * d_a), row),
                   pl.BlockSpec((1, tm, N_OPERANDS * d_a), row),
                   pl.BlockSpec((1, tm, d_a), row), pe_spec, pe_spec),
        compiler_params=pltpu.CompilerParams(dimension_semantics=("parallel", "parallel"),
                                             vmem_limit_bytes=VMEM_LIMIT),
        name="rwkv7_operands",
    )(rkv, kk, lora, w0.reshape(2, 1, d_a), wlb_hi, wlb_lo, a0.reshape(2, 1, d_a), alb_hi, alb_lo, k_a)


def _rwkv_kernel(opf_ref, opr_ref, vf_ref, vr_ref, pef_ref, per_ref, yf_ref, yr_ref, st_ref, *, d_a):
    n = HEAD_DIM
    heads = d_a // n
    c = CHUNK

    @pl.when(pl.program_id(1) == 0)
    def _():
        st_ref[...] = jnp.zeros_like(st_ref)

    ops = (opf_ref[0], opr_ref[0])
    vs = (vf_ref[0], vr_ref[0])
    pes = (pef_ref[0, 0, 0:1, :], per_ref[0, 0, 0:1, :])
    chains = [(d, h) for d in range(2) for h in range(heads)]
    part = lambda i: [ops[d][:, i * d_a + h * n:i * d_a + (h + 1) * n] for d, h in chains]
    each = lambda f, *ls: [f(*xs) for xs in zip(*ls)]
    rows2 = lambda a, b: jnp.concatenate([a, b], axis=0)
    ar = each(rows2, part(0), part(1))
    bk = each(rows2, part(2), part(3))
    bk_end = each(rows2, part(4), part(5))
    v = [vs[d][:, h * n:(h + 1) * n] for d, h in chains]
    p_end = [pes[d][:, h * n:(h + 1) * n] for d, h in chains]

    row = lax.broadcasted_iota(jnp.int32, (c, 2 * c), 0)
    col = lax.broadcasted_iota(jnp.int32, (c, 2 * c), 1) & (c - 1)
    incl = [(row <= col) if d else (row >= col) for d, _ in chains]
    strict = [(row < col) if d else (row > col) for d, _ in chains]

    m = each(_bdot_nt, ar, bk)
    m_a = each(lambda mm, s: jnp.where(s, mm[:c], 0.0), m, strict)
    m_r = each(lambda mm, s: jnp.where(s, mm[c:], 0.0), m, incl)
    a_ab = each(lambda mm: mm[:, :c], m_a)
    a_ak = each(lambda mm: mm[:, c:], m_a)

    row = lax.broadcasted_iota(jnp.int32, (c, c), 0)
    col = lax.broadcasted_iota(jnp.int32, (c, c), 1)
    eye = jnp.where(row == col, 1.0, 0.0)
    sub_bits = int(math.log2(SUB))
    blk = (row >> sub_bits) == (col >> sub_bits)
    ad = each(lambda a: jnp.where(blk, a, 0.0), a_ab)
    ao = each(lambda a, b: a - b, a_ab, ad)
    acc_mul = lambda t, x: t + _bdot(t, x)
    sq = lambda x: _bdot(x, x)
    td = each(lambda a: eye + a, ad)
    x = ad
    for _ in range(sub_bits - 1):
        x = each(sq, x)
        td = each(acc_mul, td, x)
    x = each(_bdot, td, ao)
    tf = each(lambda a: eye + a, x)
    for _ in range(int(math.log2(c // SUB)) - 1):
        x = each(sq, x)
        tf = each(acc_mul, tf, x)
    t_inv = each(_bdot, tf, td)
    akv = each(_bdot, a_ak, v)

    s = [st_ref[d, h] for d, h in chains]
    ms = each(_bdot_nt, ar, s)
    u = each(lambda t, mm, w: _bdot(t, mm[:c] + w).astype(BF16), t_inv, ms, akv)
    uv = each(rows2, u, v)
    y = each(lambda mm, a, w: mm[c:] + _bdot(a, w), ms, m_r, uv)
    s_new = each(lambda ss, p, w, b: ss * p + _bdot_tn(w, b), s, p_end, uv, bk_end)
    for (d, h), val in zip(chains, s_new):
        st_ref[d, h] = val
    yf_ref[0] = jnp.concatenate(y[:heads], axis=1)
    yr_ref[0] = jnp.concatenate(y[heads:], axis=1)


def _rwkv_call(rkv, kk, lora, w0, wlb, a0, alb, k_a, *, n_ctx, lora_w, lora_a):
    bsz, tt, d3 = rkv.shape
    d_a = d3 // 3
    n_chunks = tt // CHUNK
    nc_ctx = n_ctx // CHUNK
    heads = d_a // HEAD_DIM
    op_f, op_r, v, pe_f, pe_r = _rwkv_prep_call(rkv, kk, lora, w0, wlb, a0, alb, k_a,
                                                lora_w=lora_w, lora_a=lora_a)

    rev_chunk = lambda j: jnp.where(j < nc_ctx, nc_ctx - 1 - j, n_chunks - 1 - (j - nc_ctx))
    fwd = lambda b, j: (b, j, 0)
    rev = lambda b, j: (b, rev_chunk(j), 0)
    y_shape = jax.ShapeDtypeStruct((bsz, tt, d_a), F32)
    return pl.pallas_call(
        functools.partial(_rwkv_kernel, d_a=d_a),
        out_shape=(y_shape, y_shape),
        grid=(bsz, n_chunks),
        in_specs=[pl.BlockSpec((1, CHUNK, N_OPERANDS * d_a), fwd),
                  pl.BlockSpec((1, CHUNK, N_OPERANDS * d_a), rev),
                  pl.BlockSpec((1, CHUNK, d_a), fwd),
                  pl.BlockSpec((1, CHUNK, d_a), rev),
                  pl.BlockSpec((1, 1, 8, d_a), lambda b, j: (b, j, 0, 0)),
                  pl.BlockSpec((1, 1, 8, d_a), lambda b, j: (b, rev_chunk(j), 0, 0))],
        out_specs=(pl.BlockSpec((1, CHUNK, d_a), fwd), pl.BlockSpec((1, CHUNK, d_a), rev)),
        scratch_shapes=[pltpu.VMEM((2, heads, HEAD_DIM, HEAD_DIM), F32)],
        compiler_params=pltpu.CompilerParams(dimension_semantics=("parallel", "arbitrary"),
                                             vmem_limit_bytes=VMEM_LIMIT),
        name="rwkv7_chunk_scan",
    )(op_f, op_r, v, v, pe_f, pe_r)


def _softmax_pv(parts):
    m = functools.reduce(jnp.maximum, [jnp.max(s, axis=-1, keepdims=True) for s, _ in parts])
    num, den = 0.0, 0.0
    for s, vals in parts:
        p = jnp.exp(s - m)
        den = den + jnp.sum(p, axis=-1, keepdims=True)
        num = num + jnp.dot(p.astype(BF16), vals, preferred_element_type=F32)
    return num / den


def _attn_kernel(q_ref, k_ref, v_ref, bias_ref, o_ref, *, n_ctx, rows, q_off, d_b):
    n = HEAD_DIM
    heads = d_b // n
    nq_ctx = n_ctx // GRID_W
    i = pl.program_id(1) + q_off
    q = q_ref[0]
    kc = k_ref[0, 0:n_ctx, :]
    vc = v_ref[0, 0:n_ctx, :]

    def latent():
        li = i - nq_ctx
        r0 = jnp.clip(li - WIN_H // 2, 0, rows - WIN_H)
        start = pl.multiple_of(n_ctx + r0 * GRID_W, GRID_W)
        kl = k_ref[0, pl.ds(start, WIN_H * GRID_W), :]
        vl = v_ref[0, pl.ds(start, WIN_H * GRID_W), :]
        outs = []
        for h in range(heads):
            sl = slice(h * n, (h + 1) * n)
            s_l = _bdot_nt(q[:, sl], kl[:, sl]) + bias_ref[0, h]
            s_c = _bdot_nt(q[:, sl], kc[:, sl])
            outs.append(_softmax_pv([(s_l, vl[:, sl]), (s_c, vc[:, sl])]))
        o_ref[0] = jnp.concatenate(outs, axis=1).astype(o_ref.dtype)

    def context():
        outs = []
        for h in range(heads):
            sl = slice(h * n, (h + 1) * n)
            outs.append(_softmax_pv([(_bdot_nt(q[:, sl], kc[:, sl]), vc[:, sl])]))
        o_ref[0] = jnp.concatenate(outs, axis=1).astype(o_ref.dtype)

    if q_off == 0:
        pl.when(i >= nq_ctx)(latent)
        pl.when(i < nq_ctx)(context)
    else:
        latent()


def _bias_table(rpb):
    j = jnp.arange(GRID_W)
    col_start = jnp.clip(j - WIN_W // 2, 0, GRID_W - WIN_W)
    in_win = (j[None, :] >= col_start[:, None]) & (j[None, :] < col_start[:, None] + WIN_W)
    d_col = jnp.clip(j[None, :] - j[:, None], -(WIN_W - 1), WIN_W - 1) + (WIN_W - 1)
    tz = jnp.where(in_win[None, None], rpb[:, :, d_col].astype(F32), NEG_BIG)
    idx = jnp.arange(WIN_H)[:, None] + jnp.arange(WIN_H)[None, :]
    bt = tz[:, idx]
    heads = rpb.shape[0]
    return bt.transpose(1, 0, 3, 2, 4).reshape(WIN_H, heads, GRID_W, WIN_H * GRID_W)


def _attn_call(qkv, bias_tab, *, n_ctx, with_ctx):
    bsz, tt, d3 = qkv.shape
    d_b = d3 // 3
    heads = d_b // HEAD_DIM
    rows = (tt - n_ctx) // GRID_W
    nq_ctx = n_ctx // GRID_W
    q_off = 0 if with_ctx else nq_ctx
    n_q = tt // GRID_W - q_off

    def variant(b, i):
        li = jnp.maximum(i + q_off - nq_ctx, 0)
        return (jnp.clip(li - WIN_H // 2, 0, rows - WIN_H) - li + WIN_H - 1, 0, 0, 0)

    return pl.pallas_call(
        functools.partial(_attn_kernel, n_ctx=n_ctx, rows=rows, q_off=q_off, d_b=d_b),
        out_shape=jax.ShapeDtypeStruct((bsz, n_q * GRID_W, d_b), BF16),
        grid=(bsz, n_q),
        in_specs=[pl.BlockSpec((1, GRID_W, d_b), lambda b, i: (b, i + q_off, 0)),
                  pl.BlockSpec((1, tt, d_b), lambda b, i: (b, 0, 1)),
                  pl.BlockSpec((1, tt, d_b), lambda b, i: (b, 0, 2)),
                  pl.BlockSpec((1, heads, GRID_W, WIN_H * GRID_W), variant)],
        out_specs=pl.BlockSpec((1, GRID_W, d_b), lambda b, i: (b, i, 0)),
        compiler_params=pltpu.CompilerParams(dimension_semantics=("parallel", "arbitrary"),
                                             vmem_limit_bytes=VMEM_LIMIT),
        name="neighbourhood_attention",
    )(qkv, qkv, qkv, bias_tab)


def _merge_kernel(x_ref, mod_ref, y0_ref, y1_ref, rkv_ref, lora_ref, yb_ref, gates_ref,
                  rk_ref, lng_ref, lnb_ref, glb_ref, pa_ref, pb_ref, wo_ref, gs_ref, o_ref,
                  *, tm, n_ctx, t_off, d_a, lora_g):
    t = pl.program_id(1) + t_off
    is_ctx = _ctx_rows(t, tm, n_ctx)
    gs = gs_ref[...]
    inv_n = 1.0 / HEAD_DIM
    y = y0_ref[0] + y1_ref[0]
    mu = _group_sum2(y, gs) * inv_n
    yc = y - mu
    var = _group_sum2(yc * yc, gs) * inv_n
    yn = yc * lax.rsqrt(var + GN_EPS) * lng_ref[...] + lnb_ref[...]
    rkv = rkv_ref[0]
    r, k, v = rkv[:, :d_a], rkv[:, d_a:2 * d_a], rkv[:, 2 * d_a:]
    bonus = _group_sum2(r * k * rk_ref[...], gs) * v
    lg = lora_ref[0][:, lora_ref.shape[2] - lora_g:]
    g = _bdot(_sigmoid(lg), glb_ref[...])
    ya = (yn + bonus) * g
    gates = gates_ref[0].astype(F32)
    dm = gates.shape[1] // 2
    merged = gates[:, :dm] * _bdot(ya, pa_ref[...]) + gates[:, dm:] * _bdot(yb_ref[0], pb_ref[...])
    o_ref[0] = x_ref[0] + _mod_row(mod_ref, 2, is_ctx) * _bdot(merged, wo_ref[...])


def _merge_call(xs, modsel, y_f, y_r, rkv, lora, yb, gates, r_k, ln_g, ln_b, glb, pa, pb, wo, gs,
                *, n_ctx, with_ctx, lora_g, tm=256):
    bsz, tt, d = xs.shape
    d_a = rkv.shape[2] // 3
    t_off = 0 if with_ctx else n_ctx // tm
    n_t = tt // tm - t_off
    yb_off = 0 if with_ctx else -t_off
    row = lambda b, t: (b, t + t_off, 0)
    const = lambda b, t: (0, 0)
    return pl.pallas_call(
        functools.partial(_merge_kernel, tm=tm, n_ctx=n_ctx, t_off=t_off, d_a=d_a, lora_g=lora_g),
        out_shape=jax.ShapeDtypeStruct((bsz, n_t * tm, d), F32),
        grid=(bsz, n_t),
        in_specs=[pl.BlockSpec((1, tm, d), row),
                  pl.BlockSpec((1, 2, 8, d), lambda b, t: (b, 0, 0, 0)),
                  pl.BlockSpec((1, tm, d_a), row),
                  pl.BlockSpec((1, tm, d_a), row),
                  pl.BlockSpec((1, tm, 3 * d_a), row),
                  pl.BlockSpec((1, tm, lora.shape[2]), row),
                  pl.BlockSpec((1, tm, yb.shape[2]), lambda b, t: (b, t + t_off + yb_off, 0)),
                  pl.BlockSpec((1, tm, gates.shape[2]), row),
                  pl.BlockSpec((1, d_a), const),
                  pl.BlockSpec((1, d_a), const),
                  pl.BlockSpec((1, d_a), const),
                  pl.BlockSpec(glb.shape, const),
                  pl.BlockSpec(pa.shape, const),
                  pl.BlockSpec(pb.shape, const),
                  pl.BlockSpec(wo.shape, const),
                  pl.BlockSpec((LANE, LANE), const)],
        out_specs=pl.BlockSpec((1, tm, d), lambda b, t: (b, t, 0)),
        compiler_params=pltpu.CompilerParams(dimension_semantics=("parallel", "parallel"),
                                             vmem_limit_bytes=VMEM_LIMIT),
        name="branch_merge",
    )(xs, modsel, y_f, y_r, rkv, lora, yb, gates, r_k, ln_g, ln_b, glb, pa, pb, wo, gs)


def _route(sel, scores, n_experts):
    per = n_experts // N_GROUPS
    in_top = []
    for e in range(n_experts):
        g0 = (e // per) * per
        rank = 0.0
        for o in range(g0, g0 + per):
            if o == e:
                continue
            ahead = (sel[o] >= sel[e]) if o < e else (sel[o] > sel[e])
            rank = rank + jnp.where(ahead, 1.0, 0.0)
        in_top.append(rank < TOP_K)
    grp = [sum(jnp.where(in_top[e], sel[e], 0.0) for e in range(g * per, (g + 1) * per))
           for g in range(N_GROUPS)]
    best = jnp.zeros_like(grp[0], dtype=jnp.int32)
    best_s = grp[0]
    for g in range(1, N_GROUPS):
        better = grp[g] > best_s
        best = jnp.where(better, g, best)
        best_s = jnp.where(better, grp[g], best_s)
    chosen = [in_top[e] & (best == e // per) for e in range(n_experts)]
    den = sum(jnp.where(chosen[e], scores[e], 0.0) for e in range(n_experts))
    return [jnp.where(chosen[e], scores[e] / den, 0.0) for e in range(n_experts)]


def _moe_kernel(x_ref, mod_ref, g_ref, rw_ref, rb_ref, w1_ref, w3_ref, w2_ref, o_ref,
                h_ref, gate_ref, acc_ref, *, tm, n_ctx, n_experts):
    t = pl.program_id(1)
    e = pl.program_id(2)
    is_ctx = _ctx_rows(t, tm, n_ctx)

    @pl.when(e == 0)
    def _():
        h = _rms_mod(x_ref[0], g_ref[...], _mod_row(mod_ref, 3, is_ctx), _mod_row(mod_ref, 4, is_ctx))
        h_ref[...] = h.astype(BF16)
        hh, hl = _split(h)
        wh, wl = _split(rw_ref[...])
        nt = lambda a, b: lax.dot_general(a, b, (((1,), (1,)), ((), ())), preferred_element_type=F32)
        logits = nt(wh, hh) + nt(wh, hl) + nt(wl, hh)
        scores = _sigmoid(logits)
        sel = scores + rb_ref[...]
        gates = _route([sel[i:i + 1, :] for i in range(n_experts)],
                       [scores[i:i + 1, :] for i in range(n_experts)], n_experts)
        gt = jnp.concatenate(gates + [jnp.zeros((LANE - n_experts, tm), F32)], axis=0)
        gate_ref[...] = gt.T
        acc_ref[...] = jnp.zeros_like(acc_ref)

    h = h_ref[...]
    a = jnp.dot(h, w1_ref[0], preferred_element_type=F32)
    b = jnp.dot(h, w3_ref[0], preferred_element_type=F32)
    hid = (a * _sigmoid(a) * b).astype(BF16)
    lane = lax.broadcasted_iota(jnp.int32, (tm, LANE), 1)
    gcol = jnp.sum(jnp.where(lane == e, gate_ref[...], 0.0), axis=-1, keepdims=True)
    acc_ref[...] += gcol * jnp.dot(hid, w2_ref[0], preferred_element_type=F32)

    @pl.when(e == n_experts - 1)
    def _():
        o_ref[0] = x_ref[0] + _mod_row(mod_ref, 5, is_ctx) * acc_ref[...]


def _moe_call(xs, modsel, norm_g, router_wt, router_b, w1, w3, w2, *, n_ctx, tm):
    bsz, tt, d = xs.shape
    n_experts = router_wt.shape[0]
    row = lambda b, t, e: (b, t, 0)
    const = lambda b, t, e: (0, 0)
    wexp = lambda b, t, e: (e, 0, 0)
    return pl.pallas_call(
        functools.partial(_moe_kernel, tm=tm, n_ctx=n_ctx, n_experts=n_experts),
        out_shape=jax.ShapeDtypeStruct((bsz, tt, d), F32),
        grid=(bsz, tt // tm, n_experts),
        in_specs=[pl.BlockSpec((1, tm, d), row),
                  pl.BlockSpec((1, 2, 8, d), lambda b, t, e: (b, 0, 0, 0)),
                  pl.BlockSpec((1, d), const),
                  pl.BlockSpec(router_wt.shape, const),
                  pl.BlockSpec((n_experts, 1), const),
                  pl.BlockSpec((1,) + w1.shape[1:], wexp),
                  pl.BlockSpec((1,) + w3.shape[1:], wexp),
                  pl.BlockSpec((1,) + w2.shape[1:], wexp)],
        out_specs=pl.BlockSpec((1, tm, d), row),
        scratch_shapes=[pltpu.VMEM((tm, d), BF16),
                        pltpu.VMEM((tm, LANE), F32),
                        pltpu.VMEM((tm, d), F32)],
        compiler_params=pltpu.CompilerParams(
            dimension_semantics=("parallel", "parallel", "arbitrary"),
            vmem_limit_bytes=VMEM_LIMIT),
        name="moe",
    )(xs, modsel, norm_g, router_wt, router_b, w1, w3, w2)


def kernel(x, c, ctx, c_ctx, mod_w, mod_b, norm1_g, norm2_g, w_in, rw_w0, rw_w_lora_b, rw_a0, rw_a_lora_b,
           rw_g_lora_b, rw_k_k, rw_k_a, rw_r_k, rw_ln_g, rw_ln_b, na_q_g, na_k_g, na_rpb, proj_a, proj_b,
           w_out, router_w, router_bias, moe_w1, moe_w3, moe_w2):
    bsz, seq, d = x.shape
    n_ctx = ctx.shape[1]
    depth = mod_w.shape[0]
    d_a = rw_w0.shape[2]
    d_b = proj_b.shape[1]
    lora_w = rw_w_lora_b.shape[2]
    lora_a = rw_a_lora_b.shape[2]
    lora_g = rw_g_lora_b.shape[1]
    n_lora = 2 * lora_w + 2 * lora_a + lora_g
    heads_a = d_a // HEAD_DIM
    heads_b = d_b // HEAD_DIM
    assert seq % (GRID_W * WIN_H) == 0 and n_ctx % 256 == 0 and bsz + 1 <= 16

    cs = jnp.zeros((16, d), F32).at[:bsz].set(c).at[bsz].set(c_ctx)
    mod = _mod_call(cs, mod_w, mod_b).reshape(depth, 16, 6, d)

    lane = jnp.arange(LANE)
    gs = (lane[:, None] // HEAD_DIM == lane[None, :] // HEAD_DIM).astype(BF16)
    router_wt = router_w.T
    router_b = router_bias.reshape(-1, 1)

    xs = jnp.concatenate([ctx, x], axis=1)
    for l in range(depth):
        last = l == depth - 1
        m_c = jnp.broadcast_to(mod[l, bsz][None], (bsz, 6, d))
        modsel = jnp.pad(jnp.stack([m_c, mod[l, :bsz]], axis=1), ((0, 0), (0, 0), (0, 2), (0, 0)))

        rkv, kk, lora, qkv, gates = _in_proj_call(
            xs, modsel, norm1_g[l][None], w_in[l].astype(BF16), rw_k_k[l][None],
            jnp.tile(na_q_g[l], heads_b)[None], jnp.tile(na_k_g[l], heads_b)[None], gs,
            n_ctx=n_ctx, d_a=d_a, d_b=d_b, n_lora=n_lora)
        y_f, y_r = _rwkv_call(rkv, kk, lora, rw_w0[l], rw_w_lora_b[l], rw_a0[l], rw_a_lora_b[l],
                              rw_k_a[l][None], n_ctx=n_ctx, lora_w=lora_w, lora_a=lora_a)
        yb = _attn_call(qkv, _bias_table(na_rpb[l]), n_ctx=n_ctx, with_ctx=not last)
        xs = _merge_call(xs, modsel, y_f, y_r, rkv, lora, yb, gates, rw_r_k[l].reshape(1, d_a),
                         rw_ln_g[l][None], rw_ln_b[l][None], rw_g_lora_b[l].astype(BF16),
                         proj_a[l].astype(BF16), proj_b[l].astype(BF16), w_out[l].astype(BF16), gs,
                         n_ctx=n_ctx, with_ctx=not last, lora_g=lora_g)
        moe_ctx = 0 if last else n_ctx
        tm = next(m for m in (1024, 768, 512, 256) if xs.shape[1] % m == 0)
        xs = _moe_call(xs, modsel, norm2_g[l][None], router_wt, router_b,
                       moe_w1[l].astype(BF16), moe_w3[l].astype(BF16), moe_w2[l].astype(BF16),
                       n_ctx=moe_ctx, tm=tm)
    return xs
```

```python
import functools
import math

import jax
import jax.numpy as jnp
from jax import lax
from jax.experimental import pallas as pl
from jax.experimental.pallas import tpu as pltpu

F32 = jnp.float32
BF16 = jnp.bfloat16

HEAD_DIM = 64
GRID_W = 64
WIN_H = 8
WIN_W = 16
N_GROUPS = 4
TOP_K = 2
RMS_EPS = 1e-6
GN_EPS = 64e-5
LANE = 128
CHUNK = 64
SUB = 16
NEG_BIG = -1e30
DECAY_SCALE = math.exp(-0.5)
VMEM_LIMIT = 48 * 1024 * 1024


def _bdot(a, b):
    return jnp.dot(a.astype(BF16), b.astype(BF16), preferred_element_type=F32)


def _bdot_nt(a, b):
    return lax.dot_general(a.astype(BF16), b.astype(BF16), (((1,), (1,)), ((), ())),
                           preferred_element_type=F32)


def _bdot_tn(a, b):
    return lax.dot_general(a.astype(BF16), b.astype(BF16), (((0,), (0,)), ((), ())),
                           preferred_element_type=F32)


def _split(x):
    hi = x.astype(BF16)
    lo = (x - hi.astype(F32)).astype(BF16)
    return hi, lo


def _dot3(a, b):
    ah, al = _split(a)
    bh, bl = _split(b)
    return (jnp.dot(ah, bh, preferred_element_type=F32)
            + jnp.dot(al, bh, preferred_element_type=F32)
            + jnp.dot(ah, bl, preferred_element_type=F32))


def _sigmoid(x):
    return 1.0 / (1.0 + jnp.exp(-x))


def _group_sum(x, g128):
    parts = [_bdot(x[:, j * LANE:(j + 1) * LANE], g128) for j in range(x.shape[1] // LANE)]
    return jnp.concatenate(parts, axis=1)


def _group_sum2(x, g128):
    hi, lo = _split(x)
    parts = []
    for j in range(x.shape[1] // LANE):
        sl = slice(j * LANE, (j + 1) * LANE)
        parts.append(jnp.dot(hi[:, sl], g128, preferred_element_type=F32)
                     + jnp.dot(lo[:, sl], g128, preferred_element_type=F32))
    return jnp.concatenate(parts, axis=1)


def _mod_row(mod_ref, idx, is_ctx):
    mx = mod_ref[0, 1, idx:idx + 1, :]
    if is_ctx is None:
        return mx
    return jnp.where(is_ctx, mod_ref[0, 0, idx:idx + 1, :], mx)


def _ctx_rows(tile, tm, n_ctx):
    if n_ctx == 0:
        return None
    rows = tile * tm + lax.broadcasted_iota(jnp.int32, (tm, 1), 0)
    return rows < n_ctx


def _rms_mod(x, gain, shift, scale):
    xn = x * lax.rsqrt(jnp.mean(x * x, axis=-1, keepdims=True) + RMS_EPS) * gain
    return xn * (1.0 + scale) + shift


def _mod_kernel(c_ref, w_ref, b_ref, o_ref):
    c = c_ref[...]
    o_ref[0] = _dot3(c * _sigmoid(c), w_ref[0]) + b_ref[0]


def _mod_call(cs, mod_w, mod_b):
    depth, d, n = mod_w.shape
    tn = n // 4
    return pl.pallas_call(
        _mod_kernel,
        out_shape=jax.ShapeDtypeStruct((depth, cs.shape[0], n), F32),
        grid=(depth, n // tn),
        in_specs=[pl.BlockSpec(cs.shape, lambda l, j: (0, 0)),
                  pl.BlockSpec((1, d, tn), lambda l, j: (l, 0, j)),
                  pl.BlockSpec((1, 1, tn), lambda l, j: (l, 0, j))],
        out_specs=pl.BlockSpec((1, cs.shape[0], tn), lambda l, j: (l, 0, j)),
        compiler_params=pltpu.CompilerParams(dimension_semantics=("parallel", "parallel"),
                                             vmem_limit_bytes=VMEM_LIMIT),
        name="adaln_vectors",
    )(cs, mod_w, mod_b.reshape(depth, 1, n))


def _in_proj_kernel(x_ref, mod_ref, g_ref, w_ref, kkg_ref, qg_ref, kg_ref, gs_ref,
                    rkv_ref, kk_ref, lora_ref, qkv_ref, gates_ref, *, tm, n_ctx, d_a, d_b, n_lora):
    t = pl.program_id(1)
    is_ctx = _ctx_rows(t, tm, n_ctx)
    h = _rms_mod(x_ref[0], g_ref[...], _mod_row(mod_ref, 0, is_ctx), _mod_row(mod_ref, 1, is_ctx))
    h = h.astype(BF16)
    gs = gs_ref[...]

    o = 0
    rkv = jnp.dot(h, w_ref[:, o:o + 3 * d_a], preferred_element_type=F32)
    rkv_ref[0] = rkv
    kkv = rkv[:, d_a:2 * d_a] * kkg_ref[...]
    ss = _group_sum2(kkv * kkv, gs)
    kk_ref[0] = kkv * lax.rsqrt(jnp.maximum(ss, 1e-12))
    o += 3 * d_a

    lora_ref[0] = jnp.dot(h, w_ref[:, o:o + n_lora], preferred_element_type=F32)
    o += n_lora

    qkv = jnp.dot(h, w_ref[:, o:o + 3 * d_b], preferred_element_type=F32)
    q, k = qkv[:, :d_b], qkv[:, d_b:2 * d_b]
    inv_n = 1.0 / HEAD_DIM
    qn = q * lax.rsqrt(_group_sum(q * q, gs) * inv_n + RMS_EPS) * qg_ref[...] * (HEAD_DIM ** -0.5)
    kn = k * lax.rsqrt(_group_sum(k * k, gs) * inv_n + RMS_EPS) * kg_ref[...]
    qkv_ref[0] = jnp.concatenate([qn, kn, qkv[:, 2 * d_b:]], axis=1).astype(BF16)
    o += 3 * d_b

    gates_ref[0] = _sigmoid(jnp.dot(h, w_ref[:, o:], preferred_element_type=F32)).astype(BF16)


def _in_proj_call(xs, modsel, norm_g, w_in, k_k, q_g, k_g, gs, *, n_ctx, d_a, d_b, n_lora, tm=256):
    bsz, tt, d = xs.shape
    n_gate = w_in.shape[1] - 3 * d_a - n_lora - 3 * d_b
    row = lambda b, t: (b, t, 0)
    const = lambda b, t: (0, 0)
    return pl.pallas_call(
        functools.partial(_in_proj_kernel, tm=tm, n_ctx=n_ctx, d_a=d_a, d_b=d_b, n_lora=n_lora),
        out_shape=(jax.ShapeDtypeStruct((bsz, tt, 3 * d_a), F32),
                   jax.ShapeDtypeStruct((bsz, tt, d_a), F32),
                   jax.ShapeDtypeStruct((bsz, tt, n_lora), F32),
                   jax.ShapeDtypeStruct((bsz, tt, 3 * d_b), BF16),
                   jax.ShapeDtypeStruct((bsz, tt, n_gate), BF16)),
        grid=(bsz, tt // tm),
        in_specs=[pl.BlockSpec((1, tm, d), row),
                  pl.BlockSpec((1, 2, 8, d), lambda b, t: (b, 0, 0, 0)),
                  pl.BlockSpec((1, d), const),
                  pl.BlockSpec(w_in.shape, const),
                  pl.BlockSpec((1, d_a), const),
                  pl.BlockSpec((1, d_b), const),
                  pl.BlockSpec((1, d_b), const),
                  pl.BlockSpec((LANE, LANE), const)],
        out_specs=(pl.BlockSpec((1, tm, 3 * d_a), row),
                   pl.BlockSpec((1, tm, d_a), row),
                   pl.BlockSpec((1, tm, n_lora), row),
                   pl.BlockSpec((1, tm, 3 * d_b), row),
                   pl.BlockSpec((1, tm, n_gate), row)),
        compiler_params=pltpu.CompilerParams(dimension_semantics=("parallel", "parallel"),
                                             vmem_limit_bytes=VMEM_LIMIT),
        name="in_proj",
    )(xs, modsel, norm_g, w_in, k_k, q_g, k_g, gs)


N_OPERANDS = 6


def _dot3_presplit(x, w_hi, w_lo):
    xh, xl = _split(x)
    return (jnp.dot(xh, w_hi, preferred_element_type=F32) + jnp.dot(xl, w_hi, preferred_element_type=F32)
            + jnp.dot(xh, w_lo, preferred_element_type=F32))


def _rwkv_prep_kernel(rkv_ref, kk_ref, lora_ref, w0_ref, wlbh_ref, wlbl_ref, a0_ref, albh_ref, albl_ref, ka_ref,
                      opf_ref, opr_ref, v_ref, pef_ref, per_ref, *, tm, d_a, lora_w, lora_a):
    rkv = rkv_ref[0]
    r, k, v = rkv[:, :d_a], rkv[:, d_a:2 * d_a], rkv[:, 2 * d_a:]
    kk = kk_ref[0]
    lora = lora_ref[0]
    v_ref[0] = v.astype(BF16)
    chunk_bits = int(math.log2(CHUNK))
    row = lax.broadcasted_iota(jnp.int32, (tm, tm), 0)
    col = lax.broadcasted_iota(jnp.int32, (tm, tm), 1)
    same = (row >> chunk_bits) == (col >> chunk_bits)
    ones_blk = jnp.where(same, 1.0, 0.0).astype(BF16)
    for d, (op_ref, pe_ref) in enumerate(((opf_ref, pef_ref), (opr_ref, per_ref))):
        lw = lora[:, d * lora_w:(d + 1) * lora_w]
        la = lora[:, 2 * lora_w + d * lora_a:2 * lora_w + (d + 1) * lora_a]
        wl = w0_ref[d] + _dot3_presplit(jnp.tanh(lw), wlbh_ref[d], wlbl_ref[d])
        ld = -DECAY_SCALE * _sigmoid(wl)
        al = _sigmoid(a0_ref[d] + _dot3_presplit(la, albh_ref[d], albl_ref[d]))
        kd = k * (1.0 + (al - 1.0) * ka_ref[...])
        before = (row <= col) if d else (row >= col)
        tri = jnp.where(same & before, 1.0, 0.0).astype(BF16)
        ld_hi, ld_lo = _split(ld)
        cum = jnp.dot(tri, ld_hi, preferred_element_type=F32) + jnp.dot(tri, ld_lo, preferred_element_type=F32)
        tot = (jnp.dot(ones_blk, ld_hi, preferred_element_type=F32)
               + jnp.dot(ones_blk, ld_lo, preferred_element_type=F32))
        p_end = jnp.exp(tot)
        p_inv = jnp.exp(-cum)
        bt = kk * al * p_inv
        kt = kd * p_inv
        op_ref[0] = jnp.concatenate([-kk * jnp.exp(cum - ld), r * jnp.exp(cum), bt, kt, bt * p_end, kt * p_end],
                                    axis=1).astype(BF16)
        for ci in range(tm // CHUNK):
            pe_ref[0, ci] = p_end[ci * CHUNK:ci * CHUNK + 8]


def _rwkv_prep_call(rkv, kk, lora, w0, wlb, a0, alb, k_a, *, lora_w, lora_a, tm=256):
    bsz, tt, d3 = rkv.shape
    d_a = d3 // 3
    row = lambda b, t: (b, t, 0)
    const3 = lambda b, t: (0, 0, 0)
    wlb_hi, wlb_lo = _split(wlb)
    alb_hi, alb_lo = _split(alb)
    op_shape = jax.ShapeDtypeStruct((bsz, tt, N_OPERANDS * d_a), BF16)
    pe_shape = jax.ShapeDtypeStruct((bsz, tt // CHUNK, 8, d_a), F32)
    pe_spec = pl.BlockSpec((1, tm // CHUNK, 8, d_a), lambda b, t: (b, t, 0, 0))
    return pl.pallas_call(
        functools.partial(_rwkv_prep_kernel, tm=tm, d_a=d_a, lora_w=lora_w, lora_a=lora_a),
        out_shape=(op_shape, op_shape, jax.ShapeDtypeStruct((bsz, tt, d_a), BF16), pe_shape, pe_shape),
        grid=(bsz, tt // tm),
        in_specs=[pl.BlockSpec((1, tm, d3), row),
                  pl.BlockSpec((1, tm, d_a), row),
                  pl.BlockSpec((1, tm, lora.shape[2]), row),
                  pl.BlockSpec((2, 1, d_a), const3),
                  pl.BlockSpec((2, lora_w, d_a), const3),
                  pl.BlockSpec((2, lora_w, d_a), const3),
                  pl.BlockSpec((2, 1, d_a), const3),
                  pl.BlockSpec((2, lora_a, d_a), const3),
                  pl.BlockSpec((2, lora_a, d_a), const3),
                  pl.BlockSpec((1, d_a), lambda b, t: (0, 0))],
        out_specs=(pl.BlockSpec((1, tm, N_OPERANDS * d_a), row),
                   pl.BlockSpec((1, tm, N_OPERANDS * d_a), row),
                   pl.BlockSpec((1, tm, d_a), row), pe_spec, pe_spec),
        compiler_params=pltpu.CompilerParams(dimension_semantics=("parallel", "parallel"),
                                             vmem_limit_bytes=VMEM_LIMIT),
        name="rwkv7_operands",
    )(rkv, kk, lora, w0.reshape(2, 1, d_a), wlb_hi, wlb_lo, a0.reshape(2, 1, d_a), alb_hi, alb_lo, k_a)


def _rwkv_kernel(opf_ref, opr_ref, vf_ref, vr_ref, pef_ref, per_ref, yf_ref, yr_ref, st_ref, *, d_a):
    n = HEAD_DIM
    heads = d_a // n
    c = CHUNK

    @pl.when(pl.program_id(1) == 0)
    def _():
        st_ref[...] = jnp.zeros_like(st_ref)

    ops = (opf_ref[0], opr_ref[0])
    vs = (vf_ref[0], vr_ref[0])
    pes = (pef_ref[0, 0, 0:1, :], per_ref[0, 0, 0:1, :])
    chains = [(d, h) for d in range(2) for h in range(heads)]
    part = lambda i: [ops[d][:, i * d_a + h * n:i * d_a + (h + 1) * n] for d, h in chains]
    each = lambda f, *ls: [f(*xs) for xs in zip(*ls)]
    rows2 = lambda a, b: jnp.concatenate([a, b], axis=0)
    ar = each(rows2, part(0), part(1))
    bk = each(rows2, part(2), part(3))
    bk_end = each(rows2, part(4), part(5))
    v = [vs[d][:, h * n:(h + 1) * n] for d, h in chains]
    p_end = [pes[d][:, h * n:(h + 1) * n] for d, h in chains]

    row = lax.broadcasted_iota(jnp.int32, (c, 2 * c), 0)
    col = lax.broadcasted_iota(jnp.int32, (c, 2 * c), 1) & (c - 1)
    incl = [(row <= col) if d else (row >= col) for d, _ in chains]
    strict = [(row < col) if d else (row > col) for d, _ in chains]

    m = each(_bdot_nt, ar, bk)
    m_a = each(lambda mm, s: jnp.where(s, mm[:c], 0.0), m, strict)
    m_r = each(lambda mm, s: jnp.where(s, mm[c:], 0.0), m, incl)
    a_ab = each(lambda mm: mm[:, :c], m_a)
    a_ak = each(lambda mm: mm[:, c:], m_a)

    row = lax.broadcasted_iota(jnp.int32, (c, c), 0)
    col = lax.broadcasted_iota(jnp.int32, (c, c), 1)
    eye = jnp.where(row == col, 1.0, 0.0)
    sub_bits = int(math.log2(SUB))
    blk = (row >> sub_bits) == (col >> sub_bits)
    ad = each(lambda a: jnp.where(blk, a, 0.0), a_ab)
    ao = each(lambda a, b: a - b, a_ab, ad)
    acc_mul = lambda t, x: t + _bdot(t, x)
    sq = lambda x: _bdot(x, x)
    td = each(lambda a: eye + a, ad)
    x = ad
    for _ in range(sub_bits - 1):
        x = each(sq, x)
        td = each(acc_mul, td, x)
    x = each(_bdot, td, ao)
    tf = each(lambda a: eye + a, x)
    for _ in range(int(math.log2(c // SUB)) - 1):
        x = each(sq, x)
        tf = each(acc_mul, tf, x)
    t_inv = each(_bdot, tf, td)
    akv = each(_bdot, a_ak, v)

    s = [st_ref[d, h] for d, h in chains]
    ms = each(_bdot_nt, ar, s)
    u = each(lambda t, mm, w: _bdot(t, mm[:c] + w).astype(BF16), t_inv, ms, akv)
    uv = each(rows2, u, v)
    y = each(lambda mm, a, w: mm[c:] + _bdot(a, w), ms, m_r, uv)
    s_new = each(lambda ss, p, w, b: ss * p + _bdot_tn(w, b), s, p_end, uv, bk_end)
    for (d, h), val in zip(chains, s_new):
        st_ref[d, h] = val
    yf_ref[0] = jnp.concatenate(y[:heads], axis=1)
    yr_ref[0] = jnp.concatenate(y[heads:], axis=1)


def _rwkv_call(rkv, kk, lora, w0, wlb, a0, alb, k_a, *, n_ctx, lora_w, lora_a):
    bsz, tt, d3 = rkv.shape
    d_a = d3 // 3
    n_chunks = tt // CHUNK
    nc_ctx = n_ctx // CHUNK
    heads = d_a // HEAD_DIM
    op_f, op_r, v, pe_f, pe_r = _rwkv_prep_call(rkv, kk, lora, w0, wlb, a0, alb, k_a,
                                                lora_w=lora_w, lora_a=lora_a)

    rev_chunk = lambda j: jnp.where(j < nc_ctx, nc_ctx - 1 - j, n_chunks - 1 - (j - nc_ctx))
    fwd = lambda b, j: (b, j, 0)
    rev = lambda b, j: (b, rev_chunk(j), 0)
    y_shape = jax.ShapeDtypeStruct((bsz, tt, d_a), F32)
    return pl.pallas_call(
        functools.partial(_rwkv_kernel, d_a=d_a),
        out_shape=(y_shape, y_shape),
        grid=(bsz, n_chunks),
        in_specs=[pl.BlockSpec((1, CHUNK, N_OPERANDS * d_a), fwd),
                  pl.BlockSpec((1, CHUNK, N_OPERANDS * d_a), rev),
                  pl.BlockSpec((1, CHUNK, d_a), fwd),
                  pl.BlockSpec((1, CHUNK, d_a), rev),
                  pl.BlockSpec((1, 1, 8, d_a), lambda b, j: (b, j, 0, 0)),
                  pl.BlockSpec((1, 1, 8, d_a), lambda b, j: (b, rev_chunk(j), 0, 0))],
        out_specs=(pl.BlockSpec((1, CHUNK, d_a), fwd), pl.BlockSpec((1, CHUNK, d_a), rev)),
        scratch_shapes=[pltpu.VMEM((2, heads, HEAD_DIM, HEAD_DIM), F32)],
        compiler_params=pltpu.CompilerParams(dimension_semantics=("parallel", "arbitrary"),
                                             vmem_limit_bytes=VMEM_LIMIT),
        name="rwkv7_chunk_scan",
    )(op_f, op_r, v, v, pe_f, pe_r)


def _softmax_pv(heads_parts):
    ms = [functools.reduce(jnp.maximum, [jnp.max(s, axis=-1, keepdims=True) for s, _ in parts])
          for parts in heads_parts]
    ps = [[jnp.exp(s - m) for s, _ in parts] for parts, m in zip(heads_parts, ms)]
    dens = [sum(jnp.sum(p, axis=-1, keepdims=True) for p in pp) for pp in ps]
    nums = [sum(jnp.dot(p.astype(BF16), vals, preferred_element_type=F32) for p, (_, vals) in zip(pp, parts))
            for pp, parts in zip(ps, heads_parts)]
    return [num / den for num, den in zip(nums, dens)]


def _attn_kernel(q_ref, k_ref, v_ref, bias_ref, o_ref, *, n_ctx, rows, q_off, d_b):
    n = HEAD_DIM
    heads = d_b // n
    nq_ctx = n_ctx // GRID_W
    i = pl.program_id(1) + q_off
    nb = q_ref.shape[0]
    q = [q_ref[b] for b in range(nb)]
    kc = [k_ref[b, 0:n_ctx, :] for b in range(nb)]
    vc = [v_ref[b, 0:n_ctx, :] for b in range(nb)]
    units = [(b, h, slice(h * n, (h + 1) * n)) for b in range(nb) for h in range(heads)]

    def store(outs):
        for b in range(nb):
            o_ref[b] = jnp.concatenate(outs[b * heads:(b + 1) * heads], axis=1).astype(o_ref.dtype)

    def latent():
        li = i - nq_ctx
        r0 = jnp.clip(li - WIN_H // 2, 0, rows - WIN_H)
        start = pl.multiple_of(n_ctx + r0 * GRID_W, GRID_W)
        kl = [k_ref[b, pl.ds(start, WIN_H * GRID_W), :] for b in range(nb)]
        vl = [v_ref[b, pl.ds(start, WIN_H * GRID_W), :] for b in range(nb)]
        s_l = [_bdot_nt(q[b][:, sl], kl[b][:, sl]) + bias_ref[0, h] for b, h, sl in units]
        s_c = [_bdot_nt(q[b][:, sl], kc[b][:, sl]) for b, h, sl in units]
        store(_softmax_pv([[(sa, vl[b][:, sl]), (sb, vc[b][:, sl])]
                           for sa, sb, (b, h, sl) in zip(s_l, s_c, units)]))

    def context():
        store(_softmax_pv([[(_bdot_nt(q[b][:, sl], kc[b][:, sl]), vc[b][:, sl])] for b, h, sl in units]))

    if q_off == 0:
        pl.when(i >= nq_ctx)(latent)
        pl.when(i < nq_ctx)(context)
    else:
        latent()


def _bias_table(rpb):
    j = jnp.arange(GRID_W)
    col_start = jnp.clip(j - WIN_W // 2, 0, GRID_W - WIN_W)
    in_win = (j[None, :] >= col_start[:, None]) & (j[None, :] < col_start[:, None] + WIN_W)
    d_col = jnp.clip(j[None, :] - j[:, None], -(WIN_W - 1), WIN_W - 1) + (WIN_W - 1)
    tz = jnp.where(in_win[None, None], rpb[:, :, d_col].astype(F32), NEG_BIG)
    idx = jnp.arange(WIN_H)[:, None] + jnp.arange(WIN_H)[None, :]
    bt = tz[:, idx]
    heads = rpb.shape[0]
    return bt.transpose(1, 0, 3, 2, 4).reshape(WIN_H, heads, GRID_W, WIN_H * GRID_W)


def _attn_call(qkv, bias_tab, *, n_ctx, with_ctx):
    bsz, tt, d3 = qkv.shape
    d_b = d3 // 3
    heads = d_b // HEAD_DIM
    rows = (tt - n_ctx) // GRID_W
    nq_ctx = n_ctx // GRID_W
    q_off = 0 if with_ctx else nq_ctx
    n_q = tt // GRID_W - q_off
    nb = 2 if bsz % 2 == 0 else 1

    def variant(b, i):
        li = jnp.maximum(i + q_off - nq_ctx, 0)
        return (jnp.clip(li - WIN_H // 2, 0, rows - WIN_H) - li + WIN_H - 1, 0, 0, 0)

    return pl.pallas_call(
        functools.partial(_attn_kernel, n_ctx=n_ctx, rows=rows, q_off=q_off, d_b=d_b),
        out_shape=jax.ShapeDtypeStruct((bsz, n_q * GRID_W, d_b), BF16),
        grid=(bsz // nb, n_q),
        in_specs=[pl.BlockSpec((nb, GRID_W, d_b), lambda b, i: (b, i + q_off, 0)),
                  pl.BlockSpec((nb, tt, d_b), lambda b, i: (b, 0, 1)),
                  pl.BlockSpec((nb, tt, d_b), lambda b, i: (b, 0, 2)),
                  pl.BlockSpec((1, heads, GRID_W, WIN_H * GRID_W), variant)],
        out_specs=pl.BlockSpec((nb, GRID_W, d_b), lambda b, i: (b, i, 0)),
        compiler_params=pltpu.CompilerParams(dimension_semantics=("parallel", "arbitrary"),
                                             vmem_limit_bytes=VMEM_LIMIT),
        name="neighbourhood_attention",
    )(qkv, qkv, qkv, bias_tab)


def _merge_kernel(x_ref, mod_ref, y0_ref, y1_ref, rkv_ref, lora_ref, yb_ref, gates_ref,
                  rk_ref, lng_ref, lnb_ref, glb_ref, pa_ref, pb_ref, wo_ref, gs_ref, o_ref,
                  *, tm, n_ctx, t_off, d_a, lora_g):
    t = pl.program_id(1) + t_off
    is_ctx = _ctx_rows(t, tm, n_ctx)
    gs = gs_ref[...]
    inv_n = 1.0 / HEAD_DIM
    y = y0_ref[0] + y1_ref[0]
    mu = _group_sum2(y, gs) * inv_n
    yc = y - mu
    var = _group_sum2(yc * yc, gs) * inv_n
    yn = yc * lax.rsqrt(var + GN_EPS) * lng_ref[...] + lnb_ref[...]
    rkv = rkv_ref[0]
    r, k, v = rkv[:, :d_a], rkv[:, d_a:2 * d_a], rkv[:, 2 * d_a:]
    bonus = _group_sum2(r * k * rk_ref[...], gs) * v
    lg = lora_ref[0][:, lora_ref.shape[2] - lora_g:]
    g = _bdot(_sigmoid(lg), glb_ref[...])
    ya = (yn + bonus) * g
    gates = gates_ref[0].astype(F32)
    dm = gates.shape[1] // 2
    merged = gates[:, :dm] * _bdot(ya, pa_ref[...]) + gates[:, dm:] * _bdot(yb_ref[0], pb_ref[...])
    o_ref[0] = x_ref[0] + _mod_row(mod_ref, 2, is_ctx) * _bdot(merged, wo_ref[...])


def _merge_call(xs, modsel, y_f, y_r, rkv, lora, yb, gates, r_k, ln_g, ln_b, glb, pa, pb, wo, gs,
                *, n_ctx, with_ctx, lora_g, tm=256):
    bsz, tt, d = xs.shape
    d_a = rkv.shape[2] // 3
    t_off = 0 if with_ctx else n_ctx // tm
    n_t = tt // tm - t_off
    yb_off = 0 if with_ctx else -t_off
    row = lambda b, t: (b, t + t_off, 0)
    const = lambda b, t: (0, 0)
    return pl.pallas_call(
        functools.partial(_merge_kernel, tm=tm, n_ctx=n_ctx, t_off=t_off, d_a=d_a, lora_g=lora_g),
        out_shape=jax.ShapeDtypeStruct((bsz, n_t * tm, d), F32),
        grid=(bsz, n_t),
        in_specs=[pl.BlockSpec((1, tm, d), row),
                  pl.BlockSpec((1, 2, 8, d), lambda b, t: (b, 0, 0, 0)),
                  pl.BlockSpec((1, tm, d_a), row),
                  pl.BlockSpec((1, tm, d_a), row),
                  pl.BlockSpec((1, tm, 3 * d_a), row),
                  pl.BlockSpec((1, tm, lora.shape[2]), row),
                  pl.BlockSpec((1, tm, yb.shape[2]), lambda b, t: (b, t + t_off + yb_off, 0)),
                  pl.BlockSpec((1, tm, gates.shape[2]), row),
                  pl.BlockSpec((1, d_a), const),
                  pl.BlockSpec((1, d_a), const),
                  pl.BlockSpec((1, d_a), const),
                  pl.BlockSpec(glb.shape, const),
                  pl.BlockSpec(pa.shape, const),
                  pl.BlockSpec(pb.shape, const),
                  pl.BlockSpec(wo.shape, const),
                  pl.BlockSpec((LANE, LANE), const)],
        out_specs=pl.BlockSpec((1, tm, d), lambda b, t: (b, t, 0)),
        compiler_params=pltpu.CompilerParams(dimension_semantics=("parallel", "parallel"),
                                             vmem_limit_bytes=VMEM_LIMIT),
        name="branch_merge",
    )(xs, modsel, y_f, y_r, rkv, lora, yb, gates, r_k, ln_g, ln_b, glb, pa, pb, wo, gs)


def _route(sel, scores, n_experts):
    per = n_experts // N_GROUPS
    in_top = []
    for e in range(n_experts):
        g0 = (e // per) * per
        rank = 0.0
        for o in range(g0, g0 + per):
            if o == e:
                continue
            ahead = (sel[o] >= sel[e]) if o < e else (sel[o] > sel[e])
            rank = rank + jnp.where(ahead, 1.0, 0.0)
        in_top.append(rank < TOP_K)
    grp = [sum(jnp.where(in_top[e], sel[e], 0.0) for e in range(g * per, (g + 1) * per))
           for g in range(N_GROUPS)]
    best = jnp.zeros_like(grp[0], dtype=jnp.int32)
    best_s = grp[0]
    for g in range(1, N_GROUPS):
        better = grp[g] > best_s
        best = jnp.where(better, g, best)
        best_s = jnp.where(better, grp[g], best_s)
    chosen = [in_top[e] & (best == e // per) for e in range(n_experts)]
    den = sum(jnp.where(chosen[e], scores[e], 0.0) for e in range(n_experts))
    return [jnp.where(chosen[e], scores[e] / den, 0.0) for e in range(n_experts)], best


MOE_BLK = 128
POS_COL = 16


def _moe_route_kernel(x_ref, mod_ref, g_ref, rw_ref, rb_ref, h_ref, tab_ref, pos_ref, cnt_ref,
                      *, tm, n_ctx, n_experts):
    t = pl.program_id(1)
    is_ctx = _ctx_rows(t, tm, n_ctx)
    h = _rms_mod(x_ref[0], g_ref[...], _mod_row(mod_ref, 3, is_ctx), _mod_row(mod_ref, 4, is_ctx))
    h_ref[0] = h.astype(BF16)
    hh, hl = _split(h)
    wh, wl = _split(rw_ref[...])
    nt = lambda a, b: lax.dot_general(a, b, (((1,), (1,)), ((), ())), preferred_element_type=F32)
    logits = nt(wh, hh) + nt(wh, hl) + nt(wl, hh)
    scores = _sigmoid(logits)
    sel = scores + rb_ref[...]
    gates, best = _route([sel[i:i + 1, :] for i in range(n_experts)],
                         [scores[i:i + 1, :] for i in range(n_experts)], n_experts)

    in_grp = [jnp.where(best == g, 1.0, 0.0) for g in range(N_GROUPS)]
    grp8 = jnp.concatenate(in_grp + [jnp.zeros((8 - N_GROUPS, tm), F32)], axis=0).astype(BF16)
    row = lax.broadcasted_iota(jnp.int32, (tm, tm), 0)
    col = lax.broadcasted_iota(jnp.int32, (tm, tm), 1)
    upper = jnp.where(row <= col, 1.0, 0.0).astype(BF16)
    run = jnp.dot(grp8, upper, preferred_element_type=F32)
    pos = jnp.zeros((1, tm), F32)
    seg = jnp.zeros((1, 1), F32)
    cnts = []
    for g in range(N_GROUPS):
        cnt = run[g:g + 1, tm - 1:tm]
        cnts.append(cnt)
        pos = pos + in_grp[g] * (seg + run[g:g + 1, :] - 1.0)
        seg = seg + jnp.floor((cnt + (MOE_BLK - 1)) * (1.0 / MOE_BLK)) * MOE_BLK
    pos_ref[0, 0] = jnp.broadcast_to(pos, (8, tm))
    cnt_ref[0, 0] = jnp.concatenate([jnp.broadcast_to(c, (1, LANE)) for c in cnts]
                                    + [jnp.zeros((8 - N_GROUPS, LANE), F32)], axis=0)
    tab = jnp.concatenate(gates + [pos, jnp.zeros((LANE - n_experts - 1, tm), F32)], axis=0)
    tab_ref[0] = tab.T


def _moe_expert_kernel(lo_ref, hi_ref, x_ref, mod_ref, h_ref, tab_ref, pos_ref, w1_ref, w3_ref, w2_ref, o_ref,
                       hs_ref, gs_ref, acc_ref, *, tm, ts, n_ctx, n_experts):
    b = pl.program_id(0)
    t = pl.program_id(1)
    e = pl.program_id(2)
    n_t = pl.num_programs(1)

    @pl.when(e == 0)
    def _():
        pos = pos_ref[0, 0, 0:1, :].astype(jnp.int32)
        sel = jnp.where(lax.broadcasted_iota(jnp.int32, (ts, tm), 0) == pos, 1.0, 0.0).astype(BF16)
        hs_ref[...] = jnp.dot(sel, h_ref[0], preferred_element_type=F32).astype(BF16)
        tab = tab_ref[0]
        t1 = tab.astype(BF16)
        r1 = tab - t1.astype(F32)
        t2 = r1.astype(BF16)
        t3 = (r1 - t2.astype(F32)).astype(BF16)
        gs_ref[...] = (jnp.dot(sel, t1, preferred_element_type=F32) + jnp.dot(sel, t2, preferred_element_type=F32)
                       + jnp.dot(sel, t3, preferred_element_type=F32))
        acc_ref[...] = jnp.zeros_like(acc_ref)

    seg = (b * n_t + t) * N_GROUPS + e // (n_experts // N_GROUPS)
    lane = lax.broadcasted_iota(jnp.int32, (MOE_BLK, LANE), 1)

    def block(j, carry):
        rows = pl.ds(pl.multiple_of(j * MOE_BLK, MOE_BLK), MOE_BLK)
        hb = hs_ref[rows, :]
        a = jnp.dot(hb, w1_ref[0], preferred_element_type=F32)
        bb = jnp.dot(hb, w3_ref[0], preferred_element_type=F32)
        gcol = jnp.sum(jnp.where(lane == e, gs_ref[rows, :], 0.0), axis=-1, keepdims=True)
        hid = (a * _sigmoid(a) * bb * gcol).astype(BF16)
        acc_ref[rows, :] += jnp.dot(hid, w2_ref[0], preferred_element_type=F32)
        return carry

    lax.fori_loop(lo_ref[seg], hi_ref[seg], block, 0)

    @pl.when(e == n_experts - 1)
    def _():
        is_ctx = _ctx_rows(t, tm, n_ctx)
        pos = tab_ref[0][:, POS_COL:POS_COL + 1].astype(jnp.int32)
        back = jnp.where(lax.broadcasted_iota(jnp.int32, (tm, ts), 1) == pos, 1.0, 0.0).astype(BF16)
        y = jnp.dot(back, acc_ref[...].astype(BF16), preferred_element_type=F32)
        o_ref[0] = x_ref[0] + _mod_row(mod_ref, 5, is_ctx) * y


def _moe_call(xs, modsel, norm_g, router_wt, router_b, w1, w3, w2, *, n_ctx, tm):
    bsz, tt, d = xs.shape
    n_experts = router_wt.shape[0]
    n_t = tt // tm
    ts = tm + N_GROUPS * MOE_BLK
    row2 = lambda b, t: (b, t, 0)
    const2 = lambda b, t: (0, 0)
    h2, tab, pos, cnt = pl.pallas_call(
        functools.partial(_moe_route_kernel, tm=tm, n_ctx=n_ctx, n_experts=n_experts),
        out_shape=(jax.ShapeDtypeStruct((bsz, tt, d), BF16),
                   jax.ShapeDtypeStruct((bsz, tt, LANE), F32),
                   jax.ShapeDtypeStruct((bsz, n_t, 8, tm), F32),
                   jax.ShapeDtypeStruct((bsz, n_t, 8, LANE), F32)),
        grid=(bsz, n_t),
        in_specs=[pl.BlockSpec((1, tm, d), row2),
                  pl.BlockSpec((1, 2, 8, d), lambda b, t: (b, 0, 0, 0)),
                  pl.BlockSpec((1, d), const2),
                  pl.BlockSpec(router_wt.shape, const2),
                  pl.BlockSpec((n_experts, 1), const2)],
        out_specs=(pl.BlockSpec((1, tm, d), row2),
                   pl.BlockSpec((1, tm, LANE), row2),
                   pl.BlockSpec((1, 1, 8, tm), lambda b, t: (b, t, 0, 0)),
                   pl.BlockSpec((1, 1, 8, LANE), lambda b, t: (b, t, 0, 0))),
        compiler_params=pltpu.CompilerParams(dimension_semantics=("parallel", "parallel"),
                                             vmem_limit_bytes=VMEM_LIMIT),
        name="moe_route",
    )(xs, modsel, norm_g, router_wt, router_b)

    n_blk = (cnt[:, :, :N_GROUPS, 0].astype(jnp.int32) + (MOE_BLK - 1)) // MOE_BLK
    hi = jnp.cumsum(n_blk, axis=-1)
    lo = hi - n_blk

    row = lambda b, t, e, lo_r, hi_r: (b, t, 0)
    wexp = lambda b, t, e, lo_r, hi_r: (e, 0, 0)
    tile4 = lambda b, t, e, lo_r, hi_r: (b, t, 0, 0)
    return pl.pallas_call(
        functools.partial(_moe_expert_kernel, tm=tm, ts=ts, n_ctx=n_ctx, n_experts=n_experts),
        out_shape=jax.ShapeDtypeStruct((bsz, tt, d), F32),
        grid_spec=pltpu.PrefetchScalarGridSpec(
            num_scalar_prefetch=2,
            grid=(bsz, n_t, n_experts),
            in_specs=[pl.BlockSpec((1, tm, d), row),
                      pl.BlockSpec((1, 2, 8, d), lambda b, t, e, lo_r, hi_r: (b, 0, 0, 0)),
                      pl.BlockSpec((1, tm, d), row),
                      pl.BlockSpec((1, tm, LANE), row),
                      pl.BlockSpec((1, 1, 8, tm), tile4),
                      pl.BlockSpec((1,) + w1.shape[1:], wexp),
                      pl.BlockSpec((1,) + w3.shape[1:], wexp),
                      pl.BlockSpec((1,) + w2.shape[1:], wexp)],
            out_specs=pl.BlockSpec((1, tm, d), row),
            scratch_shapes=[pltpu.VMEM((ts, d), BF16),
                            pltpu.VMEM((ts, LANE), F32),
                            pltpu.VMEM((ts, d), F32)]),
        compiler_params=pltpu.CompilerParams(
            dimension_semantics=("parallel", "parallel", "arbitrary"),
            vmem_limit_bytes=VMEM_LIMIT),
        name="moe_experts",
    )(lo.reshape(-1), hi.reshape(-1), xs, modsel, h2, tab, pos, w1, w3, w2)


def kernel(x, c, ctx, c_ctx, mod_w, mod_b, norm1_g, norm2_g, w_in, rw_w0, rw_w_lora_b, rw_a0, rw_a_lora_b,
           rw_g_lora_b, rw_k_k, rw_k_a, rw_r_k, rw_ln_g, rw_ln_b, na_q_g, na_k_g, na_rpb, proj_a, proj_b,
           w_out, router_w, router_bias, moe_w1, moe_w3, moe_w2):
    bsz, seq, d = x.shape
    n_ctx = ctx.shape[1]
    depth = mod_w.shape[0]
    d_a = rw_w0.shape[2]
    d_b = proj_b.shape[1]
    lora_w = rw_w_lora_b.shape[2]
    lora_a = rw_a_lora_b.shape[2]
    lora_g = rw_g_lora_b.shape[1]
    n_lora = 2 * lora_w + 2 * lora_a + lora_g
    heads_a = d_a // HEAD_DIM
    heads_b = d_b // HEAD_DIM
    assert seq % (GRID_W * WIN_H) == 0 and n_ctx % 256 == 0 and bsz + 1 <= 16

    cs = jnp.zeros((16, d), F32).at[:bsz].set(c).at[bsz].set(c_ctx)
    mod = _mod_call(cs, mod_w, mod_b).reshape(depth, 16, 6, d)

    lane = jnp.arange(LANE)
    gs = (lane[:, None] // HEAD_DIM == lane[None, :] // HEAD_DIM).astype(BF16)
    router_wt = router_w.T
    router_b = router_bias.reshape(-1, 1)

    xs = jnp.concatenate([ctx, x], axis=1)
    for l in range(depth):
        last = l == depth - 1
        m_c = jnp.broadcast_to(mod[l, bsz][None], (bsz, 6, d))
        modsel = jnp.pad(jnp.stack([m_c, mod[l, :bsz]], axis=1), ((0, 0), (0, 0), (0, 2), (0, 0)))

        rkv, kk, lora, qkv, gates = _in_proj_call(
            xs, modsel, norm1_g[l][None], w_in[l].astype(BF16), rw_k_k[l][None],
            jnp.tile(na_q_g[l], heads_b)[None], jnp.tile(na_k_g[l], heads_b)[None], gs,
            n_ctx=n_ctx, d_a=d_a, d_b=d_b, n_lora=n_lora)
        y_f, y_r = _rwkv_call(rkv, kk, lora, rw_w0[l], rw_w_lora_b[l], rw_a0[l], rw_a_lora_b[l],
                              rw_k_a[l][None], n_ctx=n_ctx, lora_w=lora_w, lora_a=lora_a)
        yb = _attn_call(qkv, _bias_table(na_rpb[l]), n_ctx=n_ctx, with_ctx=not last)
        xs = _merge_call(xs, modsel, y_f, y_r, rkv, lora, yb, gates, rw_r_k[l].reshape(1, d_a),
                         rw_ln_g[l][None], rw_ln_b[l][None], rw_g_lora_b[l].astype(BF16),
                         proj_a[l].astype(BF16), proj_b[l].astype(BF16), w_out[l].astype(BF16), gs,
                         n_ctx=n_ctx, with_ctx=not last, lora_g=lora_g)
        moe_ctx = 0 if last else n_ctx
        tm = next(m for m in (1024, 768, 512, 256) if xs.shape[1] % m == 0)
        xs = _moe_call(xs, modsel, norm2_g[l][None], router_wt, router_b,
                       moe_w1[l].astype(BF16), moe_w3[l].astype(BF16), moe_w2[l].astype(BF16),
                       n_ctx=moe_ctx, tm=tm)
    return xs
```

```python
import functools
import math

import jax
import jax.numpy as jnp
import numpy as np
from jax import lax
from jax.experimental import pallas as pl
from jax.experimental.pallas import tpu as pltpu

F32 = jnp.float32
BF16 = jnp.bfloat16

HEAD_DIM = 64
GRID_W = 64
WIN_H = 8
WIN_W = 16
N_GROUPS = 4
TOP_K = 2
RMS_EPS = 1e-6
GN_EPS = 64e-5
LANE = 128
CHUNK = 64
SUB = 16
NEG_BIG = -1e30
DECAY_SCALE = math.exp(-0.5)
VMEM_LIMIT = 56 * 1024 * 1024


def _bdot(a, b):
    return jnp.dot(a.astype(BF16), b.astype(BF16), preferred_element_type=F32)


def _bdot_nt(a, b):
    return lax.dot_general(a.astype(BF16), b.astype(BF16), (((1,), (1,)), ((), ())),
                           preferred_element_type=F32)


def _bdot_tn(a, b):
    return lax.dot_general(a.astype(BF16), b.astype(BF16), (((0,), (0,)), ((), ())),
                           preferred_element_type=F32)


def _split(x):
    hi = x.astype(BF16)
    lo = (x - hi.astype(F32)).astype(BF16)
    return hi, lo


def _dot3(a, b):
    ah, al = _split(a)
    bh, bl = _split(b)
    return (jnp.dot(ah, bh, preferred_element_type=F32)
            + jnp.dot(al, bh, preferred_element_type=F32)
            + jnp.dot(ah, bl, preferred_element_type=F32))


def _sigmoid(x):
    return 1.0 / (1.0 + jnp.exp(-x))


def _group_sum(x, g128):
    parts = [_bdot(x[:, j * LANE:(j + 1) * LANE], g128) for j in range(x.shape[1] // LANE)]
    return jnp.concatenate(parts, axis=1)


def _group_sum2(x, g128):
    hi, lo = _split(x)
    parts = []
    for j in range(x.shape[1] // LANE):
        sl = slice(j * LANE, (j + 1) * LANE)
        parts.append(jnp.dot(hi[:, sl], g128, preferred_element_type=F32)
                     + jnp.dot(lo[:, sl], g128, preferred_element_type=F32))
    return jnp.concatenate(parts, axis=1)


def _mod_row(mod_ref, idx, is_ctx):
    mx = mod_ref[0, 1, idx:idx + 1, :]
    if is_ctx is None:
        return mx
    return jnp.where(is_ctx, mod_ref[0, 0, idx:idx + 1, :], mx)


def _ctx_rows(tile, tm, n_ctx):
    if n_ctx == 0:
        return None
    rows = tile * tm + lax.broadcasted_iota(jnp.int32, (tm, 1), 0)
    return rows < n_ctx


def _rms_mod(x, gain, shift, scale):
    xn = x * lax.rsqrt(jnp.mean(x * x, axis=-1, keepdims=True) + RMS_EPS) * gain
    return xn * (1.0 + scale) + shift


def _mod_kernel(c_ref, w_ref, b_ref, o_ref):
    c = c_ref[...]
    o_ref[0] = _dot3(c * _sigmoid(c), w_ref[0]) + b_ref[0]


def _mod_call(cs, mod_w, mod_b):
    depth, d, n = mod_w.shape
    tn = n // 4
    return pl.pallas_call(
        _mod_kernel,
        out_shape=jax.ShapeDtypeStruct((depth, cs.shape[0], n), F32),
        grid=(depth, n // tn),
        in_specs=[pl.BlockSpec(cs.shape, lambda l, j: (0, 0)),
                  pl.BlockSpec((1, d, tn), lambda l, j: (l, 0, j)),
                  pl.BlockSpec((1, 1, tn), lambda l, j: (l, 0, j))],
        out_specs=pl.BlockSpec((1, cs.shape[0], tn), lambda l, j: (l, 0, j)),
        compiler_params=pltpu.CompilerParams(dimension_semantics=("parallel", "parallel"),
                                             vmem_limit_bytes=VMEM_LIMIT),
        name="adaln_vectors",
    )(cs, mod_w, mod_b.reshape(depth, 1, n))


def _in_proj_kernel(x_ref, mod_ref, g_ref, w_ref, kkg_ref, qg_ref, kg_ref, gs_ref,
                    rkv_ref, kk_ref, lora_ref, qkv_ref, gates_ref, *, tm, n_ctx, d_a, d_b, n_lora):
    t = pl.program_id(1)
    is_ctx = _ctx_rows(t, tm, n_ctx)
    h = _rms_mod(x_ref[0], g_ref[...], _mod_row(mod_ref, 0, is_ctx), _mod_row(mod_ref, 1, is_ctx))
    h = h.astype(BF16)
    gs = gs_ref[...]

    o = 0
    rkv = jnp.dot(h, w_ref[:, o:o + 3 * d_a], preferred_element_type=F32)
    rkv_ref[0] = rkv
    kkv = rkv[:, d_a:2 * d_a] * kkg_ref[...]
    ss = _group_sum2(kkv * kkv, gs)
    kk_ref[0] = kkv * lax.rsqrt(jnp.maximum(ss, 1e-12))
    o += 3 * d_a

    lora_ref[0] = jnp.dot(h, w_ref[:, o:o + n_lora], preferred_element_type=F32)
    o += n_lora

    qkv = jnp.dot(h, w_ref[:, o:o + 3 * d_b], preferred_element_type=F32)
    q, k = qkv[:, :d_b], qkv[:, d_b:2 * d_b]
    inv_n = 1.0 / HEAD_DIM
    qn = q * lax.rsqrt(_group_sum(q * q, gs) * inv_n + RMS_EPS) * qg_ref[...] * (HEAD_DIM ** -0.5)
    kn = k * lax.rsqrt(_group_sum(k * k, gs) * inv_n + RMS_EPS) * kg_ref[...]
    qkv_ref[0] = jnp.concatenate([qn, kn, qkv[:, 2 * d_b:]], axis=1).astype(BF16)
    o += 3 * d_b

    gates_ref[0] = _sigmoid(jnp.dot(h, w_ref[:, o:], preferred_element_type=F32)).astype(BF16)


def _in_proj_call(xs, modsel, norm_g, w_in, k_k, q_g, k_g, gs, *, n_ctx, d_a, d_b, n_lora, tm=256):
    bsz, tt, d = xs.shape
    n_gate = w_in.shape[1] - 3 * d_a - n_lora - 3 * d_b
    row = lambda b, t: (b, t, 0)
    const = lambda b, t: (0, 0)
    return pl.pallas_call(
        functools.partial(_in_proj_kernel, tm=tm, n_ctx=n_ctx, d_a=d_a, d_b=d_b, n_lora=n_lora),
        out_shape=(jax.ShapeDtypeStruct((bsz, tt, 3 * d_a), F32),
                   jax.ShapeDtypeStruct((bsz, tt, d_a), F32),
                   jax.ShapeDtypeStruct((bsz, tt, n_lora), F32),
                   jax.ShapeDtypeStruct((bsz, tt, 3 * d_b), BF16),
                   jax.ShapeDtypeStruct((bsz, tt, n_gate), BF16)),
        grid=(bsz, tt // tm),
        in_specs=[pl.BlockSpec((1, tm, d), row),
                  pl.BlockSpec((1, 2, 8, d), lambda b, t: (b, 0, 0, 0)),
                  pl.BlockSpec((1, d), const),
                  pl.BlockSpec(w_in.shape, const),
                  pl.BlockSpec((1, d_a), const),
                  pl.BlockSpec((1, d_b), const),
                  pl.BlockSpec((1, d_b), const),
                  pl.BlockSpec((LANE, LANE), const)],
        out_specs=(pl.BlockSpec((1, tm, 3 * d_a), row),
                   pl.BlockSpec((1, tm, d_a), row),
                   pl.BlockSpec((1, tm, n_lora), row),
                   pl.BlockSpec((1, tm, 3 * d_b), row),
                   pl.BlockSpec((1, tm, n_gate), row)),
        compiler_params=pltpu.CompilerParams(dimension_semantics=("parallel", "parallel"),
                                             vmem_limit_bytes=VMEM_LIMIT),
        name="in_proj",
    )(xs, modsel, norm_g, w_in, k_k, q_g, k_g, gs)


N_OPERANDS = 6


def _dot3_presplit(x, w_hi, w_lo):
    xh, xl = _split(x)
    return (jnp.dot(xh, w_hi, preferred_element_type=F32) + jnp.dot(xl, w_hi, preferred_element_type=F32)
            + jnp.dot(xh, w_lo, preferred_element_type=F32))


def _rwkv_prep_kernel(rkv_ref, kk_ref, lora_ref, w0_ref, wlbh_ref, wlbl_ref, a0_ref, albh_ref, albl_ref, ka_ref,
                      opf_ref, opr_ref, v_ref, pef_ref, per_ref, *, tm, d_a, lora_w, lora_a):
    rkv = rkv_ref[0]
    r, k, v = rkv[:, :d_a], rkv[:, d_a:2 * d_a], rkv[:, 2 * d_a:]
    kk = kk_ref[0]
    lora = lora_ref[0]
    v_ref[0] = v.astype(BF16)
    chunk_bits = int(math.log2(CHUNK))
    row = lax.broadcasted_iota(jnp.int32, (tm, tm), 0)
    col = lax.broadcasted_iota(jnp.int32, (tm, tm), 1)
    same = (row >> chunk_bits) == (col >> chunk_bits)
    ones_blk = jnp.where(same, 1.0, 0.0).astype(BF16)
    for d, (op_ref, pe_ref) in enumerate(((opf_ref, pef_ref), (opr_ref, per_ref))):
        lw = lora[:, d * lora_w:(d + 1) * lora_w]
        la = lora[:, 2 * lora_w + d * lora_a:2 * lora_w + (d + 1) * lora_a]
        wl = w0_ref[d] + _dot3_presplit(jnp.tanh(lw), wlbh_ref[d], wlbl_ref[d])
        ld = -DECAY_SCALE * _sigmoid(wl)
        al = _sigmoid(a0_ref[d] + _dot3_presplit(la, albh_ref[d], albl_ref[d]))
        kd = k * (1.0 + (al - 1.0) * ka_ref[...])
        before = (row <= col) if d else (row >= col)
        tri = jnp.where(same & before, 1.0, 0.0).astype(BF16)
        ld_hi, ld_lo = _split(ld)
        cum = jnp.dot(tri, ld_hi, preferred_element_type=F32) + jnp.dot(tri, ld_lo, preferred_element_type=F32)
        tot = (jnp.dot(ones_blk, ld_hi, preferred_element_type=F32)
               + jnp.dot(ones_blk, ld_lo, preferred_element_type=F32))
        p_end = jnp.exp(tot)
        p_inv = jnp.exp(-cum)
        bt = kk * al * p_inv
        kt = kd * p_inv
        op_ref[0] = jnp.concatenate([-kk * jnp.exp(cum - ld), r * jnp.exp(cum), bt, kt, bt * p_end, kt * p_end],
                                    axis=1).astype(BF16)
        for ci in range(tm // CHUNK):
            pe_ref[0, ci] = p_end[ci * CHUNK:ci * CHUNK + 8]


def _rwkv_prep_call(rkv, kk, lora, w0, wlb, a0, alb, k_a, *, lora_w, lora_a, tm=256):
    bsz, tt, d3 = rkv.shape
    d_a = d3 // 3
    row = lambda b, t: (b, t, 0)
    const3 = lambda b, t: (0, 0, 0)
    wlb_hi, wlb_lo = _split(wlb)
    alb_hi, alb_lo = _split(alb)
    op_shape = jax.ShapeDtypeStruct((bsz, tt, N_OPERANDS * d_a), BF16)
    pe_shape = jax.ShapeDtypeStruct((bsz, tt // CHUNK, 8, d_a), F32)
    pe_spec = pl.BlockSpec((1, tm // CHUNK, 8, d_a), lambda b, t: (b, t, 0, 0))
    return pl.pallas_call(
        functools.partial(_rwkv_prep_kernel, tm=tm, d_a=d_a, lora_w=lora_w, lora_a=lora_a),
        out_shape=(op_shape, op_shape, jax.ShapeDtypeStruct((bsz, tt, d_a), BF16), pe_shape, pe_shape),
        grid=(bsz, tt // tm),
        in_specs=[pl.BlockSpec((1, tm, d3), row),
                  pl.BlockSpec((1, tm, d_a), row),
                  pl.BlockSpec((1, tm, lora.shape[2]), row),
                  pl.BlockSpec((2, 1, d_a), const3),
                  pl.BlockSpec((2, lora_w, d_a), const3),
                  pl.BlockSpec((2, lora_w, d_a), const3),
                  pl.BlockSpec((2, 1, d_a), const3),
                  pl.BlockSpec((2, lora_a, d_a), const3),
                  pl.BlockSpec((2, lora_a, d_a), const3),
                  pl.BlockSpec((1, d_a), lambda b, t: (0, 0))],
        out_specs=(pl.BlockSpec((1, tm, N_OPERANDS * d_a), row),
                   pl.BlockSpec((1, tm, N_OPERANDS * d_a), row),
                   pl.BlockSpec((1, tm, d_a), row), pe_spec, pe_spec),
        compiler_params=pltpu.CompilerParams(dimension_semantics=("parallel", "parallel"),
                                             vmem_limit_bytes=VMEM_LIMIT),
        name="rwkv7_operands",
    )(rkv, kk, lora, w0.reshape(2, 1, d_a), wlb_hi, wlb_lo, a0.reshape(2, 1, d_a), alb_hi, alb_lo, k_a)


def _rwkv_kernel(opf_ref, opr_ref, vf_ref, vr_ref, pef_ref, per_ref, yf_ref, yr_ref, st_ref, *, d_a):
    n = HEAD_DIM
    heads = d_a // n
    c = CHUNK

    @pl.when(pl.program_id(1) == 0)
    def _():
        st_ref[...] = jnp.zeros_like(st_ref)

    nb = opf_ref.shape[0]
    ops = [(opf_ref[b], opr_ref[b]) for b in range(nb)]
    vs = [(vf_ref[b], vr_ref[b]) for b in range(nb)]
    pes = [(pef_ref[b, 0, 0:1, :], per_ref[b, 0, 0:1, :]) for b in range(nb)]
    chains = [(b, d, h) for b in range(nb) for d in range(2) for h in range(heads)]
    part = lambda i: [ops[b][d][:, i * d_a + h * n:i * d_a + (h + 1) * n] for b, d, h in chains]
    each = lambda f, *ls: [f(*xs) for xs in zip(*ls)]
    rows2 = lambda a, b: jnp.concatenate([a, b], axis=0)
    ar = each(rows2, part(0), part(1))
    bk = each(rows2, part(2), part(3))
    bk_end = each(rows2, part(4), part(5))
    v = [vs[b][d][:, h * n:(h + 1) * n] for b, d, h in chains]
    p_end = [pes[b][d][:, h * n:(h + 1) * n] for b, d, h in chains]

    row = lax.broadcasted_iota(jnp.int32, (c, 2 * c), 0)
    col = lax.broadcasted_iota(jnp.int32, (c, 2 * c), 1) & (c - 1)
    incl = [(row <= col) if d else (row >= col) for _, d, _ in chains]
    strict = [(row < col) if d else (row > col) for _, d, _ in chains]

    m = each(_bdot_nt, ar, bk)
    m_a = each(lambda mm, s: jnp.where(s, mm[:c], 0.0), m, strict)
    m_r = each(lambda mm, s: jnp.where(s, mm[c:], 0.0), m, incl)
    a_ab = each(lambda mm: mm[:, :c], m_a)
    a_ak = each(lambda mm: mm[:, c:], m_a)

    row = lax.broadcasted_iota(jnp.int32, (c, c), 0)
    col = lax.broadcasted_iota(jnp.int32, (c, c), 1)
    eye = jnp.where(row == col, 1.0, 0.0)
    sub_bits = int(math.log2(SUB))
    blk = (row >> sub_bits) == (col >> sub_bits)
    ad = each(lambda a: jnp.where(blk, a, 0.0), a_ab)
    ao = each(lambda a, b: a - b, a_ab, ad)
    acc_mul = lambda t, x: t + _bdot(t, x)
    sq = lambda x: _bdot(x, x)
    td = each(lambda a: eye + a, ad)
    x = ad
    for _ in range(sub_bits - 1):
        x = each(sq, x)
        td = each(acc_mul, td, x)
    x = each(_bdot, td, ao)
    tf = each(lambda a: eye + a, x)
    for _ in range(int(math.log2(c // SUB)) - 1):
        x = each(sq, x)
        tf = each(acc_mul, tf, x)
    t_inv = each(_bdot, tf, td)
    akv = each(_bdot, a_ak, v)

    s = [st_ref[b, d, h] for b, d, h in chains]
    ms = each(_bdot_nt, ar, s)
    u = each(lambda t, mm, w: _bdot(t, mm[:c] + w).astype(BF16), t_inv, ms, akv)
    uv = each(rows2, u, v)
    y = each(lambda mm, a, w: mm[c:] + _bdot(a, w), ms, m_r, uv)
    s_new = each(lambda ss, p, w, b: ss * p + _bdot_tn(w, b), s, p_end, uv, bk_end)
    for (b, d, h), val in zip(chains, s_new):
        st_ref[b, d, h] = val
    for b in range(nb):
        yf_ref[b] = jnp.concatenate(y[2 * b * heads:(2 * b + 1) * heads], axis=1)
        yr_ref[b] = jnp.concatenate(y[(2 * b + 1) * heads:(2 * b + 2) * heads], axis=1)


def _rwkv_call(rkv, kk, lora, w0, wlb, a0, alb, k_a, *, n_ctx, lora_w, lora_a):
    bsz, tt, d3 = rkv.shape
    d_a = d3 // 3
    n_chunks = tt // CHUNK
    nc_ctx = n_ctx // CHUNK
    heads = d_a // HEAD_DIM
    op_f, op_r, v, pe_f, pe_r = _rwkv_prep_call(rkv, kk, lora, w0, wlb, a0, alb, k_a,
                                                lora_w=lora_w, lora_a=lora_a)

    rev_chunk = lambda j: jnp.where(j < nc_ctx, nc_ctx - 1 - j, n_chunks - 1 - (j - nc_ctx))
    fwd = lambda b, j: (b, j, 0)
    rev = lambda b, j: (b, rev_chunk(j), 0)
    y_shape = jax.ShapeDtypeStruct((bsz, tt, d_a), F32)
    nb = 2 if bsz % 2 == 0 else 1
    return pl.pallas_call(
        functools.partial(_rwkv_kernel, d_a=d_a),
        out_shape=(y_shape, y_shape),
        grid=(bsz // nb, n_chunks),
        in_specs=[pl.BlockSpec((nb, CHUNK, N_OPERANDS * d_a), fwd),
                  pl.BlockSpec((nb, CHUNK, N_OPERANDS * d_a), rev),
                  pl.BlockSpec((nb, CHUNK, d_a), fwd),
                  pl.BlockSpec((nb, CHUNK, d_a), rev),
                  pl.BlockSpec((nb, 1, 8, d_a), lambda b, j: (b, j, 0, 0)),
                  pl.BlockSpec((nb, 1, 8, d_a), lambda b, j: (b, rev_chunk(j), 0, 0))],
        out_specs=(pl.BlockSpec((nb, CHUNK, d_a), fwd), pl.BlockSpec((nb, CHUNK, d_a), rev)),
        scratch_shapes=[pltpu.VMEM((nb, 2, heads, HEAD_DIM, HEAD_DIM), F32)],
        compiler_params=pltpu.CompilerParams(dimension_semantics=("parallel", "arbitrary"),
                                             vmem_limit_bytes=VMEM_LIMIT),
        name="rwkv7_chunk_scan",
    )(op_f, op_r, v, v, pe_f, pe_r)


def _softmax_pv(heads_parts):
    ms = [functools.reduce(jnp.maximum, [jnp.max(s, axis=-1, keepdims=True) for s, _ in parts])
          for parts in heads_parts]
    ps = [[jnp.exp(s - m) for s, _ in parts] for parts, m in zip(heads_parts, ms)]
    dens = [sum(jnp.sum(p, axis=-1, keepdims=True) for p in pp) for pp in ps]
    nums = [sum(jnp.dot(p.astype(BF16), vals, preferred_element_type=F32) for p, (_, vals) in zip(pp, parts))
            for pp, parts in zip(ps, heads_parts)]
    return [num / den for num, den in zip(nums, dens)]


def _attn_kernel(q_ref, k_ref, v_ref, bias_ref, o_ref, *, n_ctx, rows, q_off, d_b):
    n = HEAD_DIM
    heads = d_b // n
    nq_ctx = n_ctx // GRID_W
    i = pl.program_id(1) + q_off
    nb = q_ref.shape[0]
    q = [q_ref[b] for b in range(nb)]
    kc = [k_ref[b, 0:n_ctx, :] for b in range(nb)]
    vc = [v_ref[b, 0:n_ctx, :] for b in range(nb)]
    units = [(b, h, slice(h * n, (h + 1) * n)) for b in range(nb) for h in range(heads)]

    def store(outs):
        for b in range(nb):
            o_ref[b] = jnp.concatenate(outs[b * heads:(b + 1) * heads], axis=1).astype(o_ref.dtype)

    def latent():
        li = i - nq_ctx
        r0 = jnp.clip(li - WIN_H // 2, 0, rows - WIN_H)
        start = pl.multiple_of(n_ctx + r0 * GRID_W, GRID_W)
        kl = [k_ref[b, pl.ds(start, WIN_H * GRID_W), :] for b in range(nb)]
        vl = [v_ref[b, pl.ds(start, WIN_H * GRID_W), :] for b in range(nb)]
        s_l = [_bdot_nt(q[b][:, sl], kl[b][:, sl]) + bias_ref[0, h] for b, h, sl in units]
        s_c = [_bdot_nt(q[b][:, sl], kc[b][:, sl]) for b, h, sl in units]
        store(_softmax_pv([[(sa, vl[b][:, sl]), (sb, vc[b][:, sl])]
                           for sa, sb, (b, h, sl) in zip(s_l, s_c, units)]))

    def context():
        store(_softmax_pv([[(_bdot_nt(q[b][:, sl], kc[b][:, sl]), vc[b][:, sl])] for b, h, sl in units]))

    if q_off == 0:
        pl.when(i >= nq_ctx)(latent)
        pl.when(i < nq_ctx)(context)
    else:
        latent()


def _bias_table(rpb):
    w = GRID_W
    heads = rpb.shape[0]
    j = np.arange(w)
    col_start = np.clip(j - WIN_W // 2, 0, w - WIN_W)
    in_win = (j[None, :] >= col_start[:, None]) & (j[None, :] < col_start[:, None] + WIN_W)
    rpb = rpb.astype(F32)
    ext = jnp.concatenate([jnp.repeat(rpb[..., :1], w - WIN_W, axis=-1), rpb,
                           jnp.repeat(rpb[..., -1:], w - WIN_W, axis=-1)], axis=-1)
    toe = jnp.stack([ext[..., w - 1 - qc:2 * w - 1 - qc] for qc in range(w)], axis=2)
    tz = jnp.where(in_win, toe, NEG_BIG)
    bt = jnp.stack([tz[:, v:v + WIN_H] for v in range(WIN_H)], axis=0)
    return bt.transpose(0, 1, 3, 2, 4).reshape(WIN_H, heads, w, WIN_H * w)


def _attn_call(qkv, bias_tab, *, n_ctx, with_ctx):
    bsz, tt, d3 = qkv.shape
    d_b = d3 // 3
    heads = d_b // HEAD_DIM
    rows = (tt - n_ctx) // GRID_W
    nq_ctx = n_ctx // GRID_W
    q_off = 0 if with_ctx else nq_ctx
    n_q = tt // GRID_W - q_off
    nb = 2 if bsz % 2 == 0 else 1

    def variant(b, i):
        li = jnp.maximum(i + q_off - nq_ctx, 0)
        return (jnp.clip(li - WIN_H // 2, 0, rows - WIN_H) - li + WIN_H - 1, 0, 0, 0)

    return pl.pallas_call(
        functools.partial(_attn_kernel, n_ctx=n_ctx, rows=rows, q_off=q_off, d_b=d_b),
        out_shape=jax.ShapeDtypeStruct((bsz, n_q * GRID_W, d_b), BF16),
        grid=(bsz // nb, n_q),
        in_specs=[pl.BlockSpec((nb, GRID_W, d_b), lambda b, i: (b, i + q_off, 0)),
                  pl.BlockSpec((nb, tt, d_b), lambda b, i: (b, 0, 1)),
                  pl.BlockSpec((nb, tt, d_b), lambda b, i: (b, 0, 2)),
                  pl.BlockSpec((1, heads, GRID_W, WIN_H * GRID_W), variant)],
        out_specs=pl.BlockSpec((nb, GRID_W, d_b), lambda b, i: (b, i, 0)),
        compiler_params=pltpu.CompilerParams(dimension_semantics=("parallel", "arbitrary"),
                                             vmem_limit_bytes=VMEM_LIMIT),
        name="neighbourhood_attention",
    )(qkv, qkv, qkv, bias_tab)


def _merge_kernel(x_ref, mod_ref, y0_ref, y1_ref, rkv_ref, lora_ref, yb_ref, gates_ref,
                  rk_ref, lng_ref, lnb_ref, glb_ref, pa_ref, pb_ref, wo_ref, gs_ref, o_ref,
                  *, tm, n_ctx, t_off, d_a, lora_g):
    t = pl.program_id(1) + t_off
    is_ctx = _ctx_rows(t, tm, n_ctx)
    gs = gs_ref[...]
    inv_n = 1.0 / HEAD_DIM
    y = y0_ref[0] + y1_ref[0]
    mu = _group_sum2(y, gs) * inv_n
    yc = y - mu
    var = _group_sum2(yc * yc, gs) * inv_n
    yn = yc * lax.rsqrt(var + GN_EPS) * lng_ref[...] + lnb_ref[...]
    rkv = rkv_ref[0]
    r, k, v = rkv[:, :d_a], rkv[:, d_a:2 * d_a], rkv[:, 2 * d_a:]
    bonus = _group_sum2(r * k * rk_ref[...], gs) * v
    lg = lora_ref[0][:, lora_ref.shape[2] - lora_g:]
    g = _bdot(_sigmoid(lg), glb_ref[...])
    ya = (yn + bonus) * g
    gates = gates_ref[0].astype(F32)
    dm = gates.shape[1] // 2
    merged = gates[:, :dm] * _bdot(ya, pa_ref[...]) + gates[:, dm:] * _bdot(yb_ref[0], pb_ref[...])
    o_ref[0] = x_ref[0] + _mod_row(mod_ref, 2, is_ctx) * _bdot(merged, wo_ref[...])


def _merge_call(xs, modsel, y_f, y_r, rkv, lora, yb, gates, r_k, ln_g, ln_b, glb, pa, pb, wo, gs,
                *, n_ctx, with_ctx, lora_g, tm=256):
    bsz, tt, d = xs.shape
    d_a = rkv.shape[2] // 3
    t_off = 0 if with_ctx else n_ctx // tm
    n_t = tt // tm - t_off
    yb_off = 0 if with_ctx else -t_off
    row = lambda b, t: (b, t + t_off, 0)
    const = lambda b, t: (0, 0)
    return pl.pallas_call(
        functools.partial(_merge_kernel, tm=tm, n_ctx=n_ctx, t_off=t_off, d_a=d_a, lora_g=lora_g),
        out_shape=jax.ShapeDtypeStruct((bsz, n_t * tm, d), F32),
        grid=(bsz, n_t),
        in_specs=[pl.BlockSpec((1, tm, d), row),
                  pl.BlockSpec((1, 2, 8, d), lambda b, t: (b, 0, 0, 0)),
                  pl.BlockSpec((1, tm, d_a), row),
                  pl.BlockSpec((1, tm, d_a), row),
                  pl.BlockSpec((1, tm, 3 * d_a), row),
                  pl.BlockSpec((1, tm, lora.shape[2]), row),
                  pl.BlockSpec((1, tm, yb.shape[2]), lambda b, t: (b, t + t_off + yb_off, 0)),
                  pl.BlockSpec((1, tm, gates.shape[2]), row),
                  pl.BlockSpec((1, d_a), const),
                  pl.BlockSpec((1, d_a), const),
                  pl.BlockSpec((1, d_a), const),
                  pl.BlockSpec(glb.shape, const),
                  pl.BlockSpec(pa.shape, const),
                  pl.BlockSpec(pb.shape, const),
                  pl.BlockSpec(wo.shape, const),
                  pl.BlockSpec((LANE, LANE), const)],
        out_specs=pl.BlockSpec((1, tm, d), lambda b, t: (b, t, 0)),
        compiler_params=pltpu.CompilerParams(dimension_semantics=("parallel", "parallel"),
                                             vmem_limit_bytes=VMEM_LIMIT),
        name="branch_merge",
    )(xs, modsel, y_f, y_r, rkv, lora, yb, gates, r_k, ln_g, ln_b, glb, pa, pb, wo, gs)


def _route(sel, scores, n_experts):
    per = n_experts // N_GROUPS
    in_top = []
    for e in range(n_experts):
        g0 = (e // per) * per
        rank = 0.0
        for o in range(g0, g0 + per):
            if o == e:
                continue
            ahead = (sel[o] >= sel[e]) if o < e else (sel[o] > sel[e])
            rank = rank + jnp.where(ahead, 1.0, 0.0)
        in_top.append(rank < TOP_K)
    grp = [sum(jnp.where(in_top[e], sel[e], 0.0) for e in range(g * per, (g + 1) * per))
           for g in range(N_GROUPS)]
    best = jnp.zeros_like(grp[0], dtype=jnp.int32)
    best_s = grp[0]
    for g in range(1, N_GROUPS):
        better = grp[g] > best_s
        best = jnp.where(better, g, best)
        best_s = jnp.where(better, grp[g], best_s)
    chosen = [in_top[e] & (best == e // per) for e in range(n_experts)]
    den = sum(jnp.where(chosen[e], scores[e], 0.0) for e in range(n_experts))
    return [jnp.where(chosen[e], scores[e] / den, 0.0) for e in range(n_experts)], best


MOE_EXPERTS_PER_STEP = 2
MOE_BLK = 128
POS_COL = 16


def _moe_route_kernel(x_ref, mod_ref, g_ref, rw_ref, rb_ref, h_ref, tab_ref, pos_ref, cnt_ref,
                      *, tm, n_ctx, n_experts):
    t = pl.program_id(1)
    is_ctx = _ctx_rows(t, tm, n_ctx)
    h = _rms_mod(x_ref[0], g_ref[...], _mod_row(mod_ref, 3, is_ctx), _mod_row(mod_ref, 4, is_ctx))
    h_ref[0] = h.astype(BF16)
    hh, hl = _split(h)
    wh, wl = _split(rw_ref[...])
    nt = lambda a, b: lax.dot_general(a, b, (((1,), (1,)), ((), ())), preferred_element_type=F32)
    logits = nt(wh, hh) + nt(wh, hl) + nt(wl, hh)
    scores = _sigmoid(logits)
    sel = scores + rb_ref[...]
    gates, best = _route([sel[i:i + 1, :] for i in range(n_experts)],
                         [scores[i:i + 1, :] for i in range(n_experts)], n_experts)

    in_grp = [jnp.where(best == g, 1.0, 0.0) for g in range(N_GROUPS)]
    grp8 = jnp.concatenate(in_grp + [jnp.zeros((8 - N_GROUPS, tm), F32)], axis=0).astype(BF16)
    row = lax.broadcasted_iota(jnp.int32, (tm, tm), 0)
    col = lax.broadcasted_iota(jnp.int32, (tm, tm), 1)
    upper = jnp.where(row <= col, 1.0, 0.0).astype(BF16)
    run = jnp.dot(grp8, upper, preferred_element_type=F32)
    pos = jnp.zeros((1, tm), F32)
    seg = jnp.zeros((1, 1), F32)
    cnts = []
    for g in range(N_GROUPS):
        cnt = run[g:g + 1, tm - 1:tm]
        cnts.append(cnt)
        pos = pos + in_grp[g] * (seg + run[g:g + 1, :] - 1.0)
        seg = seg + jnp.floor((cnt + (MOE_BLK - 1)) * (1.0 / MOE_BLK)) * MOE_BLK
    pos_ref[0, 0] = jnp.broadcast_to(pos, (8, tm))
    cnt_ref[0, 0] = jnp.concatenate([jnp.broadcast_to(c, (1, LANE)) for c in cnts]
                                    + [jnp.zeros((8 - N_GROUPS, LANE), F32)], axis=0)
    tab = jnp.concatenate(gates + [pos, jnp.zeros((LANE - n_experts - 1, tm), F32)], axis=0)
    tab_ref[0] = tab.T


def _moe_expert_kernel(lo_ref, hi_ref, x_ref, mod_ref, h_ref, tab_ref, pos_ref, w1_ref, w3_ref, w2_ref, o_ref,
                       hs_ref, gs_ref, acc_ref, *, tm, ts, n_ctx, n_experts):
    b = pl.program_id(0)
    t = pl.program_id(1)
    e = pl.program_id(2)
    n_t = pl.num_programs(1)

    @pl.when(e == 0)
    def _():
        pos = pos_ref[0, 0, 0:1, :].astype(jnp.int32)
        sel = jnp.where(lax.broadcasted_iota(jnp.int32, (ts, tm), 0) == pos, 1.0, 0.0).astype(BF16)
        hs_ref[...] = jnp.dot(sel, h_ref[0], preferred_element_type=F32).astype(BF16)
        tab = tab_ref[0]
        t1 = tab.astype(BF16)
        r1 = tab - t1.astype(F32)
        t2 = r1.astype(BF16)
        t3 = (r1 - t2.astype(F32)).astype(BF16)
        gs_ref[...] = (jnp.dot(sel, t1, preferred_element_type=F32) + jnp.dot(sel, t2, preferred_element_type=F32)
                       + jnp.dot(sel, t3, preferred_element_type=F32))
        acc_ref[...] = jnp.zeros_like(acc_ref)

    per_step = w1_ref.shape[0]
    e_first = e * per_step
    seg = (b * n_t + t) * N_GROUPS + e_first // (n_experts // N_GROUPS)
    lane = lax.broadcasted_iota(jnp.int32, (MOE_BLK, LANE), 1)

    def block(j, carry):
        rows = pl.ds(pl.multiple_of(j * MOE_BLK, MOE_BLK), MOE_BLK)
        hb = hs_ref[rows, :]
        gs = gs_ref[rows, :]
        hids = []
        for i in range(per_step):
            a = jnp.dot(hb, w1_ref[i], preferred_element_type=F32)
            bb = jnp.dot(hb, w3_ref[i], preferred_element_type=F32)
            gcol = jnp.sum(jnp.where(lane == e_first + i, gs, 0.0), axis=-1, keepdims=True)
            hids.append((a * _sigmoid(a) * bb * gcol).astype(BF16))
        w2 = w2_ref[...].reshape(per_step * w2_ref.shape[1], w2_ref.shape[2])
        acc_ref[rows, :] += jnp.dot(jnp.concatenate(hids, axis=1), w2, preferred_element_type=F32)
        return carry

    lax.fori_loop(lo_ref[seg], hi_ref[seg], block, 0)

    @pl.when(e == pl.num_programs(2) - 1)
    def _():
        is_ctx = _ctx_rows(t, tm, n_ctx)
        pos = tab_ref[0][:, POS_COL:POS_COL + 1].astype(jnp.int32)
        back = jnp.where(lax.broadcasted_iota(jnp.int32, (tm, ts), 1) == pos, 1.0, 0.0).astype(BF16)
        y = jnp.dot(back, acc_ref[...].astype(BF16), preferred_element_type=F32)
        o_ref[0] = x_ref[0] + _mod_row(mod_ref, 5, is_ctx) * y


def _moe_call(xs, modsel, norm_g, router_wt, router_b, w1, w3, w2, *, n_ctx, tm):
    bsz, tt, d = xs.shape
    n_experts = router_wt.shape[0]
    n_t = tt // tm
    ts = tm + N_GROUPS * MOE_BLK
    row2 = lambda b, t: (b, t, 0)
    const2 = lambda b, t: (0, 0)
    h2, tab, pos, cnt = pl.pallas_call(
        functools.partial(_moe_route_kernel, tm=tm, n_ctx=n_ctx, n_experts=n_experts),
        out_shape=(jax.ShapeDtypeStruct((bsz, tt, d), BF16),
                   jax.ShapeDtypeStruct((bsz, tt, LANE), F32),
                   jax.ShapeDtypeStruct((bsz, n_t, 8, tm), F32),
                   jax.ShapeDtypeStruct((bsz, n_t, 8, LANE), F32)),
        grid=(bsz, n_t),
        in_specs=[pl.BlockSpec((1, tm, d), row2),
                  pl.BlockSpec((1, 2, 8, d), lambda b, t: (b, 0, 0, 0)),
                  pl.BlockSpec((1, d), const2),
                  pl.BlockSpec(router_wt.shape, const2),
                  pl.BlockSpec((n_experts, 1), const2)],
        out_specs=(pl.BlockSpec((1, tm, d), row2),
                   pl.BlockSpec((1, tm, LANE), row2),
                   pl.BlockSpec((1, 1, 8, tm), lambda b, t: (b, t, 0, 0)),
                   pl.BlockSpec((1, 1, 8, LANE), lambda b, t: (b, t, 0, 0))),
        compiler_params=pltpu.CompilerParams(dimension_semantics=("parallel", "parallel"),
                                             vmem_limit_bytes=VMEM_LIMIT),
        name="moe_route",
    )(xs, modsel, norm_g, router_wt, router_b)

    n_blk = (cnt[:, :, :N_GROUPS, 0].astype(jnp.int32) + (MOE_BLK - 1)) // MOE_BLK
    hi = jnp.cumsum(n_blk, axis=-1)
    lo = hi - n_blk

    row = lambda b, t, e, lo_r, hi_r: (b, t, 0)
    wexp = lambda b, t, e, lo_r, hi_r: (e, 0, 0)
    tile4 = lambda b, t, e, lo_r, hi_r: (b, t, 0, 0)
    return pl.pallas_call(
        functools.partial(_moe_expert_kernel, tm=tm, ts=ts, n_ctx=n_ctx, n_experts=n_experts),
        out_shape=jax.ShapeDtypeStruct((bsz, tt, d), F32),
        grid_spec=pltpu.PrefetchScalarGridSpec(
            num_scalar_prefetch=2,
            grid=(bsz, n_t, n_experts // MOE_EXPERTS_PER_STEP),
            in_specs=[pl.BlockSpec((1, tm, d), row),
                      pl.BlockSpec((1, 2, 8, d), lambda b, t, e, lo_r, hi_r: (b, 0, 0, 0)),
                      pl.BlockSpec((1, tm, d), row),
                      pl.BlockSpec((1, tm, LANE), row),
                      pl.BlockSpec((1, 1, 8, tm), tile4),
                      pl.BlockSpec((MOE_EXPERTS_PER_STEP,) + w1.shape[1:], wexp),
                      pl.BlockSpec((MOE_EXPERTS_PER_STEP,) + w3.shape[1:], wexp),
                      pl.BlockSpec((MOE_EXPERTS_PER_STEP,) + w2.shape[1:], wexp)],
            out_specs=pl.BlockSpec((1, tm, d), row),
            scratch_shapes=[pltpu.VMEM((ts, d), BF16),
                            pltpu.VMEM((ts, LANE), F32),
                            pltpu.VMEM((ts, d), F32)]),
        compiler_params=pltpu.CompilerParams(
            dimension_semantics=("parallel", "parallel", "arbitrary"),
            vmem_limit_bytes=VMEM_LIMIT),
        name="moe_experts",
    )(lo.reshape(-1), hi.reshape(-1), xs, modsel, h2, tab, pos, w1, w3, w2)


def kernel(x, c, ctx, c_ctx, mod_w, mod_b, norm1_g, norm2_g, w_in, rw_w0, rw_w_lora_b, rw_a0, rw_a_lora_b,
           rw_g_lora_b, rw_k_k, rw_k_a, rw_r_k, rw_ln_g, rw_ln_b, na_q_g, na_k_g, na_rpb, proj_a, proj_b,
           w_out, router_w, router_bias, moe_w1, moe_w3, moe_w2):
    bsz, seq, d = x.shape
    n_ctx = ctx.shape[1]
    depth = mod_w.shape[0]
    d_a = rw_w0.shape[2]
    d_b = proj_b.shape[1]
    lora_w = rw_w_lora_b.shape[2]
    lora_a = rw_a_lora_b.shape[2]
    lora_g = rw_g_lora_b.shape[1]
    n_lora = 2 * lora_w + 2 * lora_a + lora_g
    heads_a = d_a // HEAD_DIM
    heads_b = d_b // HEAD_DIM
    assert seq % (GRID_W * WIN_H) == 0 and n_ctx % 256 == 0 and bsz + 1 <= 16

    cs = jnp.zeros((16, d), F32).at[:bsz].set(c).at[bsz].set(c_ctx)
    mod = _mod_call(cs, mod_w, mod_b).reshape(depth, 16, 6, d)

    lane = jnp.arange(LANE)
    gs = (lane[:, None] // HEAD_DIM == lane[None, :] // HEAD_DIM).astype(BF16)
    router_wt = router_w.T
    router_b = router_bias.reshape(-1, 1)

    xs = jnp.concatenate([ctx, x], axis=1)
    for l in range(depth):
        last = l == depth - 1
        m_c = jnp.broadcast_to(mod[l, bsz][None], (bsz, 6, d))
        modsel = jnp.pad(jnp.stack([m_c, mod[l, :bsz]], axis=1), ((0, 0), (0, 0), (0, 2), (0, 0)))

        rkv, kk, lora, qkv, gates = _in_proj_call(
            xs, modsel, norm1_g[l][None], w_in[l].astype(BF16), rw_k_k[l][None],
            jnp.tile(na_q_g[l], heads_b)[None], jnp.tile(na_k_g[l], heads_b)[None], gs,
            n_ctx=n_ctx, d_a=d_a, d_b=d_b, n_lora=n_lora)
        y_f, y_r = _rwkv_call(rkv, kk, lora, rw_w0[l], rw_w_lora_b[l], rw_a0[l], rw_a_lora_b[l],
                              rw_k_a[l][None], n_ctx=n_ctx, lora_w=lora_w, lora_a=lora_a)
        yb = _attn_call(qkv, _bias_table(na_rpb[l]), n_ctx=n_ctx, with_ctx=not last)
        xs = _merge_call(xs, modsel, y_f, y_r, rkv, lora, yb, gates, rw_r_k[l].reshape(1, d_a),
                         rw_ln_g[l][None], rw_ln_b[l][None], rw_g_lora_b[l].astype(BF16),
                         proj_a[l].astype(BF16), proj_b[l].astype(BF16), w_out[l].astype(BF16), gs,
                         n_ctx=n_ctx, with_ctx=not last, lora_g=lora_g)
        moe_ctx = 0 if last else n_ctx
        tm = next(m for m in (1024, 768, 512, 256) if xs.shape[1] % m == 0)
        xs = _moe_call(xs, modsel, norm2_g[l][None], router_wt, router_b,
                       moe_w1[l].astype(BF16), moe_w3[l].astype(BF16), moe_w2[l].astype(BF16),
                       n_ctx=moe_ctx, tm=tm)
    return xs
```

```python
import functools
import math

import jax
import jax.numpy as jnp
import numpy as np
from jax import lax
from jax.experimental import pallas as pl
from jax.experimental.pallas import tpu as pltpu

F32 = jnp.float32
BF16 = jnp.bfloat16

HEAD_DIM = 64
GRID_W = 64
WIN_H = 8
WIN_W = 16
N_GROUPS = 4
TOP_K = 2
RMS_EPS = 1e-6
GN_EPS = 64e-5
LANE = 128
CHUNK = 64
SUB = 16
NEG_BIG = -1e30
DECAY_SCALE = math.exp(-0.5)
VMEM_LIMIT = 56 * 1024 * 1024


def _bdot(a, b):
    return jnp.dot(a.astype(BF16), b.astype(BF16), preferred_element_type=F32)


def _bdot_nt(a, b):
    return lax.dot_general(a.astype(BF16), b.astype(BF16), (((1,), (1,)), ((), ())),
                           preferred_element_type=F32)


def _bdot_tn(a, b):
    return lax.dot_general(a.astype(BF16), b.astype(BF16), (((0,), (0,)), ((), ())),
                           preferred_element_type=F32)


def _split(x):
    hi = x.astype(BF16)
    lo = (x - hi.astype(F32)).astype(BF16)
    return hi, lo


def _dot3(a, b):
    ah, al = _split(a)
    bh, bl = _split(b)
    return (jnp.dot(ah, bh, preferred_element_type=F32)
            + jnp.dot(al, bh, preferred_element_type=F32)
            + jnp.dot(ah, bl, preferred_element_type=F32))


def _sigmoid(x):
    return 1.0 / (1.0 + jnp.exp(-x))


def _group_sum(x, g128):
    parts = [_bdot(x[:, j * LANE:(j + 1) * LANE], g128) for j in range(x.shape[1] // LANE)]
    return jnp.concatenate(parts, axis=1)


def _group_sum2(x, g128):
    hi, lo = _split(x)
    parts = []
    for j in range(x.shape[1] // LANE):
        sl = slice(j * LANE, (j + 1) * LANE)
        parts.append(jnp.dot(hi[:, sl], g128, preferred_element_type=F32)
                     + jnp.dot(lo[:, sl], g128, preferred_element_type=F32))
    return jnp.concatenate(parts, axis=1)


def _mod_row(mod_ref, idx, is_ctx):
    mx = mod_ref[0, 1, idx:idx + 1, :]
    if is_ctx is None:
        return mx
    return jnp.where(is_ctx, mod_ref[0, 0, idx:idx + 1, :], mx)


def _ctx_rows(tile, tm, n_ctx):
    if n_ctx == 0:
        return None
    rows = tile * tm + lax.broadcasted_iota(jnp.int32, (tm, 1), 0)
    return rows < n_ctx


def _rms_mod(x, gain, shift, scale):
    xn = x * lax.rsqrt(jnp.mean(x * x, axis=-1, keepdims=True) + RMS_EPS) * gain
    return xn * (1.0 + scale) + shift


def _mod_kernel(c_ref, w_ref, b_ref, o_ref):
    c = c_ref[...]
    o_ref[0] = _dot3(c * _sigmoid(c), w_ref[0]) + b_ref[0]


def _mod_call(cs, mod_w, mod_b):
    depth, d, n = mod_w.shape
    tn = n // 4
    return pl.pallas_call(
        _mod_kernel,
        out_shape=jax.ShapeDtypeStruct((depth, cs.shape[0], n), F32),
        grid=(depth, n // tn),
        in_specs=[pl.BlockSpec(cs.shape, lambda l, j: (0, 0)),
                  pl.BlockSpec((1, d, tn), lambda l, j: (l, 0, j)),
                  pl.BlockSpec((1, 1, tn), lambda l, j: (l, 0, j))],
        out_specs=pl.BlockSpec((1, cs.shape[0], tn), lambda l, j: (l, 0, j)),
        compiler_params=pltpu.CompilerParams(dimension_semantics=("parallel", "parallel"),
                                             vmem_limit_bytes=VMEM_LIMIT),
        name="adaln_vectors",
    )(cs, mod_w, mod_b.reshape(depth, 1, n))


def _in_proj_kernel(x_ref, mod_ref, g_ref, w_ref, kkg_ref, qg_ref, kg_ref, gs_ref,
                    rkv_ref, kk_ref, lora_ref, qkv_ref, gates_ref, *, tm, n_ctx, d_a, d_b, n_lora):
    t = pl.program_id(1)
    is_ctx = _ctx_rows(t, tm, n_ctx)
    h = _rms_mod(x_ref[0], g_ref[...], _mod_row(mod_ref, 0, is_ctx), _mod_row(mod_ref, 1, is_ctx))
    h = h.astype(BF16)
    gs = gs_ref[...]

    o = 0
    rkv = jnp.dot(h, w_ref[:, o:o + 3 * d_a], preferred_element_type=F32)
    rkv_ref[0] = rkv
    kkv = rkv[:, d_a:2 * d_a] * kkg_ref[...]
    ss = _group_sum2(kkv * kkv, gs)
    kk_ref[0] = kkv * lax.rsqrt(jnp.maximum(ss, 1e-12))
    o += 3 * d_a

    lora_ref[0] = jnp.dot(h, w_ref[:, o:o + n_lora], preferred_element_type=F32)
    o += n_lora

    qkv = jnp.dot(h, w_ref[:, o:o + 3 * d_b], preferred_element_type=F32)
    q, k = qkv[:, :d_b], qkv[:, d_b:2 * d_b]
    inv_n = 1.0 / HEAD_DIM
    qn = q * lax.rsqrt(_group_sum(q * q, gs) * inv_n + RMS_EPS) * qg_ref[...] * (HEAD_DIM ** -0.5)
    kn = k * lax.rsqrt(_group_sum(k * k, gs) * inv_n + RMS_EPS) * kg_ref[...]
    qkv_ref[0] = jnp.concatenate([qn, kn, qkv[:, 2 * d_b:]], axis=1).astype(BF16)
    o += 3 * d_b

    gates_ref[0] = _sigmoid(jnp.dot(h, w_ref[:, o:], preferred_element_type=F32)).astype(BF16)


def _in_proj_call(xs, modsel, norm_g, w_in, k_k, q_g, k_g, gs, *, n_ctx, d_a, d_b, n_lora, tm=256):
    bsz, tt, d = xs.shape
    n_gate = w_in.shape[1] - 3 * d_a - n_lora - 3 * d_b
    row = lambda b, t: (b, t, 0)
    const = lambda b, t: (0, 0)
    return pl.pallas_call(
        functools.partial(_in_proj_kernel, tm=tm, n_ctx=n_ctx, d_a=d_a, d_b=d_b, n_lora=n_lora),
        out_shape=(jax.ShapeDtypeStruct((bsz, tt, 3 * d_a), F32),
                   jax.ShapeDtypeStruct((bsz, tt, d_a), F32),
                   jax.ShapeDtypeStruct((bsz, tt, n_lora), F32),
                   jax.ShapeDtypeStruct((bsz, tt, 3 * d_b), BF16),
                   jax.ShapeDtypeStruct((bsz, tt, n_gate), BF16)),
        grid=(bsz, tt // tm),
        in_specs=[pl.BlockSpec((1, tm, d), row),
                  pl.BlockSpec((1, 2, 8, d), lambda b, t: (b, 0, 0, 0)),
                  pl.BlockSpec((1, d), const),
                  pl.BlockSpec(w_in.shape, const),
                  pl.BlockSpec((1, d_a), const),
                  pl.BlockSpec((1, d_b), const),
                  pl.BlockSpec((1, d_b), const),
                  pl.BlockSpec((LANE, LANE), const)],
        out_specs=(pl.BlockSpec((1, tm, 3 * d_a), row),
                   pl.BlockSpec((1, tm, d_a), row),
                   pl.BlockSpec((1, tm, n_lora), row),
                   pl.BlockSpec((1, tm, 3 * d_b), row),
                   pl.BlockSpec((1, tm, n_gate), row)),
        compiler_params=pltpu.CompilerParams(dimension_semantics=("parallel", "parallel"),
                                             vmem_limit_bytes=VMEM_LIMIT),
        name="in_proj",
    )(xs, modsel, norm_g, w_in, k_k, q_g, k_g, gs)


N_OPERANDS = 6


def _dot3_presplit(x, w_hi, w_lo):
    xh, xl = _split(x)
    return (jnp.dot(xh, w_hi, preferred_element_type=F32) + jnp.dot(xl, w_hi, preferred_element_type=F32)
            + jnp.dot(xh, w_lo, preferred_element_type=F32))


def _rwkv_prep_kernel(rkv_ref, kk_ref, lora_ref, w0_ref, wlbh_ref, wlbl_ref, a0_ref, albh_ref, albl_ref, ka_ref,
                      opf_ref, opr_ref, v_ref, pef_ref, per_ref, *, tm, d_a, lora_w, lora_a):
    rkv = rkv_ref[0]
    r, k, v = rkv[:, :d_a], rkv[:, d_a:2 * d_a], rkv[:, 2 * d_a:]
    kk = kk_ref[0]
    lora = lora_ref[0]
    v_ref[0] = v.astype(BF16)
    chunk_bits = int(math.log2(CHUNK))
    row = lax.broadcasted_iota(jnp.int32, (tm, tm), 0)
    col = lax.broadcasted_iota(jnp.int32, (tm, tm), 1)
    same = (row >> chunk_bits) == (col >> chunk_bits)
    ones_blk = jnp.where(same, 1.0, 0.0).astype(BF16)
    for d, (op_ref, pe_ref) in enumerate(((opf_ref, pef_ref), (opr_ref, per_ref))):
        lw = lora[:, d * lora_w:(d + 1) * lora_w]
        la = lora[:, 2 * lora_w + d * lora_a:2 * lora_w + (d + 1) * lora_a]
        wl = w0_ref[d] + _dot3_presplit(jnp.tanh(lw), wlbh_ref[d], wlbl_ref[d])
        ld = -DECAY_SCALE * _sigmoid(wl)
        al = _sigmoid(a0_ref[d] + _dot3_presplit(la, albh_ref[d], albl_ref[d]))
        kd = k * (1.0 + (al - 1.0) * ka_ref[...])
        before = (row <= col) if d else (row >= col)
        tri = jnp.where(same & before, 1.0, 0.0).astype(BF16)
        ld_hi, ld_lo = _split(ld)
        cum = jnp.dot(tri, ld_hi, preferred_element_type=F32) + jnp.dot(tri, ld_lo, preferred_element_type=F32)
        tot = (jnp.dot(ones_blk, ld_hi, preferred_element_type=F32)
               + jnp.dot(ones_blk, ld_lo, preferred_element_type=F32))
        p_end = jnp.exp(tot)
        p_inv = jnp.exp(-cum)
        bt = kk * al * p_inv
        kt = kd * p_inv
        op_ref[0] = jnp.concatenate([-kk * jnp.exp(cum - ld), r * jnp.exp(cum), bt, kt, bt * p_end, kt * p_end],
                                    axis=1).astype(BF16)
        for ci in range(tm // CHUNK):
            pe_ref[0, ci] = p_end[ci * CHUNK:ci * CHUNK + 8]


def _rwkv_prep_call(rkv, kk, lora, w0, wlb, a0, alb, k_a, *, lora_w, lora_a, tm=256):
    bsz, tt, d3 = rkv.shape
    d_a = d3 // 3
    row = lambda b, t: (b, t, 0)
    const3 = lambda b, t: (0, 0, 0)
    wlb_hi, wlb_lo = _split(wlb)
    alb_hi, alb_lo = _split(alb)
    op_shape = jax.ShapeDtypeStruct((bsz, tt, N_OPERANDS * d_a), BF16)
    pe_shape = jax.ShapeDtypeStruct((bsz, tt // CHUNK, 8, d_a), F32)
    pe_spec = pl.BlockSpec((1, tm // CHUNK, 8, d_a), lambda b, t: (b, t, 0, 0))
    return pl.pallas_call(
        functools.partial(_rwkv_prep_kernel, tm=tm, d_a=d_a, lora_w=lora_w, lora_a=lora_a),
        out_shape=(op_shape, op_shape, jax.ShapeDtypeStruct((bsz, tt, d_a), BF16), pe_shape, pe_shape),
        grid=(bsz, tt // tm),
        in_specs=[pl.BlockSpec((1, tm, d3), row),
                  pl.BlockSpec((1, tm, d_a), row),
                  pl.BlockSpec((1, tm, lora.shape[2]), row),
                  pl.BlockSpec((2, 1, d_a), const3),
                  pl.BlockSpec((2, lora_w, d_a), const3),
                  pl.BlockSpec((2, lora_w, d_a), const3),
                  pl.BlockSpec((2, 1, d_a), const3),
                  pl.BlockSpec((2, lora_a, d_a), const3),
                  pl.BlockSpec((2, lora_a, d_a), const3),
                  pl.BlockSpec((1, d_a), lambda b, t: (0, 0))],
        out_specs=(pl.BlockSpec((1, tm, N_OPERANDS * d_a), row),
                   pl.BlockSpec((1, tm, N_OPERANDS * d_a), row),
                   pl.BlockSpec((1, tm, d_a), row), pe_spec, pe_spec),
        compiler_params=pltpu.CompilerParams(dimension_semantics=("parallel", "parallel"),
                                             vmem_limit_bytes=VMEM_LIMIT),
        name="rwkv7_operands",
    )(rkv, kk, lora, w0.reshape(2, 1, d_a), wlb_hi, wlb_lo, a0.reshape(2, 1, d_a), alb_hi, alb_lo, k_a)


def _rwkv_pair_kernel(opf_ref, opr_ref, vf_ref, vr_ref, pef_ref, per_ref, yf_ref, yr_ref, st_ref, *, d_a):
    n = HEAD_DIM
    pw = 2 * n
    pairs = d_a // pw
    c = CHUNK
    nb = opf_ref.shape[0]

    @pl.when(pl.program_id(1) == 0)
    def _():
        st_ref[...] = jnp.zeros_like(st_ref)

    chains = [(b, d, p) for b in range(nb) for d in range(2) for p in range(pairs)]
    op_refs = (opf_ref, opr_ref)
    part = lambda i: [op_refs[d][b, :, i * d_a + p * pw:i * d_a + (p + 1) * pw] for b, d, p in chains]
    v_refs = (vf_ref, vr_ref)
    pe_refs = (pef_ref, per_ref)
    v = [v_refs[d][b, :, p * pw:(p + 1) * pw] for b, d, p in chains]
    p_end = [pe_refs[d][b, 0, 0:1, p * pw:(p + 1) * pw] for b, d, p in chains]
    each = lambda f, *ls: [f(*xs) for xs in zip(*ls)]
    rows2 = lambda a, b: jnp.concatenate([a, b], axis=0)
    bf = lambda t: t.astype(BF16)

    def bd(x):
        x = bf(x)
        left = lax.broadcasted_iota(jnp.int32, x.shape, 1) < n
        zero = jnp.zeros_like(x)
        return rows2(jnp.where(left, x, zero), jnp.where(left, zero, x))

    pdot = lambda a, b: jnp.dot(bf(a), bd(b), preferred_element_type=F32)
    ar = each(rows2, part(0), part(1))
    b_k = each(lambda b, k: rows2(bd(b), bd(k)), part(2), part(3))
    bk_end = each(rows2, part(4), part(5))
    s = [st_ref[b, d, p] for b, d, p in chains]

    mm_all = each(lambda a, w, ss: _bdot_nt(a, rows2(w, bd(ss))), ar, b_k, s)
    row = lax.broadcasted_iota(jnp.int32, (c, 2 * pw), 0)
    col = lax.broadcasted_iota(jnp.int32, (c, 2 * pw), 1) & (c - 1)
    incl = [(row <= col) if d else (row >= col) for _, d, _ in chains]
    strict = [(row < col) if d else (row > col) for _, d, _ in chains]
    m_a = each(lambda mm, msk: jnp.where(msk, mm[:c, :2 * pw], 0.0), mm_all, strict)
    m_r = each(lambda mm, msk: jnp.where(msk, mm[c:, :2 * pw], 0.0), mm_all, incl)
    ms_a = each(lambda mm: mm[:c, 2 * pw:], mm_all)
    ms_r = each(lambda mm: mm[c:, 2 * pw:], mm_all)
    a_ab = each(lambda mm: mm[:, :pw], m_a)
    a_ak = each(lambda mm: mm[:, pw:], m_a)

    row = lax.broadcasted_iota(jnp.int32, (c, pw), 0)
    col = lax.broadcasted_iota(jnp.int32, (c, pw), 1) & (c - 1)
    eye = jnp.where(row == col, 1.0, 0.0)
    sub_bits = int(math.log2(SUB))
    blk = (row >> sub_bits) == (col >> sub_bits)
    ad = each(lambda a: jnp.where(blk, a, 0.0), a_ab)
    ao = each(lambda a, b: a - b, a_ab, ad)
    td = each(lambda a: eye + a, ad)
    x = each(pdot, ad, ad)
    for _ in range(sub_bits - 2):
        res = each(lambda xx, tt: pdot(rows2(xx, tt), xx), x, td)
        td = each(lambda tt, rr: tt + rr[c:], td, res)
        x = each(lambda rr: rr[:c], res)
    td = each(lambda tt, xx: tt + pdot(tt, xx), td, x)
    x = each(pdot, td, ao)
    w = td
    for _ in range(int(math.log2(c // SUB)) - 1):
        res = each(lambda xx, ww: jnp.dot(bf(xx), jnp.concatenate([bd(xx), bd(ww)], axis=1),
                                          preferred_element_type=F32), x, w)
        w = each(lambda ww, rr: ww + rr[:, pw:], w, res)
        x = each(lambda rr: rr[:, :pw], res)
    t_inv = each(lambda ww, xx: ww + pdot(xx, ww), w, x)

    akv = each(pdot, a_ak, v)
    u = each(lambda t, m0, m1: bf(pdot(t, m0 + m1)), t_inv, ms_a, akv)
    y = each(lambda m0, a, uu, vv: m0 + jnp.dot(bf(a), rows2(bd(uu), bd(vv)), preferred_element_type=F32),
             ms_r, m_r, u, v)
    left = lax.broadcasted_iota(jnp.int32, (n, pw), 1) < n
    upd = each(lambda uu, vv, w_end: _bdot_tn(rows2(uu, vv), w_end), u, v, bk_end)
    s_new = each(lambda ss, p, dd: ss * p + jnp.where(left, dd[:n], dd[n:]), s, p_end, upd)
    for (b, d, p), val in zip(chains, s_new):
        st_ref[b, d, p] = val
    for b in range(nb):
        yf_ref[b] = jnp.concatenate(y[2 * b * pairs:(2 * b + 1) * pairs], axis=1)
        yr_ref[b] = jnp.concatenate(y[(2 * b + 1) * pairs:(2 * b + 2) * pairs], axis=1)


def _rwkv_call(rkv, kk, lora, w0, wlb, a0, alb, k_a, *, n_ctx, lora_w, lora_a):
    bsz, tt, d3 = rkv.shape
    d_a = d3 // 3
    n_chunks = tt // CHUNK
    nc_ctx = n_ctx // CHUNK
    heads = d_a // HEAD_DIM
    op_f, op_r, v, pe_f, pe_r = _rwkv_prep_call(rkv, kk, lora, w0, wlb, a0, alb, k_a,
                                                lora_w=lora_w, lora_a=lora_a)

    rev_chunk = lambda j: jnp.where(j < nc_ctx, nc_ctx - 1 - j, n_chunks - 1 - (j - nc_ctx))
    fwd = lambda b, j: (b, j, 0)
    rev = lambda b, j: (b, rev_chunk(j), 0)
    y_shape = jax.ShapeDtypeStruct((bsz, tt, d_a), F32)
    nb = next(m for m in (2, 1) if bsz % m == 0)
    return pl.pallas_call(
        functools.partial(_rwkv_pair_kernel, d_a=d_a),
        out_shape=(y_shape, y_shape),
        grid=(bsz // nb, n_chunks),
        in_specs=[pl.BlockSpec((nb, CHUNK, N_OPERANDS * d_a), fwd),
                  pl.BlockSpec((nb, CHUNK, N_OPERANDS * d_a), rev),
                  pl.BlockSpec((nb, CHUNK, d_a), fwd),
                  pl.BlockSpec((nb, CHUNK, d_a), rev),
                  pl.BlockSpec((nb, 1, 8, d_a), lambda b, j: (b, j, 0, 0)),
                  pl.BlockSpec((nb, 1, 8, d_a), lambda b, j: (b, rev_chunk(j), 0, 0))],
        out_specs=(pl.BlockSpec((nb, CHUNK, d_a), fwd), pl.BlockSpec((nb, CHUNK, d_a), rev)),
        scratch_shapes=[pltpu.VMEM((nb, 2, heads // 2, HEAD_DIM, 2 * HEAD_DIM), F32)],
        compiler_params=pltpu.CompilerParams(dimension_semantics=("parallel", "arbitrary"),
                                             vmem_limit_bytes=VMEM_LIMIT),
        name="rwkv7_chunk_scan",
    )(op_f, op_r, v, v, pe_f, pe_r)


def _softmax_pv(heads_parts):
    ms = [functools.reduce(jnp.maximum, [jnp.max(s, axis=-1, keepdims=True) for s, _ in parts])
          for parts in heads_parts]
    ps = [[jnp.exp(s - m) for s, _ in parts] for parts, m in zip(heads_parts, ms)]
    dens = [sum(jnp.sum(p, axis=-1, keepdims=True) for p in pp) for pp in ps]
    nums = [sum(jnp.dot(p.astype(BF16), vals, preferred_element_type=F32) for p, (_, vals) in zip(pp, parts))
            for pp, parts in zip(ps, heads_parts)]
    return [num / den for num, den in zip(nums, dens)]


def _attn_kernel(q_ref, k_ref, v_ref, bias_ref, o_ref, *, n_ctx, rows, q_off, d_b):
    n = HEAD_DIM
    heads = d_b // n
    nq_ctx = n_ctx // GRID_W
    i = pl.program_id(1) + q_off
    nb = q_ref.shape[0]
    q = [q_ref[b] for b in range(nb)]
    kc = [k_ref[b, 0:n_ctx, :] for b in range(nb)]
    vc = [v_ref[b, 0:n_ctx, :] for b in range(nb)]
    units = [(b, h, slice(h * n, (h + 1) * n)) for b in range(nb) for h in range(heads)]

    def store(outs):
        for b in range(nb):
            o_ref[b] = jnp.concatenate(outs[b * heads:(b + 1) * heads], axis=1).astype(o_ref.dtype)

    def latent():
        li = i - nq_ctx
        r0 = jnp.clip(li - WIN_H // 2, 0, rows - WIN_H)
        start = pl.multiple_of(n_ctx + r0 * GRID_W, GRID_W)
        kl = [k_ref[b, pl.ds(start, WIN_H * GRID_W), :] for b in range(nb)]
        vl = [v_ref[b, pl.ds(start, WIN_H * GRID_W), :] for b in range(nb)]
        s_l = [_bdot_nt(q[b][:, sl], kl[b][:, sl]) + bias_ref[0, h] for b, h, sl in units]
        s_c = [_bdot_nt(q[b][:, sl], kc[b][:, sl]) for b, h, sl in units]
        store(_softmax_pv([[(sa, vl[b][:, sl]), (sb, vc[b][:, sl])]
                           for sa, sb, (b, h, sl) in zip(s_l, s_c, units)]))

    def context():
        store(_softmax_pv([[(_bdot_nt(q[b][:, sl], kc[b][:, sl]), vc[b][:, sl])] for b, h, sl in units]))

    if q_off == 0:
        pl.when(i >= nq_ctx)(latent)
        pl.when(i < nq_ctx)(context)
    else:
        latent()


def _bias_table(rpb):
    w = GRID_W
    heads = rpb.shape[0]
    j = np.arange(w)
    col_start = np.clip(j - WIN_W // 2, 0, w - WIN_W)
    in_win = (j[None, :] >= col_start[:, None]) & (j[None, :] < col_start[:, None] + WIN_W)
    rpb = rpb.astype(F32)
    ext = jnp.concatenate([jnp.repeat(rpb[..., :1], w - WIN_W, axis=-1), rpb,
                           jnp.repeat(rpb[..., -1:], w - WIN_W, axis=-1)], axis=-1)
    toe = jnp.stack([ext[..., w - 1 - qc:2 * w - 1 - qc] for qc in range(w)], axis=2)
    tz = jnp.where(in_win, toe, NEG_BIG)
    bt = jnp.stack([tz[:, v:v + WIN_H] for v in range(WIN_H)], axis=0)
    return bt.transpose(0, 1, 3, 2, 4).reshape(WIN_H, heads, w, WIN_H * w)


def _attn_call(qkv, bias_tab, *, n_ctx, with_ctx):
    bsz, tt, d3 = qkv.shape
    d_b = d3 // 3
    heads = d_b // HEAD_DIM
    rows = (tt - n_ctx) // GRID_W
    nq_ctx = n_ctx // GRID_W
    q_off = 0 if with_ctx else nq_ctx
    n_q = tt // GRID_W - q_off
    nb = next(m for m in (4, 2, 1) if bsz % m == 0)

    def variant(b, i):
        li = jnp.maximum(i + q_off - nq_ctx, 0)
        return (jnp.clip(li - WIN_H // 2, 0, rows - WIN_H) - li + WIN_H - 1, 0, 0, 0)

    return pl.pallas_call(
        functools.partial(_attn_kernel, n_ctx=n_ctx, rows=rows, q_off=q_off, d_b=d_b),
        out_shape=jax.ShapeDtypeStruct((bsz, n_q * GRID_W, d_b), BF16),
        grid=(bsz // nb, n_q),
        in_specs=[pl.BlockSpec((nb, GRID_W, d_b), lambda b, i: (b, i + q_off, 0)),
                  pl.BlockSpec((nb, tt, d_b), lambda b, i: (b, 0, 1)),
                  pl.BlockSpec((nb, tt, d_b), lambda b, i: (b, 0, 2)),
                  pl.BlockSpec((1, heads, GRID_W, WIN_H * GRID_W), variant)],
        out_specs=pl.BlockSpec((nb, GRID_W, d_b), lambda b, i: (b, i, 0)),
        compiler_params=pltpu.CompilerParams(dimension_semantics=("parallel", "arbitrary"),
                                             vmem_limit_bytes=VMEM_LIMIT),
        name="neighbourhood_attention",
    )(qkv, qkv, qkv, bias_tab)


def _merge_kernel(x_ref, mod_ref, y0_ref, y1_ref, rkv_ref, lora_ref, yb_ref, gates_ref,
                  rk_ref, lng_ref, lnb_ref, glb_ref, pa_ref, pb_ref, wo_ref, gs_ref, o_ref,
                  *, tm, n_ctx, t_off, d_a, lora_g):
    t = pl.program_id(1) + t_off
    is_ctx = _ctx_rows(t, tm, n_ctx)
    gs = gs_ref[...]
    inv_n = 1.0 / HEAD_DIM
    y = y0_ref[0] + y1_ref[0]
    mu = _group_sum2(y, gs) * inv_n
    yc = y - mu
    var = _group_sum2(yc * yc, gs) * inv_n
    yn = yc * lax.rsqrt(var + GN_EPS) * lng_ref[...] + lnb_ref[...]
    rkv = rkv_ref[0]
    r, k, v = rkv[:, :d_a], rkv[:, d_a:2 * d_a], rkv[:, 2 * d_a:]
    bonus = _group_sum2(r * k * rk_ref[...], gs) * v
    lg = lora_ref[0][:, lora_ref.shape[2] - lora_g:]
    g = _bdot(_sigmoid(lg), glb_ref[...])
    ya = (yn + bonus) * g
    gates = gates_ref[0].astype(F32)
    dm = gates.shape[1] // 2
    merged = gates[:, :dm] * _bdot(ya, pa_ref[...]) + gates[:, dm:] * _bdot(yb_ref[0], pb_ref[...])
    o_ref[0] = x_ref[0] + _mod_row(mod_ref, 2, is_ctx) * _bdot(merged, wo_ref[...])


def _merge_call(xs, modsel, y_f, y_r, rkv, lora, yb, gates, r_k, ln_g, ln_b, glb, pa, pb, wo, gs,
                *, n_ctx, with_ctx, lora_g, tm=256):
    bsz, tt, d = xs.shape
    d_a = rkv.shape[2] // 3
    t_off = 0 if with_ctx else n_ctx // tm
    n_t = tt // tm - t_off
    yb_off = 0 if with_ctx else -t_off
    row = lambda b, t: (b, t + t_off, 0)
    const = lambda b, t: (0, 0)
    return pl.pallas_call(
        functools.partial(_merge_kernel, tm=tm, n_ctx=n_ctx, t_off=t_off, d_a=d_a, lora_g=lora_g),
        out_shape=jax.ShapeDtypeStruct((bsz, n_t * tm, d), F32),
        grid=(bsz, n_t),
        in_specs=[pl.BlockSpec((1, tm, d), row),
                  pl.BlockSpec((1, 2, 8, d), lambda b, t: (b, 0, 0, 0)),
                  pl.BlockSpec((1, tm, d_a), row),
                  pl.BlockSpec((1, tm, d_a), row),
                  pl.BlockSpec((1, tm, 3 * d_a), row),
                  pl.BlockSpec((1, tm, lora.shape[2]), row),
                  pl.BlockSpec((1, tm, yb.shape[2]), lambda b, t: (b, t + t_off + yb_off, 0)),
                  pl.BlockSpec((1, tm, gates.shape[2]), row),
                  pl.BlockSpec((1, d_a), const),
                  pl.BlockSpec((1, d_a), const),
                  pl.BlockSpec((1, d_a), const),
                  pl.BlockSpec(glb.shape, const),
                  pl.BlockSpec(pa.shape, const),
                  pl.BlockSpec(pb.shape, const),
                  pl.BlockSpec(wo.shape, const),
                  pl.BlockSpec((LANE, LANE), const)],
        out_specs=pl.BlockSpec((1, tm, d), lambda b, t: (b, t, 0)),
        compiler_params=pltpu.CompilerParams(dimension_semantics=("parallel", "parallel"),
                                             vmem_limit_bytes=VMEM_LIMIT),
        name="branch_merge",
    )(xs, modsel, y_f, y_r, rkv, lora, yb, gates, r_k, ln_g, ln_b, glb, pa, pb, wo, gs)


def _route(sel, scores, n_experts):
    per = n_experts // N_GROUPS
    in_top = []
    for e in range(n_experts):
        g0 = (e // per) * per
        rank = 0.0
        for o in range(g0, g0 + per):
            if o == e:
                continue
            ahead = (sel[o] >= sel[e]) if o < e else (sel[o] > sel[e])
            rank = rank + jnp.where(ahead, 1.0, 0.0)
        in_top.append(rank < TOP_K)
    grp = [sum(jnp.where(in_top[e], sel[e], 0.0) for e in range(g * per, (g + 1) * per))
           for g in range(N_GROUPS)]
    best = jnp.zeros_like(grp[0], dtype=jnp.int32)
    best_s = grp[0]
    for g in range(1, N_GROUPS):
        better = grp[g] > best_s
        best = jnp.where(better, g, best)
        best_s = jnp.where(better, grp[g], best_s)
    chosen = [in_top[e] & (best == e // per) for e in range(n_experts)]
    den = sum(jnp.where(chosen[e], scores[e], 0.0) for e in range(n_experts))
    return [jnp.where(chosen[e], scores[e] / den, 0.0) for e in range(n_experts)], best


MOE_EXPERTS_PER_STEP = 2
MOE_BLK = 128
POS_COL = 16


def _moe_route_kernel(x_ref, mod_ref, g_ref, rw_ref, rb_ref, h_ref, tab_ref, pos_ref, cnt_ref,
                      *, tm, n_ctx, n_experts):
    t = pl.program_id(1)
    is_ctx = _ctx_rows(t, tm, n_ctx)
    h = _rms_mod(x_ref[0], g_ref[...], _mod_row(mod_ref, 3, is_ctx), _mod_row(mod_ref, 4, is_ctx))
    h_ref[0] = h.astype(BF16)
    hh, hl = _split(h)
    wh, wl = _split(rw_ref[...])
    nt = lambda a, b: lax.dot_general(a, b, (((1,), (1,)), ((), ())), preferred_element_type=F32)
    logits = nt(wh, hh) + nt(wh, hl) + nt(wl, hh)
    scores = _sigmoid(logits)
    sel = scores + rb_ref[...]
    gates, best = _route([sel[i:i + 1, :] for i in range(n_experts)],
                         [scores[i:i + 1, :] for i in range(n_experts)], n_experts)

    in_grp = [jnp.where(best == g, 1.0, 0.0) for g in range(N_GROUPS)]
    grp8 = jnp.concatenate(in_grp + [jnp.zeros((8 - N_GROUPS, tm), F32)], axis=0).astype(BF16)
    row = lax.broadcasted_iota(jnp.int32, (tm, tm), 0)
    col = lax.broadcasted_iota(jnp.int32, (tm, tm), 1)
    upper = jnp.where(row <= col, 1.0, 0.0).astype(BF16)
    run = jnp.dot(grp8, upper, preferred_element_type=F32)
    pos = jnp.zeros((1, tm), F32)
    seg = jnp.zeros((1, 1), F32)
    cnts = []
    for g in range(N_GROUPS):
        cnt = run[g:g + 1, tm - 1:tm]
        cnts.append(cnt)
        pos = pos + in_grp[g] * (seg + run[g:g + 1, :] - 1.0)
        seg = seg + jnp.floor((cnt + (MOE_BLK - 1)) * (1.0 / MOE_BLK)) * MOE_BLK
    pos_ref[0, 0] = jnp.broadcast_to(pos, (8, tm))
    cnt_ref[0, 0] = jnp.concatenate([jnp.broadcast_to(c, (1, LANE)) for c in cnts]
                                    + [jnp.zeros((8 - N_GROUPS, LANE), F32)], axis=0)
    tab = jnp.concatenate(gates + [pos, jnp.zeros((LANE - n_experts - 1, tm), F32)], axis=0)
    tab_ref[0] = tab.T


def _moe_expert_kernel(lo_ref, hi_ref, x_ref, mod_ref, h_ref, tab_ref, pos_ref, w1_ref, w3_ref, w2_ref, o_ref,
                       hs_ref, gs_ref, acc_ref, *, tm, ts, n_ctx, n_experts):
    b = pl.program_id(0)
    t = pl.program_id(1)
    e = pl.program_id(2)
    n_t = pl.num_programs(1)

    @pl.when(e == 0)
    def _():
        pos = pos_ref[0, 0, 0:1, :].astype(jnp.int32)
        sel = jnp.where(lax.broadcasted_iota(jnp.int32, (ts, tm), 0) == pos, 1.0, 0.0).astype(BF16)
        hs_ref[...] = jnp.dot(sel, h_ref[0], preferred_element_type=F32).astype(BF16)
        tab = tab_ref[0]
        t1 = tab.astype(BF16)
        r1 = tab - t1.astype(F32)
        t2 = r1.astype(BF16)
        t3 = (r1 - t2.astype(F32)).astype(BF16)
        gs_ref[...] = (jnp.dot(sel, t1, preferred_element_type=F32) + jnp.dot(sel, t2, preferred_element_type=F32)
                       + jnp.dot(sel, t3, preferred_element_type=F32))
        acc_ref[...] = jnp.zeros_like(acc_ref)

    per_step = w1_ref.shape[0]
    e_first = e * per_step
    seg = (b * n_t + t) * N_GROUPS + e_first // (n_experts // N_GROUPS)
    lane = lax.broadcasted_iota(jnp.int32, (MOE_BLK, LANE), 1)

    def block(j, carry):
        rows = pl.ds(pl.multiple_of(j * MOE_BLK, MOE_BLK), MOE_BLK)
        hb = hs_ref[rows, :]
        gs = gs_ref[rows, :]
        hids = []
        for i in range(per_step):
            a = jnp.dot(hb, w1_ref[i], preferred_element_type=F32)
            bb = jnp.dot(hb, w3_ref[i], preferred_element_type=F32)
            gcol = jnp.sum(jnp.where(lane == e_first + i, gs, 0.0), axis=-1, keepdims=True)
            hids.append((a * _sigmoid(a) * bb * gcol).astype(BF16))
        w2 = w2_ref[...].reshape(per_step * w2_ref.shape[1], w2_ref.shape[2])
        acc_ref[rows, :] += jnp.dot(jnp.concatenate(hids, axis=1), w2, preferred_element_type=F32)
        return carry

    lax.fori_loop(lo_ref[seg], hi_ref[seg], block, 0)

    @pl.when(e == pl.num_programs(2) - 1)
    def _():
        is_ctx = _ctx_rows(t, tm, n_ctx)
        pos = tab_ref[0][:, POS_COL:POS_COL + 1].astype(jnp.int32)
        back = jnp.where(lax.broadcasted_iota(jnp.int32, (tm, ts), 1) == pos, 1.0, 0.0).astype(BF16)
        y = jnp.dot(back, acc_ref[...].astype(BF16), preferred_element_type=F32)
        o_ref[0] = x_ref[0] + _mod_row(mod_ref, 5, is_ctx) * y


def _moe_call(xs, modsel, norm_g, router_wt, router_b, w1, w3, w2, *, n_ctx, tm):
    bsz, tt, d = xs.shape
    n_experts = router_wt.shape[0]
    n_t = tt // tm
    ts = tm + N_GROUPS * MOE_BLK
    row2 = lambda b, t: (b, t, 0)
    const2 = lambda b, t: (0, 0)
    h2, tab, pos, cnt = pl.pallas_call(
        functools.partial(_moe_route_kernel, tm=tm, n_ctx=n_ctx, n_experts=n_experts),
        out_shape=(jax.ShapeDtypeStruct((bsz, tt, d), BF16),
                   jax.ShapeDtypeStruct((bsz, tt, LANE), F32),
                   jax.ShapeDtypeStruct((bsz, n_t, 8, tm), F32),
                   jax.ShapeDtypeStruct((bsz, n_t, 8, LANE), F32)),
        grid=(bsz, n_t),
        in_specs=[pl.BlockSpec((1, tm, d), row2),
                  pl.BlockSpec((1, 2, 8, d), lambda b, t: (b, 0, 0, 0)),
                  pl.BlockSpec((1, d), const2),
                  pl.BlockSpec(router_wt.shape, const2),
                  pl.BlockSpec((n_experts, 1), const2)],
        out_specs=(pl.BlockSpec((1, tm, d), row2),
                   pl.BlockSpec((1, tm, LANE), row2),
                   pl.BlockSpec((1, 1, 8, tm), lambda b, t: (b, t, 0, 0)),
                   pl.BlockSpec((1, 1, 8, LANE), lambda b, t: (b, t, 0, 0))),
        compiler_params=pltpu.CompilerParams(dimension_semantics=("parallel", "parallel"),
                                             vmem_limit_bytes=VMEM_LIMIT),
        name="moe_route",
    )(xs, modsel, norm_g, router_wt, router_b)

    n_blk = (cnt[:, :, :N_GROUPS, 0].astype(jnp.int32) + (MOE_BLK - 1)) // MOE_BLK
    hi = jnp.cumsum(n_blk, axis=-1)
    lo = hi - n_blk

    row = lambda b, t, e, lo_r, hi_r: (b, t, 0)
    wexp = lambda b, t, e, lo_r, hi_r: (e, 0, 0)
    tile4 = lambda b, t, e, lo_r, hi_r: (b, t, 0, 0)
    return pl.pallas_call(
        functools.partial(_moe_expert_kernel, tm=tm, ts=ts, n_ctx=n_ctx, n_experts=n_experts),
        out_shape=jax.ShapeDtypeStruct((bsz, tt, d), F32),
        grid_spec=pltpu.PrefetchScalarGridSpec(
            num_scalar_prefetch=2,
            grid=(bsz, n_t, n_experts // MOE_EXPERTS_PER_STEP),
            in_specs=[pl.BlockSpec((1, tm, d), row),
                      pl.BlockSpec((1, 2, 8, d), lambda b, t, e, lo_r, hi_r: (b, 0, 0, 0)),
                      pl.BlockSpec((1, tm, d), row),
                      pl.BlockSpec((1, tm, LANE), row),
                      pl.BlockSpec((1, 1, 8, tm), tile4),
                      pl.BlockSpec((MOE_EXPERTS_PER_STEP,) + w1.shape[1:], wexp),
                      pl.BlockSpec((MOE_EXPERTS_PER_STEP,) + w3.shape[1:], wexp),
                      pl.BlockSpec((MOE_EXPERTS_PER_STEP,) + w2.shape[1:], wexp)],
            out_specs=pl.BlockSpec((1, tm, d), row),
            scratch_shapes=[pltpu.VMEM((ts, d), BF16),
                            pltpu.VMEM((ts, LANE), F32),
                            pltpu.VMEM((ts, d), F32)]),
        compiler_params=pltpu.CompilerParams(
            dimension_semantics=("parallel", "parallel", "arbitrary"),
            vmem_limit_bytes=VMEM_LIMIT),
        name="moe_experts",
    )(lo.reshape(-1), hi.reshape(-1), xs, modsel, h2, tab, pos, w1, w3, w2)


def kernel(x, c, ctx, c_ctx, mod_w, mod_b, norm1_g, norm2_g, w_in, rw_w0, rw_w_lora_b, rw_a0, rw_a_lora_b,
           rw_g_lora_b, rw_k_k, rw_k_a, rw_r_k, rw_ln_g, rw_ln_b, na_q_g, na_k_g, na_rpb, proj_a, proj_b,
           w_out, router_w, router_bias, moe_w1, moe_w3, moe_w2):
    bsz, seq, d = x.shape
    n_ctx = ctx.shape[1]
    depth = mod_w.shape[0]
    d_a = rw_w0.shape[2]
    d_b = proj_b.shape[1]
    lora_w = rw_w_lora_b.shape[2]
    lora_a = rw_a_lora_b.shape[2]
    lora_g = rw_g_lora_b.shape[1]
    n_lora = 2 * lora_w + 2 * lora_a + lora_g
    heads_a = d_a // HEAD_DIM
    heads_b = d_b // HEAD_DIM
    assert seq % (GRID_W * WIN_H) == 0 and n_ctx % 256 == 0 and bsz + 1 <= 16

    cs = jnp.zeros((16, d), F32).at[:bsz].set(c).at[bsz].set(c_ctx)
    mod = _mod_call(cs, mod_w, mod_b).reshape(depth, 16, 6, d)

    lane = jnp.arange(LANE)
    gs = (lane[:, None] // HEAD_DIM == lane[None, :] // HEAD_DIM).astype(BF16)
    router_wt = router_w.T
    router_b = router_bias.reshape(-1, 1)

    xs = jnp.concatenate([ctx, x], axis=1)
    for l in range(depth):
        last = l == depth - 1
        m_c = jnp.broadcast_to(mod[l, bsz][None], (bsz, 6, d))
        modsel = jnp.pad(jnp.stack([m_c, mod[l, :bsz]], axis=1), ((0, 0), (0, 0), (0, 2), (0, 0)))

        rkv, kk, lora, qkv, gates = _in_proj_call(
            xs, modsel, norm1_g[l][None], w_in[l].astype(BF16), rw_k_k[l][None],
            jnp.tile(na_q_g[l], heads_b)[None], jnp.tile(na_k_g[l], heads_b)[None], gs,
            n_ctx=n_ctx, d_a=d_a, d_b=d_b, n_lora=n_lora)
        y_f, y_r = _rwkv_call(rkv, kk, lora, rw_w0[l], rw_w_lora_b[l], rw_a0[l], rw_a_lora_b[l],
                              rw_k_a[l][None], n_ctx=n_ctx, lora_w=lora_w, lora_a=lora_a)
        yb = _attn_call(qkv, _bias_table(na_rpb[l]), n_ctx=n_ctx, with_ctx=not last)
        xs = _merge_call(xs, modsel, y_f, y_r, rkv, lora, yb, gates, rw_r_k[l].reshape(1, d_a),
                         rw_ln_g[l][None], rw_ln_b[l][None], rw_g_lora_b[l].astype(BF16),
                         proj_a[l].astype(BF16), proj_b[l].astype(BF16), w_out[l].astype(BF16), gs,
                         n_ctx=n_ctx, with_ctx=not last, lora_g=lora_g)
        moe_ctx = 0 if last else n_ctx
        tm = next(m for m in (1024, 768, 512, 256) if xs.shape[1] % m == 0)
        xs = _moe_call(xs, modsel, norm2_g[l][None], router_wt, router_b,
                       moe_w1[l].astype(BF16), moe_w3[l].astype(BF16), moe_w2[l].astype(BF16),
                       n_ctx=moe_ctx, tm=tm)
    return xs
```

```python
import functools
import math

import jax
import jax.numpy as jnp
import numpy as np
from jax import lax
from jax.experimental import pallas as pl
from jax.experimental.pallas import tpu as pltpu

F32 = jnp.float32
BF16 = jnp.bfloat16

HEAD_DIM = 64
GRID_W = 64
WIN_H = 8
WIN_W = 16
N_GROUPS = 4
TOP_K = 2
RMS_EPS = 1e-6
GN_EPS = 64e-5
LANE = 128
CHUNK = 64
SUB = 16
NEG_BIG = -1e30
DECAY_SCALE = math.exp(-0.5)
VMEM_LIMIT = 56 * 1024 * 1024


def _bdot(a, b):
    return jnp.dot(a.astype(BF16), b.astype(BF16), preferred_element_type=F32)


def _bdot_nt(a, b):
    return lax.dot_general(a.astype(BF16), b.astype(BF16), (((1,), (1,)), ((), ())),
                           preferred_element_type=F32)


def _bdot_tn(a, b):
    return lax.dot_general(a.astype(BF16), b.astype(BF16), (((0,), (0,)), ((), ())),
                           preferred_element_type=F32)


def _split(x):
    hi = x.astype(BF16)
    lo = (x - hi.astype(F32)).astype(BF16)
    return hi, lo


def _dot3(a, b):
    ah, al = _split(a)
    bh, bl = _split(b)
    return (jnp.dot(ah, bh, preferred_element_type=F32)
            + jnp.dot(al, bh, preferred_element_type=F32)
            + jnp.dot(ah, bl, preferred_element_type=F32))


def _sigmoid(x):
    return 1.0 / (1.0 + jnp.exp(-x))


def _group_sum(x, g128):
    parts = [_bdot(x[:, j * LANE:(j + 1) * LANE], g128) for j in range(x.shape[1] // LANE)]
    return jnp.concatenate(parts, axis=1)


def _group_sum2(x, g128):
    hi, lo = _split(x)
    parts = []
    for j in range(x.shape[1] // LANE):
        sl = slice(j * LANE, (j + 1) * LANE)
        parts.append(jnp.dot(hi[:, sl], g128, preferred_element_type=F32)
                     + jnp.dot(lo[:, sl], g128, preferred_element_type=F32))
    return jnp.concatenate(parts, axis=1)


def _mod_row(mod_ref, idx, is_ctx):
    mx = mod_ref[0, 1, idx:idx + 1, :]
    if is_ctx is None:
        return mx
    return jnp.where(is_ctx, mod_ref[0, 0, idx:idx + 1, :], mx)


def _ctx_rows(tile, tm, n_ctx):
    if n_ctx == 0:
        return None
    rows = tile * tm + lax.broadcasted_iota(jnp.int32, (tm, 1), 0)
    return rows < n_ctx


def _rms_mod(x, gain, shift, scale):
    xn = x * lax.rsqrt(jnp.mean(x * x, axis=-1, keepdims=True) + RMS_EPS) * gain
    return xn * (1.0 + scale) + shift


def _mod_kernel(c_ref, w_ref, b_ref, o_ref):
    c = c_ref[...]
    o_ref[0] = _dot3(c * _sigmoid(c), w_ref[0]) + b_ref[0]


def _mod_call(cs, mod_w, mod_b):
    depth, d, n = mod_w.shape
    tn = n // 4
    return pl.pallas_call(
        _mod_kernel,
        out_shape=jax.ShapeDtypeStruct((depth, cs.shape[0], n), F32),
        grid=(depth, n // tn),
        in_specs=[pl.BlockSpec(cs.shape, lambda l, j: (0, 0)),
                  pl.BlockSpec((1, d, tn), lambda l, j: (l, 0, j)),
                  pl.BlockSpec((1, 1, tn), lambda l, j: (l, 0, j))],
        out_specs=pl.BlockSpec((1, cs.shape[0], tn), lambda l, j: (l, 0, j)),
        compiler_params=pltpu.CompilerParams(dimension_semantics=("parallel", "parallel"),
                                             vmem_limit_bytes=VMEM_LIMIT),
        name="adaln_vectors",
    )(cs, mod_w, mod_b.reshape(depth, 1, n))


def _in_proj_kernel(x_ref, mod_ref, g_ref, w_ref, kkg_ref, qg_ref, kg_ref, gs_ref,
                    w0_ref, wlbh_ref, wlbl_ref, a0_ref, albh_ref, albl_ref, ka_ref,
                    rkv_ref, lg_ref, qkv_ref, gates_ref, opf_ref, opr_ref, v_ref, pef_ref, per_ref,
                    *, tm, n_ctx, d_a, d_b, lora_w, lora_a, lora_g):
    t = pl.program_id(1)
    is_ctx = _ctx_rows(t, tm, n_ctx)
    h = _rms_mod(x_ref[0], g_ref[...], _mod_row(mod_ref, 0, is_ctx), _mod_row(mod_ref, 1, is_ctx))
    h = h.astype(BF16)
    gs = gs_ref[...]
    n_lora = 2 * lora_w + 2 * lora_a + lora_g

    o = 0
    rkv = jnp.dot(h, w_ref[:, o:o + 3 * d_a], preferred_element_type=F32)
    rkv_ref[0] = rkv
    kkv = rkv[:, d_a:2 * d_a] * kkg_ref[...]
    ss = _group_sum2(kkv * kkv, gs)
    kk = kkv * lax.rsqrt(jnp.maximum(ss, 1e-12))
    o += 3 * d_a

    lora = jnp.dot(h, w_ref[:, o:o + n_lora], preferred_element_type=F32)
    lg_ref[0] = lora[:, n_lora - lora_g:]
    o += n_lora
    _rwkv_operand_rows(rkv, kk, lora, w0_ref, wlbh_ref, wlbl_ref, a0_ref, albh_ref, albl_ref, ka_ref,
                       opf_ref, opr_ref, v_ref, pef_ref, per_ref, tm=tm, d_a=d_a, lora_w=lora_w, lora_a=lora_a)

    qkv = jnp.dot(h, w_ref[:, o:o + 3 * d_b], preferred_element_type=F32)
    q, k = qkv[:, :d_b], qkv[:, d_b:2 * d_b]
    inv_n = 1.0 / HEAD_DIM
    qn = q * lax.rsqrt(_group_sum(q * q, gs) * inv_n + RMS_EPS) * qg_ref[...] * (HEAD_DIM ** -0.5)
    kn = k * lax.rsqrt(_group_sum(k * k, gs) * inv_n + RMS_EPS) * kg_ref[...]
    qkv_ref[0] = jnp.concatenate([qn, kn, qkv[:, 2 * d_b:]], axis=1).astype(BF16)
    o += 3 * d_b

    gates_ref[0] = _sigmoid(jnp.dot(h, w_ref[:, o:], preferred_element_type=F32)).astype(BF16)


def _in_proj_call(xs, modsel, norm_g, w_in, k_k, q_g, k_g, gs, w0, wlb, a0, alb, k_a,
                  *, n_ctx, d_a, d_b, lora_g, tm=256):
    bsz, tt, d = xs.shape
    lora_w, lora_a = wlb.shape[1], alb.shape[1]
    n_lora = 2 * lora_w + 2 * lora_a + lora_g
    n_gate = w_in.shape[1] - 3 * d_a - n_lora - 3 * d_b
    row = lambda b, t: (b, t, 0)
    const = lambda b, t: (0, 0)
    const3 = lambda b, t: (0, 0, 0)
    wlb_hi, wlb_lo = _split(wlb)
    alb_hi, alb_lo = _split(alb)
    op_shape = jax.ShapeDtypeStruct((bsz, tt, N_OPERANDS * d_a), BF16)
    op_spec = pl.BlockSpec((1, tm, N_OPERANDS * d_a), row)
    pe_shape = jax.ShapeDtypeStruct((bsz, tt // CHUNK, 8, d_a), F32)
    pe_spec = pl.BlockSpec((1, tm // CHUNK, 8, d_a), lambda b, t: (b, t, 0, 0))
    return pl.pallas_call(
        functools.partial(_in_proj_kernel, tm=tm, n_ctx=n_ctx, d_a=d_a, d_b=d_b,
                          lora_w=lora_w, lora_a=lora_a, lora_g=lora_g),
        out_shape=(jax.ShapeDtypeStruct((bsz, tt, 3 * d_a), F32),
                   jax.ShapeDtypeStruct((bsz, tt, lora_g), F32),
                   jax.ShapeDtypeStruct((bsz, tt, 3 * d_b), BF16),
                   jax.ShapeDtypeStruct((bsz, tt, n_gate), BF16),
                   op_shape, op_shape, jax.ShapeDtypeStruct((bsz, tt, d_a), BF16), pe_shape, pe_shape),
        grid=(bsz, tt // tm),
        in_specs=[pl.BlockSpec((1, tm, d), row),
                  pl.BlockSpec((1, 2, 8, d), lambda b, t: (b, 0, 0, 0)),
                  pl.BlockSpec((1, d), const),
                  pl.BlockSpec(w_in.shape, const),
                  pl.BlockSpec((1, d_a), const),
                  pl.BlockSpec((1, d_b), const),
                  pl.BlockSpec((1, d_b), const),
                  pl.BlockSpec((LANE, LANE), const),
                  pl.BlockSpec((2, 1, d_a), const3),
                  pl.BlockSpec((2, lora_w, d_a), const3),
                  pl.BlockSpec((2, lora_w, d_a), const3),
                  pl.BlockSpec((2, 1, d_a), const3),
                  pl.BlockSpec((2, lora_a, d_a), const3),
                  pl.BlockSpec((2, lora_a, d_a), const3),
                  pl.BlockSpec((1, d_a), const)],
        out_specs=(pl.BlockSpec((1, tm, 3 * d_a), row),
                   pl.BlockSpec((1, tm, lora_g), row),
                   pl.BlockSpec((1, tm, 3 * d_b), row),
                   pl.BlockSpec((1, tm, n_gate), row),
                   op_spec, op_spec, pl.BlockSpec((1, tm, d_a), row), pe_spec, pe_spec),
        compiler_params=pltpu.CompilerParams(dimension_semantics=("parallel", "parallel"),
                                             vmem_limit_bytes=VMEM_LIMIT),
        name="in_proj",
    )(xs, modsel, norm_g, w_in, k_k, q_g, k_g, gs,
      w0.reshape(2, 1, d_a), wlb_hi, wlb_lo, a0.reshape(2, 1, d_a), alb_hi, alb_lo, k_a)


N_OPERANDS = 6


def _dot3_presplit(x, w_hi, w_lo):
    xh, xl = _split(x)
    return (jnp.dot(xh, w_hi, preferred_element_type=F32) + jnp.dot(xl, w_hi, preferred_element_type=F32)
            + jnp.dot(xh, w_lo, preferred_element_type=F32))


def _rwkv_operand_rows(rkv, kk, lora, w0_ref, wlbh_ref, wlbl_ref, a0_ref, albh_ref, albl_ref, ka_ref,
                       opf_ref, opr_ref, v_ref, pef_ref, per_ref, *, tm, d_a, lora_w, lora_a):
    r, k, v = rkv[:, :d_a], rkv[:, d_a:2 * d_a], rkv[:, 2 * d_a:]
    v_ref[0] = v.astype(BF16)
    chunk_bits = int(math.log2(CHUNK))
    row = lax.broadcasted_iota(jnp.int32, (tm, tm), 0)
    col = lax.broadcasted_iota(jnp.int32, (tm, tm), 1)
    same = (row >> chunk_bits) == (col >> chunk_bits)
    ones_blk = jnp.where(same, 1.0, 0.0).astype(BF16)
    for d, (op_ref, pe_ref) in enumerate(((opf_ref, pef_ref), (opr_ref, per_ref))):
        lw = lora[:, d * lora_w:(d + 1) * lora_w]
        la = lora[:, 2 * lora_w + d * lora_a:2 * lora_w + (d + 1) * lora_a]
        wl = w0_ref[d] + _dot3_presplit(jnp.tanh(lw), wlbh_ref[d], wlbl_ref[d])
        ld = -DECAY_SCALE * _sigmoid(wl)
        al = _sigmoid(a0_ref[d] + _dot3_presplit(la, albh_ref[d], albl_ref[d]))
        kd = k * (1.0 + (al - 1.0) * ka_ref[...])
        before = (row <= col) if d else (row >= col)
        tri = jnp.where(same & before, 1.0, 0.0).astype(BF16)
        ld_hi, ld_lo = _split(ld)
        cum = jnp.dot(tri, ld_hi, preferred_element_type=F32) + jnp.dot(tri, ld_lo, preferred_element_type=F32)
        tot = (jnp.dot(ones_blk, ld_hi, preferred_element_type=F32)
               + jnp.dot(ones_blk, ld_lo, preferred_element_type=F32))
        p_end = jnp.exp(tot)
        p_inv = jnp.exp(-cum)
        bt = kk * al * p_inv
        kt = kd * p_inv
        op_ref[0] = jnp.concatenate([-kk * jnp.exp(cum - ld), r * jnp.exp(cum), bt, kt, bt * p_end, kt * p_end],
                                    axis=1).astype(BF16)
        for ci in range(tm // CHUNK):
            pe_ref[0, ci] = p_end[ci * CHUNK:ci * CHUNK + 8]


def _rwkv_pair_kernel(opf_ref, opr_ref, vf_ref, vr_ref, pef_ref, per_ref, yf_ref, yr_ref, st_ref, *, d_a):
    n = HEAD_DIM
    pw = 2 * n
    pairs = d_a // pw
    c = CHUNK
    nb = opf_ref.shape[0]

    @pl.when(pl.program_id(1) == 0)
    def _():
        st_ref[...] = jnp.zeros_like(st_ref)

    chains = [(b, d, p) for b in range(nb) for d in range(2) for p in range(pairs)]
    op_refs = (opf_ref, opr_ref)
    part = lambda i: [op_refs[d][b, :, i * d_a + p * pw:i * d_a + (p + 1) * pw] for b, d, p in chains]
    v_refs = (vf_ref, vr_ref)
    pe_refs = (pef_ref, per_ref)
    v = [v_refs[d][b, :, p * pw:(p + 1) * pw] for b, d, p in chains]
    p_end = [pe_refs[d][b, 0, 0:1, p * pw:(p + 1) * pw] for b, d, p in chains]
    each = lambda f, *ls: [f(*xs) for xs in zip(*ls)]
    rows2 = lambda a, b: jnp.concatenate([a, b], axis=0)
    bf = lambda t: t.astype(BF16)

    def bd(x):
        x = bf(x)
        left = lax.broadcasted_iota(jnp.int32, x.shape, 1) < n
        zero = jnp.zeros_like(x)
        return rows2(jnp.where(left, x, zero), jnp.where(left, zero, x))

    pdot = lambda a, b: jnp.dot(bf(a), bd(b), preferred_element_type=F32)
    ar = each(rows2, part(0), part(1))
    b_k = each(lambda b, k: rows2(bd(b), bd(k)), part(2), part(3))
    bk_end = each(rows2, part(4), part(5))
    s = [st_ref[b, d, p] for b, d, p in chains]

    mm_all = each(lambda a, w, ss: _bdot_nt(a, rows2(w, bd(ss))), ar, b_k, s)
    row = lax.broadcasted_iota(jnp.int32, (c, 2 * pw), 0)
    col = lax.broadcasted_iota(jnp.int32, (c, 2 * pw), 1) & (c - 1)
    incl = [(row <= col) if d else (row >= col) for _, d, _ in chains]
    strict = [(row < col) if d else (row > col) for _, d, _ in chains]
    m_a = each(lambda mm, msk: jnp.where(msk, mm[:c, :2 * pw], 0.0), mm_all, strict)
    m_r = each(lambda mm, msk: jnp.where(msk, mm[c:, :2 * pw], 0.0), mm_all, incl)
    ms_a = each(lambda mm: mm[:c, 2 * pw:], mm_all)
    ms_r = each(lambda mm: mm[c:, 2 * pw:], mm_all)
    a_ab = each(lambda mm: mm[:, :pw], m_a)
    a_ak = each(lambda mm: mm[:, pw:], m_a)

    row = lax.broadcasted_iota(jnp.int32, (c, pw), 0)
    col = lax.broadcasted_iota(jnp.int32, (c, pw), 1) & (c - 1)
    eye = jnp.where(row == col, 1.0, 0.0)
    sub_bits = int(math.log2(SUB))
    blk = (row >> sub_bits) == (col >> sub_bits)
    ad = each(lambda a: jnp.where(blk, a, 0.0), a_ab)
    ao = each(lambda a, b: a - b, a_ab, ad)
    td = each(lambda a: eye + a, ad)
    x = each(pdot, ad, ad)
    for _ in range(sub_bits - 2):
        res = each(lambda xx, tt: pdot(rows2(xx, tt), xx), x, td)
        td = each(lambda tt, rr: tt + rr[c:], td, res)
        x = each(lambda rr: rr[:c], res)
    td = each(lambda tt, xx: tt + pdot(tt, xx), td, x)
    x = each(pdot, td, ao)
    w = td
    for _ in range(int(math.log2(c // SUB)) - 1):
        res = each(lambda xx, ww: jnp.dot(bf(xx), jnp.concatenate([bd(xx), bd(ww)], axis=1),
                                          preferred_element_type=F32), x, w)
        w = each(lambda ww, rr: ww + rr[:, pw:], w, res)
        x = each(lambda rr: rr[:, :pw], res)
    t_inv = each(lambda ww, xx: ww + pdot(xx, ww), w, x)

    akv = each(pdot, a_ak, v)
    u = each(lambda t, m0, m1: bf(pdot(t, m0 + m1)), t_inv, ms_a, akv)
    y = each(lambda m0, a, uu, vv: m0 + jnp.dot(bf(a), rows2(bd(uu), bd(vv)), preferred_element_type=F32),
             ms_r, m_r, u, v)
    left = lax.broadcasted_iota(jnp.int32, (n, pw), 1) < n
    upd = each(lambda uu, vv, w_end: _bdot_tn(rows2(uu, vv), w_end), u, v, bk_end)
    s_new = each(lambda ss, p, dd: ss * p + jnp.where(left, dd[:n], dd[n:]), s, p_end, upd)
    for (b, d, p), val in zip(chains, s_new):
        st_ref[b, d, p] = val
    for b in range(nb):
        yf_ref[b] = jnp.concatenate(y[2 * b * pairs:(2 * b + 1) * pairs], axis=1)
        yr_ref[b] = jnp.concatenate(y[(2 * b + 1) * pairs:(2 * b + 2) * pairs], axis=1)


def _rwkv_call(op_f, op_r, v, pe_f, pe_r, *, n_ctx):
    bsz, tt, d_a = v.shape
    n_chunks = tt // CHUNK
    nc_ctx = n_ctx // CHUNK
    heads = d_a // HEAD_DIM

    rev_chunk = lambda j: jnp.where(j < nc_ctx, nc_ctx - 1 - j, n_chunks - 1 - (j - nc_ctx))
    fwd = lambda b, j: (b, j, 0)
    rev = lambda b, j: (b, rev_chunk(j), 0)
    y_shape = jax.ShapeDtypeStruct((bsz, tt, d_a), F32)
    nb = next(m for m in (2, 1) if bsz % m == 0)
    return pl.pallas_call(
        functools.partial(_rwkv_pair_kernel, d_a=d_a),
        out_shape=(y_shape, y_shape),
        grid=(bsz // nb, n_chunks),
        in_specs=[pl.BlockSpec((nb, CHUNK, N_OPERANDS * d_a), fwd),
                  pl.BlockSpec((nb, CHUNK, N_OPERANDS * d_a), rev),
                  pl.BlockSpec((nb, CHUNK, d_a), fwd),
                  pl.BlockSpec((nb, CHUNK, d_a), rev),
                  pl.BlockSpec((nb, 1, 8, d_a), lambda b, j: (b, j, 0, 0)),
                  pl.BlockSpec((nb, 1, 8, d_a), lambda b, j: (b, rev_chunk(j), 0, 0))],
        out_specs=(pl.BlockSpec((nb, CHUNK, d_a), fwd), pl.BlockSpec((nb, CHUNK, d_a), rev)),
        scratch_shapes=[pltpu.VMEM((nb, 2, heads // 2, HEAD_DIM, 2 * HEAD_DIM), F32)],
        compiler_params=pltpu.CompilerParams(dimension_semantics=("parallel", "arbitrary"),
                                             vmem_limit_bytes=VMEM_LIMIT),
        name="rwkv7_chunk_scan",
    )(op_f, op_r, v, v, pe_f, pe_r)


def _softmax_pv(heads_parts):
    ms = [functools.reduce(jnp.maximum, [jnp.max(s, axis=-1, keepdims=True) for s, _ in parts])
          for parts in heads_parts]
    ps = [[jnp.exp(s - m) for s, _ in parts] for parts, m in zip(heads_parts, ms)]
    dens = [sum(jnp.sum(p, axis=-1, keepdims=True) for p in pp) for pp in ps]
    nums = [sum(jnp.dot(p.astype(BF16), vals, preferred_element_type=F32) for p, (_, vals) in zip(pp, parts))
            for pp, parts in zip(ps, heads_parts)]
    return [num / den for num, den in zip(nums, dens)]


def _attn_kernel(q_ref, k_ref, v_ref, bias_ref, o_ref, *, n_ctx, rows, q_off, d_b):
    n = HEAD_DIM
    heads = d_b // n
    nq_ctx = n_ctx // GRID_W
    i = pl.program_id(1) + q_off
    nb = q_ref.shape[0]
    q = [q_ref[b] for b in range(nb)]
    kc = [k_ref[b, 0:n_ctx, :] for b in range(nb)]
    vc = [v_ref[b, 0:n_ctx, :] for b in range(nb)]
    units = [(b, h, slice(h * n, (h + 1) * n)) for b in range(nb) for h in range(heads)]

    def store(outs):
        for b in range(nb):
            o_ref[b] = jnp.concatenate(outs[b * heads:(b + 1) * heads], axis=1).astype(o_ref.dtype)

    def latent():
        li = i - nq_ctx
        r0 = jnp.clip(li - WIN_H // 2, 0, rows - WIN_H)
        start = pl.multiple_of(n_ctx + r0 * GRID_W, GRID_W)
        kl = [k_ref[b, pl.ds(start, WIN_H * GRID_W), :] for b in range(nb)]
        vl = [v_ref[b, pl.ds(start, WIN_H * GRID_W), :] for b in range(nb)]
        s_l = [_bdot_nt(q[b][:, sl], kl[b][:, sl]) + bias_ref[0, h] for b, h, sl in units]
        s_c = [_bdot_nt(q[b][:, sl], kc[b][:, sl]) for b, h, sl in units]
        store(_softmax_pv([[(sa, vl[b][:, sl]), (sb, vc[b][:, sl])]
                           for sa, sb, (b, h, sl) in zip(s_l, s_c, units)]))

    def context():
        store(_softmax_pv([[(_bdot_nt(q[b][:, sl], kc[b][:, sl]), vc[b][:, sl])] for b, h, sl in units]))

    if q_off == 0:
        pl.when(i >= nq_ctx)(latent)
        pl.when(i < nq_ctx)(context)
    else:
        latent()


def _bias_table(rpb):
    w = GRID_W
    heads = rpb.shape[0]
    j = np.arange(w)
    col_start = np.clip(j - WIN_W // 2, 0, w - WIN_W)
    in_win = (j[None, :] >= col_start[:, None]) & (j[None, :] < col_start[:, None] + WIN_W)
    rpb = rpb.astype(F32)
    ext = jnp.concatenate([jnp.repeat(rpb[..., :1], w - WIN_W, axis=-1), rpb,
                           jnp.repeat(rpb[..., -1:], w - WIN_W, axis=-1)], axis=-1)
    toe = jnp.stack([ext[..., w - 1 - qc:2 * w - 1 - qc] for qc in range(w)], axis=2)
    tz = jnp.where(in_win, toe, NEG_BIG)
    bt = jnp.stack([tz[:, v:v + WIN_H] for v in range(WIN_H)], axis=0)
    return bt.transpose(0, 1, 3, 2, 4).reshape(WIN_H, heads, w, WIN_H * w)


def _attn_call(qkv, bias_tab, *, n_ctx, with_ctx):
    bsz, tt, d3 = qkv.shape
    d_b = d3 // 3
    heads = d_b // HEAD_DIM
    rows = (tt - n_ctx) // GRID_W
    nq_ctx = n_ctx // GRID_W
    q_off = 0 if with_ctx else nq_ctx
    n_q = tt // GRID_W - q_off
    nb = next(m for m in (4, 2, 1) if bsz % m == 0)

    def variant(b, i):
        li = jnp.maximum(i + q_off - nq_ctx, 0)
        return (jnp.clip(li - WIN_H // 2, 0, rows - WIN_H) - li + WIN_H - 1, 0, 0, 0)

    return pl.pallas_call(
        functools.partial(_attn_kernel, n_ctx=n_ctx, rows=rows, q_off=q_off, d_b=d_b),
        out_shape=jax.ShapeDtypeStruct((bsz, n_q * GRID_W, d_b), BF16),
        grid=(bsz // nb, n_q),
        in_specs=[pl.BlockSpec((nb, GRID_W, d_b), lambda b, i: (b, i + q_off, 0)),
                  pl.BlockSpec((nb, tt, d_b), lambda b, i: (b, 0, 1)),
                  pl.BlockSpec((nb, tt, d_b), lambda b, i: (b, 0, 2)),
                  pl.BlockSpec((1, heads, GRID_W, WIN_H * GRID_W), variant)],
        out_specs=pl.BlockSpec((nb, GRID_W, d_b), lambda b, i: (b, i, 0)),
        compiler_params=pltpu.CompilerParams(dimension_semantics=("parallel", "arbitrary"),
                                             vmem_limit_bytes=VMEM_LIMIT),
        name="neighbourhood_attention",
    )(qkv, qkv, qkv, bias_tab)


def _merge_kernel(x_ref, mod_ref, y0_ref, y1_ref, rkv_ref, lg_ref, yb_ref, gates_ref,
                  rk_ref, lng_ref, lnb_ref, glb_ref, pa_ref, pb_ref, wo_ref, gs_ref, o_ref,
                  *, tm, n_ctx, t_off, d_a):
    t = pl.program_id(1) + t_off
    is_ctx = _ctx_rows(t, tm, n_ctx)
    gs = gs_ref[...]
    inv_n = 1.0 / HEAD_DIM
    y = y0_ref[0] + y1_ref[0]
    mu = _group_sum2(y, gs) * inv_n
    yc = y - mu
    var = _group_sum2(yc * yc, gs) * inv_n
    yn = yc * lax.rsqrt(var + GN_EPS) * lng_ref[...] + lnb_ref[...]
    rkv = rkv_ref[0]
    r, k, v = rkv[:, :d_a], rkv[:, d_a:2 * d_a], rkv[:, 2 * d_a:]
    bonus = _group_sum2(r * k * rk_ref[...], gs) * v
    g = _bdot(_sigmoid(lg_ref[0]), glb_ref[...])
    ya = (yn + bonus) * g
    gates = gates_ref[0].astype(F32)
    dm = gates.shape[1] // 2
    merged = gates[:, :dm] * _bdot(ya, pa_ref[...]) + gates[:, dm:] * _bdot(yb_ref[0], pb_ref[...])
    o_ref[0] = x_ref[0] + _mod_row(mod_ref, 2, is_ctx) * _bdot(merged, wo_ref[...])


def _merge_call(xs, modsel, y_f, y_r, rkv, lg, yb, gates, r_k, ln_g, ln_b, glb, pa, pb, wo, gs,
                *, n_ctx, with_ctx, tm=256):
    bsz, tt, d = xs.shape
    d_a = rkv.shape[2] // 3
    t_off = 0 if with_ctx else n_ctx // tm
    n_t = tt // tm - t_off
    yb_off = 0 if with_ctx else -t_off
    row = lambda b, t: (b, t + t_off, 0)
    const = lambda b, t: (0, 0)
    return pl.pallas_call(
        functools.partial(_merge_kernel, tm=tm, n_ctx=n_ctx, t_off=t_off, d_a=d_a),
        out_shape=jax.ShapeDtypeStruct((bsz, n_t * tm, d), F32),
        grid=(bsz, n_t),
        in_specs=[pl.BlockSpec((1, tm, d), row),
                  pl.BlockSpec((1, 2, 8, d), lambda b, t: (b, 0, 0, 0)),
                  pl.BlockSpec((1, tm, d_a), row),
                  pl.BlockSpec((1, tm, d_a), row),
                  pl.BlockSpec((1, tm, 3 * d_a), row),
                  pl.BlockSpec((1, tm, lg.shape[2]), row),
                  pl.BlockSpec((1, tm, yb.shape[2]), lambda b, t: (b, t + t_off + yb_off, 0)),
                  pl.BlockSpec((1, tm, gates.shape[2]), row),
                  pl.BlockSpec((1, d_a), const),
                  pl.BlockSpec((1, d_a), const),
                  pl.BlockSpec((1, d_a), const),
                  pl.BlockSpec(glb.shape, const),
                  pl.BlockSpec(pa.shape, const),
                  pl.BlockSpec(pb.shape, const),
                  pl.BlockSpec(wo.shape, const),
                  pl.BlockSpec((LANE, LANE), const)],
        out_specs=pl.BlockSpec((1, tm, d), lambda b, t: (b, t, 0)),
        compiler_params=pltpu.CompilerParams(dimension_semantics=("parallel", "parallel"),
                                             vmem_limit_bytes=VMEM_LIMIT),
        name="branch_merge",
    )(xs, modsel, y_f, y_r, rkv, lg, yb, gates, r_k, ln_g, ln_b, glb, pa, pb, wo, gs)


def _route(sel, scores, n_experts):
    per = n_experts // N_GROUPS
    in_top = []
    for e in range(n_experts):
        g0 = (e // per) * per
        rank = 0.0
        for o in range(g0, g0 + per):
            if o == e:
                continue
            ahead = (sel[o] >= sel[e]) if o < e else (sel[o] > sel[e])
            rank = rank + jnp.where(ahead, 1.0, 0.0)
        in_top.append(rank < TOP_K)
    grp = [sum(jnp.where(in_top[e], sel[e], 0.0) for e in range(g * per, (g + 1) * per))
           for g in range(N_GROUPS)]
    best = jnp.zeros_like(grp[0], dtype=jnp.int32)
    best_s = grp[0]
    for g in range(1, N_GROUPS):
        better = grp[g] > best_s
        best = jnp.where(better, g, best)
        best_s = jnp.where(better, grp[g], best_s)
    chosen = [in_top[e] & (best == e // per) for e in range(n_experts)]
    den = sum(jnp.where(chosen[e], scores[e], 0.0) for e in range(n_experts))
    return [jnp.where(chosen[e], scores[e] / den, 0.0) for e in range(n_experts)], best


MOE_EXPERTS_PER_STEP = 2
MOE_BLK = 128
POS_COL = 16


def _moe_route_kernel(x_ref, mod_ref, g_ref, rw_ref, rb_ref, hs_ref, gs_ref, tab_ref, cnt_ref,
                      *, tm, ts, n_ctx, n_experts):
    t = pl.program_id(1)
    is_ctx = _ctx_rows(t, tm, n_ctx)
    h = _rms_mod(x_ref[0], g_ref[...], _mod_row(mod_ref, 3, is_ctx), _mod_row(mod_ref, 4, is_ctx))
    hh, hl = _split(h)
    wh, wl = _split(rw_ref[...])
    nt = lambda a, b: lax.dot_general(a, b, (((1,), (1,)), ((), ())), preferred_element_type=F32)
    logits = nt(wh, hh) + nt(wh, hl) + nt(wl, hh)
    scores = _sigmoid(logits)
    sel = scores + rb_ref[...]
    gates, best = _route([sel[i:i + 1, :] for i in range(n_experts)],
                         [scores[i:i + 1, :] for i in range(n_experts)], n_experts)

    in_grp = [jnp.where(best == g, 1.0, 0.0) for g in range(N_GROUPS)]
    grp8 = jnp.concatenate(in_grp + [jnp.zeros((8 - N_GROUPS, tm), F32)], axis=0).astype(BF16)
    row = lax.broadcasted_iota(jnp.int32, (tm, tm), 0)
    col = lax.broadcasted_iota(jnp.int32, (tm, tm), 1)
    upper = jnp.where(row <= col, 1.0, 0.0).astype(BF16)
    run = jnp.dot(grp8, upper, preferred_element_type=F32)
    pos = jnp.zeros((1, tm), F32)
    seg = jnp.zeros((1, 1), F32)
    cnts = []
    for g in range(N_GROUPS):
        cnt = run[g:g + 1, tm - 1:tm]
        cnts.append(cnt)
        pos = pos + in_grp[g] * (seg + run[g:g + 1, :] - 1.0)
        seg = seg + jnp.floor((cnt + (MOE_BLK - 1)) * (1.0 / MOE_BLK)) * MOE_BLK
    cnt_ref[0, 0] = jnp.concatenate([jnp.broadcast_to(c, (1, LANE)) for c in cnts]
                                    + [jnp.zeros((8 - N_GROUPS, LANE), F32)], axis=0)
    tab = jnp.concatenate(gates + [pos, jnp.zeros((LANE - n_experts - 1, tm), F32)], axis=0).T
    tab_ref[0] = tab
    sel = jnp.where(lax.broadcasted_iota(jnp.int32, (ts, tm), 0) == pos.astype(jnp.int32), 1.0, 0.0).astype(BF16)
    hs_ref[0, 0] = jnp.dot(sel, h.astype(BF16), preferred_element_type=F32).astype(BF16)
    t1 = tab.astype(BF16)
    r1 = tab - t1.astype(F32)
    t2 = r1.astype(BF16)
    t3 = (r1 - t2.astype(F32)).astype(BF16)
    gs_ref[0, 0] = (jnp.dot(sel, t1, preferred_element_type=F32) + jnp.dot(sel, t2, preferred_element_type=F32)
                    + jnp.dot(sel, t3, preferred_element_type=F32))


def _moe_expert_kernel(lo_ref, hi_ref, x_ref, mod_ref, hs_ref, gs_ref, tab_ref, w1_ref, w3_ref, w2_ref, o_ref,
                       acc_ref, *, tm, ts, n_ctx, n_experts):
    b = pl.program_id(0)
    t = pl.program_id(1)
    e = pl.program_id(2)
    n_t = pl.num_programs(1)

    @pl.when(e == 0)
    def _():
        acc_ref[...] = jnp.zeros_like(acc_ref)

    per_step = w1_ref.shape[0]
    e_first = e * per_step
    seg = (b * n_t + t) * N_GROUPS + e_first // (n_experts // N_GROUPS)
    lane = lax.broadcasted_iota(jnp.int32, (MOE_BLK, LANE), 1)

    def block(j, carry):
        rows = pl.ds(pl.multiple_of(j * MOE_BLK, MOE_BLK), MOE_BLK)
        hb = hs_ref[0, 0, rows, :]
        gs = gs_ref[0, 0, rows, :]
        hids = []
        for i in range(per_step):
            a = jnp.dot(hb, w1_ref[i], preferred_element_type=F32)
            bb = jnp.dot(hb, w3_ref[i], preferred_element_type=F32)
            gcol = jnp.sum(jnp.where(lane == e_first + i, gs, 0.0), axis=-1, keepdims=True)
            hids.append((a * _sigmoid(a) * bb * gcol).astype(BF16))
        w2 = w2_ref[...].reshape(per_step * w2_ref.shape[1], w2_ref.shape[2])
        acc_ref[rows, :] += jnp.dot(jnp.concatenate(hids, axis=1), w2, preferred_element_type=F32)
        return carry

    lax.fori_loop(lo_ref[seg], hi_ref[seg], block, 0)

    @pl.when(e == pl.num_programs(2) - 1)
    def _():
        is_ctx = _ctx_rows(t, tm, n_ctx)
        pos = tab_ref[0][:, POS_COL:POS_COL + 1].astype(jnp.int32)
        back = jnp.where(lax.broadcasted_iota(jnp.int32, (tm, ts), 1) == pos, 1.0, 0.0).astype(BF16)
        y = jnp.dot(back, acc_ref[...].astype(BF16), preferred_element_type=F32)
        o_ref[0] = x_ref[0] + _mod_row(mod_ref, 5, is_ctx) * y


def _moe_call(xs, modsel, norm_g, router_wt, router_b, w1, w3, w2, *, n_ctx, tm):
    bsz, tt, d = xs.shape
    n_experts = router_wt.shape[0]
    n_t = tt // tm
    ts = tm + N_GROUPS * MOE_BLK
    row2 = lambda b, t: (b, t, 0)
    const2 = lambda b, t: (0, 0)
    tile2 = lambda b, t: (b, t, 0, 0)
    hs, gs, tab, cnt = pl.pallas_call(
        functools.partial(_moe_route_kernel, tm=tm, ts=ts, n_ctx=n_ctx, n_experts=n_experts),
        out_shape=(jax.ShapeDtypeStruct((bsz, n_t, ts, d), BF16),
                   jax.ShapeDtypeStruct((bsz, n_t, ts, LANE), F32),
                   jax.ShapeDtypeStruct((bsz, tt, LANE), F32),
                   jax.ShapeDtypeStruct((bsz, n_t, 8, LANE), F32)),
        grid=(bsz, n_t),
        in_specs=[pl.BlockSpec((1, tm, d), row2),
                  pl.BlockSpec((1, 2, 8, d), lambda b, t: (b, 0, 0, 0)),
                  pl.BlockSpec((1, d), const2),
                  pl.BlockSpec(router_wt.shape, const2),
                  pl.BlockSpec((n_experts, 1), const2)],
        out_specs=(pl.BlockSpec((1, 1, ts, d), tile2),
                   pl.BlockSpec((1, 1, ts, LANE), tile2),
                   pl.BlockSpec((1, tm, LANE), row2),
                   pl.BlockSpec((1, 1, 8, LANE), tile2)),
        compiler_params=pltpu.CompilerParams(dimension_semantics=("parallel", "parallel"),
                                             vmem_limit_bytes=VMEM_LIMIT),
        name="moe_route",
    )(xs, modsel, norm_g, router_wt, router_b)

    n_blk = (cnt[:, :, :N_GROUPS, 0].astype(jnp.int32) + (MOE_BLK - 1)) // MOE_BLK
    hi = jnp.cumsum(n_blk, axis=-1)
    lo = hi - n_blk

    row = lambda b, t, e, lo_r, hi_r: (b, t, 0)
    wexp = lambda b, t, e, lo_r, hi_r: (e, 0, 0)
    tile4 = lambda b, t, e, lo_r, hi_r: (b, t, 0, 0)
    return pl.pallas_call(
        functools.partial(_moe_expert_kernel, tm=tm, ts=ts, n_ctx=n_ctx, n_experts=n_experts),
        out_shape=jax.ShapeDtypeStruct((bsz, tt, d), F32),
        grid_spec=pltpu.PrefetchScalarGridSpec(
            num_scalar_prefetch=2,
            grid=(bsz, n_t, n_experts // MOE_EXPERTS_PER_STEP),
            in_specs=[pl.BlockSpec((1, tm, d), row),
                      pl.BlockSpec((1, 2, 8, d), lambda b, t, e, lo_r, hi_r: (b, 0, 0, 0)),
                      pl.BlockSpec((1, 1, ts, d), tile4),
                      pl.BlockSpec((1, 1, ts, LANE), tile4),
                      pl.BlockSpec((1, tm, LANE), row),
                      pl.BlockSpec((MOE_EXPERTS_PER_STEP,) + w1.shape[1:], wexp),
                      pl.BlockSpec((MOE_EXPERTS_PER_STEP,) + w3.shape[1:], wexp),
                      pl.BlockSpec((MOE_EXPERTS_PER_STEP,) + w2.shape[1:], wexp)],
            out_specs=pl.BlockSpec((1, tm, d), row),
            scratch_shapes=[pltpu.VMEM((ts, d), F32)]),
        compiler_params=pltpu.CompilerParams(
            dimension_semantics=("parallel", "parallel", "arbitrary"),
            vmem_limit_bytes=VMEM_LIMIT),
        name="moe_experts",
    )(lo.reshape(-1), hi.reshape(-1), xs, modsel, hs, gs, tab, w1, w3, w2)


def kernel(x, c, ctx, c_ctx, mod_w, mod_b, norm1_g, norm2_g, w_in, rw_w0, rw_w_lora_b, rw_a0, rw_a_lora_b,
           rw_g_lora_b, rw_k_k, rw_k_a, rw_r_k, rw_ln_g, rw_ln_b, na_q_g, na_k_g, na_rpb, proj_a, proj_b,
           w_out, router_w, router_bias, moe_w1, moe_w3, moe_w2):
    bsz, seq, d = x.shape
    n_ctx = ctx.shape[1]
    depth = mod_w.shape[0]
    d_a = rw_w0.shape[2]
    d_b = proj_b.shape[1]
    lora_g = rw_g_lora_b.shape[1]
    heads_b = d_b // HEAD_DIM
    assert seq % (GRID_W * WIN_H) == 0 and n_ctx % 256 == 0 and bsz + 1 <= 16

    cs = jnp.zeros((16, d), F32).at[:bsz].set(c).at[bsz].set(c_ctx)
    mod = _mod_call(cs, mod_w, mod_b).reshape(depth, 16, 6, d)

    lane = jnp.arange(LANE)
    gs = (lane[:, None] // HEAD_DIM == lane[None, :] // HEAD_DIM).astype(BF16)
    router_wt = router_w.T
    router_b = router_bias.reshape(-1, 1)

    xs = jnp.concatenate([ctx, x], axis=1)
    for l in range(depth):
        last = l == depth - 1
        m_c = jnp.broadcast_to(mod[l, bsz][None], (bsz, 6, d))
        modsel = jnp.pad(jnp.stack([m_c, mod[l, :bsz]], axis=1), ((0, 0), (0, 0), (0, 2), (0, 0)))

        rkv, lg, qkv, gates, op_f, op_r, v_a, pe_f, pe_r = _in_proj_call(
            xs, modsel, norm1_g[l][None], w_in[l].astype(BF16), rw_k_k[l][None],
            jnp.tile(na_q_g[l], heads_b)[None], jnp.tile(na_k_g[l], heads_b)[None], gs,
            rw_w0[l], rw_w_lora_b[l], rw_a0[l], rw_a_lora_b[l], rw_k_a[l][None],
            n_ctx=n_ctx, d_a=d_a, d_b=d_b, lora_g=lora_g)
        y_f, y_r = _rwkv_call(op_f, op_r, v_a, pe_f, pe_r, n_ctx=n_ctx)
        yb = _attn_call(qkv, _bias_table(na_rpb[l]), n_ctx=n_ctx, with_ctx=not last)
        xs = _merge_call(xs, modsel, y_f, y_r, rkv, lg, yb, gates, rw_r_k[l].reshape(1, d_a),
                         rw_ln_g[l][None], rw_ln_b[l][None], rw_g_lora_b[l].astype(BF16),
                         proj_a[l].astype(BF16), proj_b[l].astype(BF16), w_out[l].astype(BF16), gs,
                         n_ctx=n_ctx, with_ctx=not last)
        moe_ctx = 0 if last else n_ctx
        tm = next(m for m in (1024, 768, 512, 256) if xs.shape[1] % m == 0)
        xs = _moe_call(xs, modsel, norm2_g[l][None], router_wt, router_b,
                       moe_w1[l].astype(BF16), moe_w3[l].astype(BF16), moe_w2[l].astype(BF16),
                       n_ctx=moe_ctx, tm=tm)
    return xs
```

```python
import functools
import math

import jax
import jax.numpy as jnp
import numpy as np
from jax import lax
from jax.experimental import pallas as pl
from jax.experimental.pallas import tpu as pltpu

F32 = jnp.float32
BF16 = jnp.bfloat16

HEAD_DIM = 64
GRID_W = 64
WIN_H = 8
WIN_W = 16
N_GROUPS = 4
TOP_K = 2
RMS_EPS = 1e-6
GN_EPS = 64e-5
LANE = 128
CHUNK = 64
SUB = 16
NEG_BIG = -1e30
DECAY_SCALE = math.exp(-0.5)
VMEM_LIMIT = 56 * 1024 * 1024


def _bdot(a, b):
    return jnp.dot(a.astype(BF16), b.astype(BF16), preferred_element_type=F32)


def _bdot_nt(a, b):
    return lax.dot_general(a.astype(BF16), b.astype(BF16), (((1,), (1,)), ((), ())),
                           preferred_element_type=F32)


def _bdot_tn(a, b):
    return lax.dot_general(a.astype(BF16), b.astype(BF16), (((0,), (0,)), ((), ())),
                           preferred_element_type=F32)


def _split(x):
    hi = x.astype(BF16)
    lo = (x - hi.astype(F32)).astype(BF16)
    return hi, lo


def _dot3(a, b):
    ah, al = _split(a)
    bh, bl = _split(b)
    return (jnp.dot(ah, bh, preferred_element_type=F32)
            + jnp.dot(al, bh, preferred_element_type=F32)
            + jnp.dot(ah, bl, preferred_element_type=F32))


def _sigmoid(x):
    return 1.0 / (1.0 + jnp.exp(-x))


def _group_sum(x, g128):
    parts = [_bdot(x[:, j * LANE:(j + 1) * LANE], g128) for j in range(x.shape[1] // LANE)]
    return jnp.concatenate(parts, axis=1)


def _group_sum2(x, g128):
    hi, lo = _split(x)
    parts = []
    for j in range(x.shape[1] // LANE):
        sl = slice(j * LANE, (j + 1) * LANE)
        parts.append(jnp.dot(hi[:, sl], g128, preferred_element_type=F32)
                     + jnp.dot(lo[:, sl], g128, preferred_element_type=F32))
    return jnp.concatenate(parts, axis=1)


def _mod_row(mod_ref, idx, is_ctx):
    mx = mod_ref[0, 1, idx:idx + 1, :]
    if is_ctx is None:
        return mx
    return jnp.where(is_ctx, mod_ref[0, 0, idx:idx + 1, :], mx)


def _ctx_rows(tile, tm, n_ctx):
    if n_ctx == 0:
        return None
    rows = tile * tm + lax.broadcasted_iota(jnp.int32, (tm, 1), 0)
    return rows < n_ctx


def _rms_mod(x, gain, shift, scale):
    xn = x * lax.rsqrt(jnp.mean(x * x, axis=-1, keepdims=True) + RMS_EPS) * gain
    return xn * (1.0 + scale) + shift


CAST_BLOCK_BYTES = 8 * 1024 * 1024


def _cast_kernel(w_ref, o_ref):
    o_ref[...] = w_ref[...].astype(o_ref.dtype)


def _to_bf16(w):
    c = w.shape[-1]
    w2 = w.reshape(-1, c)
    rows = w2.shape[0]
    rb = max(16, min(rows, CAST_BLOCK_BYTES // (c * 4)) // 16 * 16)
    while rows % rb:
        rb -= 16
    out = pl.pallas_call(
        _cast_kernel,
        out_shape=jax.ShapeDtypeStruct(w2.shape, BF16),
        grid=(rows // rb,),
        in_specs=[pl.BlockSpec((rb, c), lambda i: (i, 0))],
        out_specs=pl.BlockSpec((rb, c), lambda i: (i, 0)),
        compiler_params=pltpu.CompilerParams(dimension_semantics=("parallel",), vmem_limit_bytes=VMEM_LIMIT),
        name="to_bf16",
    )(w2)
    return out.reshape(w.shape)


def _mod_kernel(c_ref, w_ref, b_ref, o_ref):
    c = c_ref[...]
    o_ref[0] = _dot3(c * _sigmoid(c), w_ref[0]) + b_ref[0]


def _mod_call(cs, mod_w, mod_b):
    depth, d, n = mod_w.shape
    tn = n // 4
    return pl.pallas_call(
        _mod_kernel,
        out_shape=jax.ShapeDtypeStruct((depth, cs.shape[0], n), F32),
        grid=(depth, n // tn),
        in_specs=[pl.BlockSpec(cs.shape, lambda l, j: (0, 0)),
                  pl.BlockSpec((1, d, tn), lambda l, j: (l, 0, j)),
                  pl.BlockSpec((1, 1, tn), lambda l, j: (l, 0, j))],
        out_specs=pl.BlockSpec((1, cs.shape[0], tn), lambda l, j: (l, 0, j)),
        compiler_params=pltpu.CompilerParams(dimension_semantics=("parallel", "parallel"),
                                             vmem_limit_bytes=VMEM_LIMIT),
        name="adaln_vectors",
    )(cs, mod_w, mod_b.reshape(depth, 1, n))


def _in_proj_kernel(x_ref, mod_ref, g_ref, w_ref, kkg_ref, qg_ref, kg_ref, gs_ref,
                    w0_ref, wlbh_ref, wlbl_ref, a0_ref, albh_ref, albl_ref, ka_ref,
                    rkv_ref, lg_ref, qkv_ref, gates_ref, opf_ref, opr_ref, v_ref, pef_ref, per_ref,
                    *, tm, n_ctx, d_a, d_b, lora_w, lora_a, lora_g):
    t = pl.program_id(1)
    is_ctx = _ctx_rows(t, tm, n_ctx)
    h = _rms_mod(x_ref[0], g_ref[...], _mod_row(mod_ref, 0, is_ctx), _mod_row(mod_ref, 1, is_ctx))
    h = h.astype(BF16)
    gs = gs_ref[...]
    n_lora = 2 * lora_w + 2 * lora_a + lora_g

    o = 0
    rkv = jnp.dot(h, w_ref[0, :, o:o + 3 * d_a], preferred_element_type=F32)
    rkv_ref[0] = rkv
    kkv = rkv[:, d_a:2 * d_a] * kkg_ref[...]
    ss = _group_sum2(kkv * kkv, gs)
    kk = kkv * lax.rsqrt(jnp.maximum(ss, 1e-12))
    o += 3 * d_a

    lora = jnp.dot(h, w_ref[0, :, o:o + n_lora], preferred_element_type=F32)
    lg_ref[0] = lora[:, n_lora - lora_g:]
    v_ref[0] = rkv[:, 2 * d_a:].astype(BF16)
    o += n_lora

    qkv = jnp.dot(h, w_ref[0, :, o:o + 3 * d_b], preferred_element_type=F32)
    q, k = qkv[:, :d_b], qkv[:, d_b:2 * d_b]
    inv_n = 1.0 / HEAD_DIM
    qn = q * lax.rsqrt(_group_sum(q * q, gs) * inv_n + RMS_EPS) * qg_ref[...] * (HEAD_DIM ** -0.5)
    kn = k * lax.rsqrt(_group_sum(k * k, gs) * inv_n + RMS_EPS) * kg_ref[...]
    qkv_ref[0] = jnp.concatenate([qn, kn, qkv[:, 2 * d_b:]], axis=1).astype(BF16)
    o += 3 * d_b

    gates_ref[0] = _sigmoid(jnp.dot(h, w_ref[0, :, o:], preferred_element_type=F32)).astype(BF16)
    _rwkv_operand_rows((0, 1), rkv, kk, lora, w0_ref, wlbh_ref, wlbl_ref, a0_ref, albh_ref, albl_ref, ka_ref,
                       opf_ref, opr_ref, pef_ref, per_ref, tm=tm, d_a=d_a, lora_w=lora_w, lora_a=lora_a)


def _in_proj_call(xs, modsel, norm_g, w_in, k_k, q_g, k_g, gs, w0, wlb, a0, alb, k_a,
                  *, layer, n_ctx, d_a, d_b, lora_g, tm=256):
    bsz, tt, d = xs.shape
    lora_w, lora_a = wlb.shape[1], alb.shape[1]
    n_lora = 2 * lora_w + 2 * lora_a + lora_g
    n_gate = w_in.shape[2] - 3 * d_a - n_lora - 3 * d_b
    row = lambda b, t: (b, t, 0)
    const = lambda b, t: (0, 0)
    const3 = lambda b, t: (0, 0, 0)
    wlb_hi, wlb_lo = _split(wlb)
    alb_hi, alb_lo = _split(alb)
    op_shape = jax.ShapeDtypeStruct((bsz, tt, N_OPERANDS * d_a), BF16)
    op_spec = pl.BlockSpec((1, tm, N_OPERANDS * d_a), row)
    pe_shape = jax.ShapeDtypeStruct((bsz, tt // CHUNK, 8, d_a), F32)
    pe_spec = pl.BlockSpec((1, tm // CHUNK, 8, d_a), lambda b, t: (b, t, 0, 0))
    return pl.pallas_call(
        functools.partial(_in_proj_kernel, tm=tm, n_ctx=n_ctx, d_a=d_a, d_b=d_b,
                          lora_w=lora_w, lora_a=lora_a, lora_g=lora_g),
        out_shape=(jax.ShapeDtypeStruct((bsz, tt, 3 * d_a), F32),
                   jax.ShapeDtypeStruct((bsz, tt, lora_g), F32),
                   jax.ShapeDtypeStruct((bsz, tt, 3 * d_b), BF16),
                   jax.ShapeDtypeStruct((bsz, tt, n_gate), BF16),
                   op_shape, op_shape, jax.ShapeDtypeStruct((bsz, tt, d_a), BF16), pe_shape, pe_shape),
        grid=(bsz, tt // tm),
        in_specs=[pl.BlockSpec((1, tm, d), row),
                  pl.BlockSpec((1, 2, 8, d), lambda b, t: (b, 0, 0, 0)),
                  pl.BlockSpec((1, d), const),
                  pl.BlockSpec((1,) + w_in.shape[1:], lambda b, t: (layer, 0, 0)),
                  pl.BlockSpec((1, d_a), const),
                  pl.BlockSpec((1, d_b), const),
                  pl.BlockSpec((1, d_b), const),
                  pl.BlockSpec((LANE, LANE), const),
                  pl.BlockSpec((2, 1, d_a), const3),
                  pl.BlockSpec((2, lora_w, d_a), const3),
                  pl.BlockSpec((2, lora_w, d_a), const3),
                  pl.BlockSpec((2, 1, d_a), const3),
                  pl.BlockSpec((2, lora_a, d_a), const3),
                  pl.BlockSpec((2, lora_a, d_a), const3),
                  pl.BlockSpec((1, d_a), const)],
        out_specs=(pl.BlockSpec((1, tm, 3 * d_a), row),
                   pl.BlockSpec((1, tm, lora_g), row),
                   pl.BlockSpec((1, tm, 3 * d_b), row),
                   pl.BlockSpec((1, tm, n_gate), row),
                   op_spec, op_spec, pl.BlockSpec((1, tm, d_a), row), pe_spec, pe_spec),
        compiler_params=pltpu.CompilerParams(dimension_semantics=("parallel", "parallel"),
                                             vmem_limit_bytes=VMEM_LIMIT),
        name="in_proj",
    )(xs, modsel, norm_g, w_in, k_k, q_g, k_g, gs,
      w0.reshape(2, 1, d_a), wlb_hi, wlb_lo, a0.reshape(2, 1, d_a), alb_hi, alb_lo, k_a)


N_OPERANDS = 4


def _dot3_presplit(x, w_hi, w_lo):
    xh, xl = _split(x)
    return (jnp.dot(xh, w_hi, preferred_element_type=F32) + jnp.dot(xl, w_hi, preferred_element_type=F32)
            + jnp.dot(xh, w_lo, preferred_element_type=F32))


def _rwkv_operand_rows(dirs, rkv, kk, lora, w0_ref, wlbh_ref, wlbl_ref, a0_ref, albh_ref, albl_ref, ka_ref,
                       opf_ref, opr_ref, pef_ref, per_ref, *, tm, d_a, lora_w, lora_a):
    r, k = rkv[:, :d_a], rkv[:, d_a:2 * d_a]
    chunk_bits = int(math.log2(CHUNK))
    row = lax.broadcasted_iota(jnp.int32, (tm, tm), 0)
    col = lax.broadcasted_iota(jnp.int32, (tm, tm), 1)
    same = (row >> chunk_bits) == (col >> chunk_bits)
    for d in dirs:
        op_ref, pe_ref = ((opf_ref, pef_ref), (opr_ref, per_ref))[d]
        lw = lora[:, d * lora_w:(d + 1) * lora_w]
        la = lora[:, 2 * lora_w + d * lora_a:2 * lora_w + (d + 1) * lora_a]
        wl = w0_ref[d] + _dot3_presplit(jnp.tanh(lw), wlbh_ref[d], wlbl_ref[d])
        ld = -DECAY_SCALE * _sigmoid(wl)
        al = _sigmoid(a0_ref[d] + _dot3_presplit(la, albh_ref[d], albl_ref[d]))
        kd = k * (1.0 + (al - 1.0) * ka_ref[...])
        before = (row <= col) if d else (row >= col)
        tri = jnp.where(same & before, 1.0, 0.0).astype(BF16)
        ld_hi, ld_lo = _split(ld)
        cum = jnp.dot(tri, ld_hi, preferred_element_type=F32) + jnp.dot(tri, ld_lo, preferred_element_type=F32)
        p_inv = jnp.exp(-cum)
        op_ref[0] = jnp.concatenate([-kk * jnp.exp(cum - ld), r * jnp.exp(cum), kk * al * p_inv, kd * p_inv],
                                    axis=1).astype(BF16)
        for ci in range(tm // CHUNK):
            last = ci * CHUNK + (0 if d else CHUNK - 1)
            pe_ref[0, ci] = jnp.exp(jnp.broadcast_to(cum[last:last + 1, :], (8, d_a)))


def _rwkv_pair_kernel(opf_ref, opr_ref, vf_ref, vr_ref, pef_ref, per_ref, yf_ref, yr_ref, st_ref, *, d_a):
    n = HEAD_DIM
    pw = 2 * n
    pairs = d_a // pw
    c = CHUNK
    nb = opf_ref.shape[0]

    @pl.when(pl.program_id(1) == 0)
    def _():
        st_ref[...] = jnp.zeros_like(st_ref)

    chains = [(b, d, p) for b in range(nb) for d in range(2) for p in range(pairs)]
    op_refs = (opf_ref, opr_ref)
    part = lambda i: [op_refs[d][b, :, i * d_a + p * pw:i * d_a + (p + 1) * pw] for b, d, p in chains]
    v_refs = (vf_ref, vr_ref)
    pe_refs = (pef_ref, per_ref)
    v = [v_refs[d][b, :, p * pw:(p + 1) * pw] for b, d, p in chains]
    p_end = [pe_refs[d][b, 0, 0:1, p * pw:(p + 1) * pw] for b, d, p in chains]
    each = lambda f, *ls: [f(*xs) for xs in zip(*ls)]
    rows2 = lambda a, b: jnp.concatenate([a, b], axis=0)
    bf = lambda t: t.astype(BF16)

    def bd(x):
        x = bf(x)
        left = lax.broadcasted_iota(jnp.int32, x.shape, 1) < n
        zero = jnp.zeros_like(x)
        return rows2(jnp.where(left, x, zero), jnp.where(left, zero, x))

    pdot = lambda a, b: jnp.dot(bf(a), bd(b), preferred_element_type=F32)
    ar = each(rows2, part(0), part(1))
    bk = each(rows2, part(2), part(3))
    b_k = each(lambda w: rows2(bd(w[:c]), bd(w[c:])), bk)
    s = [st_ref[b, d, p] for b, d, p in chains]

    mm_all = each(lambda a, w, ss: _bdot_nt(a, rows2(w, bd(ss))), ar, b_k, s)
    row = lax.broadcasted_iota(jnp.int32, (c, 2 * pw), 0)
    col = lax.broadcasted_iota(jnp.int32, (c, 2 * pw), 1) & (c - 1)
    incl = [(row <= col) if d else (row >= col) for _, d, _ in chains]
    strict = [(row < col) if d else (row > col) for _, d, _ in chains]
    m_a = each(lambda mm, msk: jnp.where(msk, mm[:c, :2 * pw], 0.0), mm_all, strict)
    m_r = each(lambda mm, msk: jnp.where(msk, mm[c:, :2 * pw], 0.0), mm_all, incl)
    ms_a = each(lambda mm: mm[:c, 2 * pw:], mm_all)
    ms_r = each(lambda mm: mm[c:, 2 * pw:], mm_all)
    a_ab = each(lambda mm: mm[:, :pw], m_a)
    a_ak = each(lambda mm: mm[:, pw:], m_a)

    row = lax.broadcasted_iota(jnp.int32, (c, pw), 0)
    col = lax.broadcasted_iota(jnp.int32, (c, pw), 1) & (c - 1)
    eye = jnp.where(row == col, 1.0, 0.0)
    sub_bits = int(math.log2(SUB))
    blk = (row >> sub_bits) == (col >> sub_bits)
    ad = each(lambda a: jnp.where(blk, a, 0.0), a_ab)
    ao = each(lambda a, b: a - b, a_ab, ad)
    td = each(lambda a: eye + a, ad)
    x = each(pdot, ad, ad)
    for _ in range(sub_bits - 2):
        res = each(lambda xx, tt: pdot(rows2(xx, tt), xx), x, td)
        td = each(lambda tt, rr: tt + rr[c:], td, res)
        x = each(lambda rr: rr[:c], res)
    td = each(lambda tt, xx: tt + pdot(tt, xx), td, x)
    x = each(pdot, td, ao)
    w = td
    for _ in range(int(math.log2(c // SUB)) - 1):
        res = each(lambda xx, ww: jnp.dot(bf(xx), jnp.concatenate([bd(xx), bd(ww)], axis=1),
                                          preferred_element_type=F32), x, w)
        w = each(lambda ww, rr: ww + rr[:, pw:], w, res)
        x = each(lambda rr: rr[:, :pw], res)
    t_inv = each(lambda ww, xx: ww + pdot(xx, ww), w, x)

    akv = each(pdot, a_ak, v)
    u = each(lambda t, m0, m1: bf(pdot(t, m0 + m1)), t_inv, ms_a, akv)
    y = each(lambda m0, a, uu, vv: m0 + jnp.dot(bf(a), rows2(bd(uu), bd(vv)), preferred_element_type=F32),
             ms_r, m_r, u, v)
    left = lax.broadcasted_iota(jnp.int32, (n, pw), 1) < n
    upd = each(lambda uu, vv, w: _bdot_tn(rows2(uu, vv), w), u, v, bk)
    s_new = each(lambda ss, p, dd: (ss + jnp.where(left, dd[:n], dd[n:])) * p, s, p_end, upd)
    for (b, d, p), val in zip(chains, s_new):
        st_ref[b, d, p] = val
    for b in range(nb):
        yf_ref[b] = jnp.concatenate(y[2 * b * pairs:(2 * b + 1) * pairs], axis=1)
        yr_ref[b] = jnp.concatenate(y[(2 * b + 1) * pairs:(2 * b + 2) * pairs], axis=1)


def _rwkv_call(op_f, op_r, v, pe_f, pe_r, *, n_ctx):
    bsz, tt, d_a = v.shape
    n_chunks = tt // CHUNK
    nc_ctx = n_ctx // CHUNK
    heads = d_a // HEAD_DIM

    rev_chunk = lambda j: jnp.where(j < nc_ctx, nc_ctx - 1 - j, n_chunks - 1 - (j - nc_ctx))
    fwd = lambda b, j: (b, j, 0)
    rev = lambda b, j: (b, rev_chunk(j), 0)
    y_shape = jax.ShapeDtypeStruct((bsz, tt, d_a), F32)
    nb = next(m for m in (2, 1) if bsz % m == 0)
    return pl.pallas_call(
        functools.partial(_rwkv_pair_kernel, d_a=d_a),
        out_shape=(y_shape, y_shape),
        grid=(bsz // nb, n_chunks),
        in_specs=[pl.BlockSpec((nb, CHUNK, N_OPERANDS * d_a), fwd),
                  pl.BlockSpec((nb, CHUNK, N_OPERANDS * d_a), rev),
                  pl.BlockSpec((nb, CHUNK, d_a), fwd),
                  pl.BlockSpec((nb, CHUNK, d_a), rev),
                  pl.BlockSpec((nb, 1, 8, d_a), lambda b, j: (b, j, 0, 0)),
                  pl.BlockSpec((nb, 1, 8, d_a), lambda b, j: (b, rev_chunk(j), 0, 0))],
        out_specs=(pl.BlockSpec((nb, CHUNK, d_a), fwd), pl.BlockSpec((nb, CHUNK, d_a), rev)),
        scratch_shapes=[pltpu.VMEM((nb, 2, heads // 2, HEAD_DIM, 2 * HEAD_DIM), F32)],
        compiler_params=pltpu.CompilerParams(dimension_semantics=("parallel", "arbitrary"),
                                             vmem_limit_bytes=VMEM_LIMIT),
        name="rwkv7_chunk_scan",
    )(op_f, op_r, v, v, pe_f, pe_r)


def _softmax_pv(heads_parts):
    ms = [functools.reduce(jnp.maximum, [jnp.max(s, axis=-1, keepdims=True) for s, _ in parts])
          for parts in heads_parts]
    ps = [[jnp.exp(s - m) for s, _ in parts] for parts, m in zip(heads_parts, ms)]
    dens = [sum(jnp.sum(p, axis=-1, keepdims=True) for p in pp) for pp in ps]
    nums = [sum(jnp.dot(p.astype(BF16), vals, preferred_element_type=F32) for p, (_, vals) in zip(pp, parts))
            for pp, parts in zip(ps, heads_parts)]
    return [num / den for num, den in zip(nums, dens)]


def _attn_kernel(q_ref, k_ref, v_ref, bias_ref, o_ref, *, n_ctx, rows, q_off, d_b):
    n = HEAD_DIM
    heads = d_b // n
    nq_ctx = n_ctx // GRID_W
    i = pl.program_id(1) + q_off
    nb = q_ref.shape[0]
    q = [q_ref[b] for b in range(nb)]
    kc = [k_ref[b, 0:n_ctx, :] for b in range(nb)]
    vc = [v_ref[b, 0:n_ctx, :] for b in range(nb)]
    units = [(b, h, slice(h * n, (h + 1) * n)) for b in range(nb) for h in range(heads)]

    def store(outs):
        for b in range(nb):
            o_ref[b] = jnp.concatenate(outs[b * heads:(b + 1) * heads], axis=1).astype(o_ref.dtype)

    def latent():
        li = i - nq_ctx
        r0 = jnp.clip(li - WIN_H // 2, 0, rows - WIN_H)
        start = pl.multiple_of(n_ctx + r0 * GRID_W, GRID_W)
        kl = [k_ref[b, pl.ds(start, WIN_H * GRID_W), :] for b in range(nb)]
        vl = [v_ref[b, pl.ds(start, WIN_H * GRID_W), :] for b in range(nb)]
        s_l = [_bdot_nt(q[b][:, sl], kl[b][:, sl]) + bias_ref[0, h] for b, h, sl in units]
        s_c = [_bdot_nt(q[b][:, sl], kc[b][:, sl]) for b, h, sl in units]
        store(_softmax_pv([[(sa, vl[b][:, sl]), (sb, vc[b][:, sl])]
                           for sa, sb, (b, h, sl) in zip(s_l, s_c, units)]))

    def context():
        store(_softmax_pv([[(_bdot_nt(q[b][:, sl], kc[b][:, sl]), vc[b][:, sl])] for b, h, sl in units]))

    if q_off == 0:
        pl.when(i >= nq_ctx)(latent)
        pl.when(i < nq_ctx)(context)
    else:
        latent()


def _bias_table(rpb):
    w = GRID_W
    heads = rpb.shape[0]
    j = np.arange(w)
    col_start = np.clip(j - WIN_W // 2, 0, w - WIN_W)
    in_win = (j[None, :] >= col_start[:, None]) & (j[None, :] < col_start[:, None] + WIN_W)
    rpb = rpb.astype(F32)
    ext = jnp.concatenate([jnp.repeat(rpb[..., :1], w - WIN_W, axis=-1), rpb,
                           jnp.repeat(rpb[..., -1:], w - WIN_W, axis=-1)], axis=-1)
    toe = jnp.stack([ext[..., w - 1 - qc:2 * w - 1 - qc] for qc in range(w)], axis=2)
    tz = jnp.where(in_win, toe, NEG_BIG)
    bt = jnp.stack([tz[:, v:v + WIN_H] for v in range(WIN_H)], axis=0)
    return bt.transpose(0, 1, 3, 2, 4).reshape(WIN_H, heads, w, WIN_H * w)


def _attn_call(qkv, bias_tab, *, n_ctx, with_ctx):
    bsz, tt, d3 = qkv.shape
    d_b = d3 // 3
    heads = d_b // HEAD_DIM
    rows = (tt - n_ctx) // GRID_W
    nq_ctx = n_ctx // GRID_W
    q_off = 0 if with_ctx else nq_ctx
    n_q = tt // GRID_W - q_off
    nb = next(m for m in (4, 2, 1) if bsz % m == 0)

    def variant(b, i):
        li = jnp.maximum(i + q_off - nq_ctx, 0)
        return (jnp.clip(li - WIN_H // 2, 0, rows - WIN_H) - li + WIN_H - 1, 0, 0, 0)

    return pl.pallas_call(
        functools.partial(_attn_kernel, n_ctx=n_ctx, rows=rows, q_off=q_off, d_b=d_b),
        out_shape=jax.ShapeDtypeStruct((bsz, n_q * GRID_W, d_b), BF16),
        grid=(bsz // nb, n_q),
        in_specs=[pl.BlockSpec((nb, GRID_W, d_b), lambda b, i: (b, i + q_off, 0)),
                  pl.BlockSpec((nb, tt, d_b), lambda b, i: (b, 0, 1)),
                  pl.BlockSpec((nb, tt, d_b), lambda b, i: (b, 0, 2)),
                  pl.BlockSpec((1, heads, GRID_W, WIN_H * GRID_W), variant)],
        out_specs=pl.BlockSpec((nb, GRID_W, d_b), lambda b, i: (b, i, 0)),
        compiler_params=pltpu.CompilerParams(dimension_semantics=("parallel", "arbitrary"),
                                             vmem_limit_bytes=VMEM_LIMIT),
        name="neighbourhood_attention",
    )(qkv, qkv, qkv, bias_tab)


def _merge_kernel(x_ref, mod_ref, y0_ref, y1_ref, rkv_ref, lg_ref, yb_ref, gates_ref,
                  rk_ref, lng_ref, lnb_ref, glb_ref, pa_ref, pb_ref, wo_ref, gs_ref, o_ref,
                  *, tm, n_ctx, t_off, d_a):
    t = pl.program_id(1) + t_off
    is_ctx = _ctx_rows(t, tm, n_ctx)
    gs = gs_ref[...]
    inv_n = 1.0 / HEAD_DIM
    y = y0_ref[0] + y1_ref[0]
    mu = _group_sum2(y, gs) * inv_n
    yc = y - mu
    var = _group_sum2(yc * yc, gs) * inv_n
    yn = yc * lax.rsqrt(var + GN_EPS) * lng_ref[...] + lnb_ref[...]
    rkv = rkv_ref[0]
    r, k, v = rkv[:, :d_a], rkv[:, d_a:2 * d_a], rkv[:, 2 * d_a:]
    bonus = _group_sum2(r * k * rk_ref[...], gs) * v
    g = _bdot(_sigmoid(lg_ref[0]), glb_ref[...])
    ya = (yn + bonus) * g
    gates = gates_ref[0].astype(F32)
    dm = gates.shape[1] // 2
    merged = gates[:, :dm] * _bdot(ya, pa_ref[...]) + gates[:, dm:] * _bdot(yb_ref[0], pb_ref[...])
    o_ref[0] = x_ref[0] + _mod_row(mod_ref, 2, is_ctx) * _bdot(merged, wo_ref[...])


def _merge_call(xs, modsel, y_f, y_r, rkv, lg, yb, gates, r_k, ln_g, ln_b, glb, pa, pb, wo, gs,
                *, n_ctx, with_ctx, tm=256):
    bsz, tt, d = xs.shape
    d_a = rkv.shape[2] // 3
    t_off = 0 if with_ctx else n_ctx // tm
    n_t = tt // tm - t_off
    yb_off = 0 if with_ctx else -t_off
    row = lambda b, t: (b, t + t_off, 0)
    const = lambda b, t: (0, 0)
    return pl.pallas_call(
        functools.partial(_merge_kernel, tm=tm, n_ctx=n_ctx, t_off=t_off, d_a=d_a),
        out_shape=jax.ShapeDtypeStruct((bsz, n_t * tm, d), F32),
        grid=(bsz, n_t),
        in_specs=[pl.BlockSpec((1, tm, d), row),
                  pl.BlockSpec((1, 2, 8, d), lambda b, t: (b, 0, 0, 0)),
                  pl.BlockSpec((1, tm, d_a), row),
                  pl.BlockSpec((1, tm, d_a), row),
                  pl.BlockSpec((1, tm, 3 * d_a), row),
                  pl.BlockSpec((1, tm, lg.shape[2]), row),
                  pl.BlockSpec((1, tm, yb.shape[2]), lambda b, t: (b, t + t_off + yb_off, 0)),
                  pl.BlockSpec((1, tm, gates.shape[2]), row),
                  pl.BlockSpec((1, d_a), const),
                  pl.BlockSpec((1, d_a), const),
                  pl.BlockSpec((1, d_a), const),
                  pl.BlockSpec(glb.shape, const),
                  pl.BlockSpec(pa.shape, const),
                  pl.BlockSpec(pb.shape, const),
                  pl.BlockSpec(wo.shape, const),
                  pl.BlockSpec((LANE, LANE), const)],
        out_specs=pl.BlockSpec((1, tm, d), lambda b, t: (b, t, 0)),
        compiler_params=pltpu.CompilerParams(dimension_semantics=("parallel", "parallel"),
                                             vmem_limit_bytes=VMEM_LIMIT),
        name="branch_merge",
    )(xs, modsel, y_f, y_r, rkv, lg, yb, gates, r_k, ln_g, ln_b, glb, pa, pb, wo, gs)


def _route(sel, scores, n_experts):
    per = n_experts // N_GROUPS
    in_top = []
    for e in range(n_experts):
        g0 = (e // per) * per
        rank = 0.0
        for o in range(g0, g0 + per):
            if o == e:
                continue
            ahead = (sel[o] >= sel[e]) if o < e else (sel[o] > sel[e])
            rank = rank + jnp.where(ahead, 1.0, 0.0)
        in_top.append(rank < TOP_K)
    grp = [sum(jnp.where(in_top[e], sel[e], 0.0) for e in range(g * per, (g + 1) * per))
           for g in range(N_GROUPS)]
    best = jnp.zeros_like(grp[0], dtype=jnp.int32)
    best_s = grp[0]
    for g in range(1, N_GROUPS):
        better = grp[g] > best_s
        best = jnp.where(better, g, best)
        best_s = jnp.where(better, grp[g], best_s)
    chosen = [in_top[e] & (best == e // per) for e in range(n_experts)]
    den = sum(jnp.where(chosen[e], scores[e], 0.0) for e in range(n_experts))
    return [jnp.where(chosen[e], scores[e] / den, 0.0) for e in range(n_experts)], best


MOE_EXPERTS_PER_STEP = 2
MOE_BLK = 128
POS_COL = 16


def _moe_route_kernel(x_ref, mod_ref, g_ref, rw_ref, rb_ref, hs_ref, gs_ref, tab_ref, cnt_ref,
                      *, tm, ts, n_ctx, n_experts):
    t = pl.program_id(1)
    is_ctx = _ctx_rows(t, tm, n_ctx)
    h = _rms_mod(x_ref[0], g_ref[...], _mod_row(mod_ref, 3, is_ctx), _mod_row(mod_ref, 4, is_ctx))
    hh, hl = _split(h)
    wh, wl = _split(rw_ref[...])
    nt = lambda a, b: lax.dot_general(a, b, (((1,), (1,)), ((), ())), preferred_element_type=F32)
    logits = nt(wh, hh) + nt(wh, hl) + nt(wl, hh)
    scores = _sigmoid(logits)
    sel = scores + rb_ref[...]
    gates, best = _route([sel[i:i + 1, :] for i in range(n_experts)],
                         [scores[i:i + 1, :] for i in range(n_experts)], n_experts)

    in_grp = [jnp.where(best == g, 1.0, 0.0) for g in range(N_GROUPS)]
    grp8 = jnp.concatenate(in_grp + [jnp.zeros((8 - N_GROUPS, tm), F32)], axis=0).astype(BF16)
    row = lax.broadcasted_iota(jnp.int32, (tm, tm), 0)
    col = lax.broadcasted_iota(jnp.int32, (tm, tm), 1)
    upper = jnp.where(row <= col, 1.0, 0.0).astype(BF16)
    run = jnp.dot(grp8, upper, preferred_element_type=F32)
    pos = jnp.zeros((1, tm), F32)
    seg = jnp.zeros((1, 1), F32)
    cnts = []
    for g in range(N_GROUPS):
        cnt = run[g:g + 1, tm - 1:tm]
        cnts.append(cnt)
        pos = pos + in_grp[g] * (seg + run[g:g + 1, :] - 1.0)
        seg = seg + jnp.floor((cnt + (MOE_BLK - 1)) * (1.0 / MOE_BLK)) * MOE_BLK
    cnt_ref[0, 0] = jnp.concatenate([jnp.broadcast_to(c, (1, LANE)) for c in cnts]
                                    + [jnp.zeros((8 - N_GROUPS, LANE), F32)], axis=0)
    tab = jnp.concatenate(gates + [pos, jnp.zeros((LANE - n_experts - 1, tm), F32)], axis=0).T
    tab_ref[0] = tab
    sel = jnp.where(lax.broadcasted_iota(jnp.int32, (ts, tm), 0) == pos.astype(jnp.int32), 1.0, 0.0).astype(BF16)
    hs_ref[0, 0] = jnp.dot(sel, h.astype(BF16), preferred_element_type=F32).astype(BF16)
    t1 = tab.astype(BF16)
    r1 = tab - t1.astype(F32)
    t2 = r1.astype(BF16)
    t3 = (r1 - t2.astype(F32)).astype(BF16)
    gs_ref[0, 0] = (jnp.dot(sel, t1, preferred_element_type=F32) + jnp.dot(sel, t2, preferred_element_type=F32)
                    + jnp.dot(sel, t3, preferred_element_type=F32))


def _moe_expert_kernel(lo_ref, hi_ref, x_ref, mod_ref, hs_ref, gs_ref, tab_ref, w1_ref, w3_ref, w2_ref, o_ref,
                       acc_ref, *, tm, ts, n_ctx, n_experts):
    b = pl.program_id(0)
    t = pl.program_id(1)
    e = pl.program_id(2)
    n_t = pl.num_programs(1)

    @pl.when(e == 0)
    def _():
        acc_ref[...] = jnp.zeros_like(acc_ref)

    per_step = w1_ref.shape[1]
    e_first = e * per_step
    seg = (b * n_t + t) * N_GROUPS + e_first // (n_experts // N_GROUPS)
    lane = lax.broadcasted_iota(jnp.int32, (MOE_BLK, LANE), 1)

    def block(j, carry):
        rows = pl.ds(pl.multiple_of(j * MOE_BLK, MOE_BLK), MOE_BLK)
        hb = hs_ref[0, 0, rows, :]
        gs = gs_ref[0, 0, rows, :]
        hids = []
        for i in range(per_step):
            a = jnp.dot(hb, w1_ref[0, i], preferred_element_type=F32)
            bb = jnp.dot(hb, w3_ref[0, i], preferred_element_type=F32)
            gcol = jnp.sum(jnp.where(lane == e_first + i, gs, 0.0), axis=-1, keepdims=True)
            hids.append((a * _sigmoid(a) * bb * gcol).astype(BF16))
        w2 = w2_ref[0].reshape(per_step * w2_ref.shape[2], w2_ref.shape[3])
        acc_ref[rows, :] += jnp.dot(jnp.concatenate(hids, axis=1), w2, preferred_element_type=F32)
        return carry

    lax.fori_loop(lo_ref[seg], hi_ref[seg], block, 0)

    @pl.when(e == pl.num_programs(2) - 1)
    def _():
        is_ctx = _ctx_rows(t, tm, n_ctx)
        pos = tab_ref[0][:, POS_COL:POS_COL + 1].astype(jnp.int32)
        back = jnp.where(lax.broadcasted_iota(jnp.int32, (tm, ts), 1) == pos, 1.0, 0.0).astype(BF16)
        y = jnp.dot(back, acc_ref[...].astype(BF16), preferred_element_type=F32)
        o_ref[0] = x_ref[0] + _mod_row(mod_ref, 5, is_ctx) * y


def _moe_call(xs, modsel, norm_g, router_wt, router_b, w1, w3, w2, *, layer, n_ctx, tm):
    bsz, tt, d = xs.shape
    n_experts = router_wt.shape[0]
    n_t = tt // tm
    ts = tm + N_GROUPS * MOE_BLK
    row2 = lambda b, t: (b, t, 0)
    const2 = lambda b, t: (0, 0)
    tile2 = lambda b, t: (b, t, 0, 0)
    hs, gs, tab, cnt = pl.pallas_call(
        functools.partial(_moe_route_kernel, tm=tm, ts=ts, n_ctx=n_ctx, n_experts=n_experts),
        out_shape=(jax.ShapeDtypeStruct((bsz, n_t, ts, d), BF16),
                   jax.ShapeDtypeStruct((bsz, n_t, ts, LANE), F32),
                   jax.ShapeDtypeStruct((bsz, tt, LANE), F32),
                   jax.ShapeDtypeStruct((bsz, n_t, 8, LANE), F32)),
        grid=(bsz, n_t),
        in_specs=[pl.BlockSpec((1, tm, d), row2),
                  pl.BlockSpec((1, 2, 8, d), lambda b, t: (b, 0, 0, 0)),
                  pl.BlockSpec((1, d), const2),
                  pl.BlockSpec(router_wt.shape, const2),
                  pl.BlockSpec((n_experts, 1), const2)],
        out_specs=(pl.BlockSpec((1, 1, ts, d), tile2),
                   pl.BlockSpec((1, 1, ts, LANE), tile2),
                   pl.BlockSpec((1, tm, LANE), row2),
                   pl.BlockSpec((1, 1, 8, LANE), tile2)),
        compiler_params=pltpu.CompilerParams(dimension_semantics=("parallel", "parallel"),
                                             vmem_limit_bytes=VMEM_LIMIT),
        name="moe_route",
    )(xs, modsel, norm_g, router_wt, router_b)

    n_blk = (cnt[:, :, :N_GROUPS, 0].astype(jnp.int32) + (MOE_BLK - 1)) // MOE_BLK
    hi = jnp.cumsum(n_blk, axis=-1)
    lo = hi - n_blk

    row = lambda b, t, e, lo_r, hi_r: (b, t, 0)
    wexp = lambda b, t, e, lo_r, hi_r: (layer, e, 0, 0)
    tile4 = lambda b, t, e, lo_r, hi_r: (b, t, 0, 0)
    return pl.pallas_call(
        functools.partial(_moe_expert_kernel, tm=tm, ts=ts, n_ctx=n_ctx, n_experts=n_experts),
        out_shape=jax.ShapeDtypeStruct((bsz, tt, d), F32),
        grid_spec=pltpu.PrefetchScalarGridSpec(
            num_scalar_prefetch=2,
            grid=(bsz, n_t, n_experts // MOE_EXPERTS_PER_STEP),
            in_specs=[pl.BlockSpec((1, tm, d), row),
                      pl.BlockSpec((1, 2, 8, d), lambda b, t, e, lo_r, hi_r: (b, 0, 0, 0)),
                      pl.BlockSpec((1, 1, ts, d), tile4),
                      pl.BlockSpec((1, 1, ts, LANE), tile4),
                      pl.BlockSpec((1, tm, LANE), row),
                      pl.BlockSpec((1, MOE_EXPERTS_PER_STEP) + w1.shape[2:], wexp),
                      pl.BlockSpec((1, MOE_EXPERTS_PER_STEP) + w3.shape[2:], wexp),
                      pl.BlockSpec((1, MOE_EXPERTS_PER_STEP) + w2.shape[2:], wexp)],
            out_specs=pl.BlockSpec((1, tm, d), row),
            scratch_shapes=[pltpu.VMEM((ts, d), F32)]),
        compiler_params=pltpu.CompilerParams(
            dimension_semantics=("parallel", "parallel", "arbitrary"),
            vmem_limit_bytes=VMEM_LIMIT),
        name="moe_experts",
    )(lo.reshape(-1), hi.reshape(-1), xs, modsel, hs, gs, tab, w1, w3, w2)


def kernel(x, c, ctx, c_ctx, mod_w, mod_b, norm1_g, norm2_g, w_in, rw_w0, rw_w_lora_b, rw_a0, rw_a_lora_b,
           rw_g_lora_b, rw_k_k, rw_k_a, rw_r_k, rw_ln_g, rw_ln_b, na_q_g, na_k_g, na_rpb, proj_a, proj_b,
           w_out, router_w, router_bias, moe_w1, moe_w3, moe_w2):
    bsz, seq, d = x.shape
    n_ctx = ctx.shape[1]
    depth = mod_w.shape[0]
    d_a = rw_w0.shape[2]
    d_b = proj_b.shape[1]
    lora_g = rw_g_lora_b.shape[1]
    heads_b = d_b // HEAD_DIM
    assert seq % (GRID_W * WIN_H) == 0 and n_ctx % 256 == 0 and bsz + 1 <= 16

    cs = jnp.zeros((16, d), F32).at[:bsz].set(c).at[bsz].set(c_ctx)
    mod = _mod_call(cs, mod_w, mod_b).reshape(depth, 16, 6, d)

    lane = jnp.arange(LANE)
    gs = (lane[:, None] // HEAD_DIM == lane[None, :] // HEAD_DIM).astype(BF16)
    router_wt = router_w.T
    router_b = router_bias.reshape(-1, 1)

    w_in_b, w1_b, w3_b, w2_b = (_to_bf16(w) for w in (w_in, moe_w1, moe_w3, moe_w2))
    xs = jnp.concatenate([ctx, x], axis=1)
    for l in range(depth):
        last = l == depth - 1
        m_c = jnp.broadcast_to(mod[l, bsz][None], (bsz, 6, d))
        modsel = jnp.pad(jnp.stack([m_c, mod[l, :bsz]], axis=1), ((0, 0), (0, 0), (0, 2), (0, 0)))

        rkv, lg, qkv, gates, op_f, op_r, v_a, pe_f, pe_r = _in_proj_call(
            xs, modsel, norm1_g[l][None], w_in_b, rw_k_k[l][None],
            jnp.tile(na_q_g[l], heads_b)[None], jnp.tile(na_k_g[l], heads_b)[None], gs,
            rw_w0[l], rw_w_lora_b[l], rw_a0[l], rw_a_lora_b[l], rw_k_a[l][None],
            layer=l, n_ctx=n_ctx, d_a=d_a, d_b=d_b, lora_g=lora_g)
        y_f, y_r = _rwkv_call(op_f, op_r, v_a, pe_f, pe_r, n_ctx=n_ctx)
        yb = _attn_call(qkv, _bias_table(na_rpb[l]), n_ctx=n_ctx, with_ctx=not last)
        xs = _merge_call(xs, modsel, y_f, y_r, rkv, lg, yb, gates, rw_r_k[l].reshape(1, d_a),
                         rw_ln_g[l][None], rw_ln_b[l][None], rw_g_lora_b[l].astype(BF16),
                         proj_a[l].astype(BF16), proj_b[l].astype(BF16), w_out[l].astype(BF16), gs,
                         n_ctx=n_ctx, with_ctx=not last)
        moe_ctx = 0 if last else n_ctx
        tm = next(m for m in (1024, 768, 512, 256) if xs.shape[1] % m == 0)
        xs = _moe_call(xs, modsel, norm2_g[l][None], router_wt, router_b, w1_b, w3_b, w2_b,
                       layer=l, n_ctx=moe_ctx, tm=tm)
    return xs
```

```python
import functools
import math

import jax
import jax.numpy as jnp
import numpy as np
from jax import lax
from jax.experimental import pallas as pl
from jax.experimental.pallas import tpu as pltpu

F32 = jnp.float32
BF16 = jnp.bfloat16

HEAD_DIM = 64
GRID_W = 64
WIN_H = 8
WIN_W = 16
N_GROUPS = 4
TOP_K = 2
RMS_EPS = 1e-6
GN_EPS = 64e-5
LANE = 128
CHUNK = 64
SUB = 16
NEG_BIG = -1e30
DECAY_SCALE = math.exp(-0.5)
VMEM_LIMIT = 56 * 1024 * 1024


def _bdot(a, b):
    return jnp.dot(a.astype(BF16), b.astype(BF16), preferred_element_type=F32)


def _bdot_nt(a, b):
    return lax.dot_general(a.astype(BF16), b.astype(BF16), (((1,), (1,)), ((), ())),
                           preferred_element_type=F32)


def _bdot_tn(a, b):
    return lax.dot_general(a.astype(BF16), b.astype(BF16), (((0,), (0,)), ((), ())),
                           preferred_element_type=F32)


def _split(x):
    hi = x.astype(BF16)
    lo = (x - hi.astype(F32)).astype(BF16)
    return hi, lo


def _dot3(a, b):
    ah, al = _split(a)
    bh, bl = _split(b)
    return (jnp.dot(ah, bh, preferred_element_type=F32)
            + jnp.dot(al, bh, preferred_element_type=F32)
            + jnp.dot(ah, bl, preferred_element_type=F32))


def _sigmoid(x):
    return 1.0 / (1.0 + jnp.exp(-x))


def _group_sum(x, g128):
    parts = [_bdot(x[:, j * LANE:(j + 1) * LANE], g128) for j in range(x.shape[1] // LANE)]
    return jnp.concatenate(parts, axis=1)


def _group_sum2(x, g128):
    hi, lo = _split(x)
    parts = []
    for j in range(x.shape[1] // LANE):
        sl = slice(j * LANE, (j + 1) * LANE)
        parts.append(jnp.dot(hi[:, sl], g128, preferred_element_type=F32)
                     + jnp.dot(lo[:, sl], g128, preferred_element_type=F32))
    return jnp.concatenate(parts, axis=1)


def _mod_row(mod_ref, idx, is_ctx):
    mx = mod_ref[0, 1, idx:idx + 1, :]
    if is_ctx is None:
        return mx
    return jnp.where(is_ctx, mod_ref[0, 0, idx:idx + 1, :], mx)


def _ctx_rows(tile, tm, n_ctx):
    if n_ctx == 0:
        return None
    rows = tile * tm + lax.broadcasted_iota(jnp.int32, (tm, 1), 0)
    return rows < n_ctx


def _token_sources(src, tm, n_ctx, t_off):
    assert n_ctx % tm == 0
    n_ct = n_ctx // tm
    if isinstance(src, tuple):
        ctx, x = src
        d = x.shape[2]
        lat = pl.BlockSpec((1, tm, d), lambda b, t: (b, jnp.maximum(t + t_off - n_ct, 0), 0))
    else:
        ctx = x = src
        d = x.shape[2]
        lat = pl.BlockSpec((1, tm, d), lambda b, t: (b, t + t_off, 0))
    con = pl.BlockSpec((1, tm, d), lambda b, t: (b, jnp.minimum(t + t_off, n_ct - 1), 0))
    return (ctx, x), [con, lat]


def _token_tile(ctx_ref, lat_ref, tile, tm, n_ctx):
    return jnp.where(tile * tm < n_ctx, ctx_ref[0], lat_ref[0])


def _rms_mod(x, gain, shift, scale):
    xn = x * lax.rsqrt(jnp.mean(x * x, axis=-1, keepdims=True) + RMS_EPS) * gain
    return xn * (1.0 + scale) + shift


CAST_BLOCK_BYTES = 8 * 1024 * 1024


def _cast_kernel(w_ref, o_ref):
    o_ref[...] = w_ref[...].astype(o_ref.dtype)


def _to_bf16(w):
    c = w.shape[-1]
    w2 = w.reshape(-1, c)
    rows = w2.shape[0]
    rb = max(16, min(rows, CAST_BLOCK_BYTES // (c * 4)) // 16 * 16)
    while rows % rb:
        rb -= 16
    out = pl.pallas_call(
        _cast_kernel,
        out_shape=jax.ShapeDtypeStruct(w2.shape, BF16),
        grid=(rows // rb,),
        in_specs=[pl.BlockSpec((rb, c), lambda i: (i, 0))],
        out_specs=pl.BlockSpec((rb, c), lambda i: (i, 0)),
        compiler_params=pltpu.CompilerParams(dimension_semantics=("parallel",), vmem_limit_bytes=VMEM_LIMIT),
        name="to_bf16",
    )(w2)
    return out.reshape(w.shape)


def _mod_kernel(c_ref, w_ref, b_ref, o_ref):
    c = c_ref[...]
    o_ref[0] = _dot3(c * _sigmoid(c), w_ref[0]) + b_ref[0]


def _mod_call(cs, mod_w, mod_b):
    depth, d, n = mod_w.shape
    tn = n // 4
    return pl.pallas_call(
        _mod_kernel,
        out_shape=jax.ShapeDtypeStruct((depth, cs.shape[0], n), F32),
        grid=(depth, n // tn),
        in_specs=[pl.BlockSpec(cs.shape, lambda l, j: (0, 0)),
                  pl.BlockSpec((1, d, tn), lambda l, j: (l, 0, j)),
                  pl.BlockSpec((1, 1, tn), lambda l, j: (l, 0, j))],
        out_specs=pl.BlockSpec((1, cs.shape[0], tn), lambda l, j: (l, 0, j)),
        compiler_params=pltpu.CompilerParams(dimension_semantics=("parallel", "parallel"),
                                             vmem_limit_bytes=VMEM_LIMIT),
        name="adaln_vectors",
    )(cs, mod_w, mod_b.reshape(depth, 1, n))


def _in_proj_kernel(xc_ref, x_ref, mod_ref, g_ref, w_ref, kkg_ref, qg_ref, kg_ref, gs_ref,
                    w0_ref, wlbh_ref, wlbl_ref, a0_ref, albh_ref, albl_ref, ka_ref,
                    rkv_ref, lg_ref, qkv_ref, gates_ref, opf_ref, opr_ref, v_ref, pef_ref, per_ref,
                    *, tm, n_ctx, d_a, d_b, lora_w, lora_a, lora_g):
    t = pl.program_id(1)
    is_ctx = _ctx_rows(t, tm, n_ctx)
    h = _rms_mod(_token_tile(xc_ref, x_ref, t, tm, n_ctx), g_ref[...],
                 _mod_row(mod_ref, 0, is_ctx), _mod_row(mod_ref, 1, is_ctx))
    h = h.astype(BF16)
    gs = gs_ref[...]
    n_lora = 2 * lora_w + 2 * lora_a + lora_g

    o = 0
    rkv = jnp.dot(h, w_ref[0, :, o:o + 3 * d_a], preferred_element_type=F32)
    rkv_ref[0] = rkv
    kkv = rkv[:, d_a:2 * d_a] * kkg_ref[...]
    ss = _group_sum2(kkv * kkv, gs)
    kk = kkv * lax.rsqrt(jnp.maximum(ss, 1e-12))
    o += 3 * d_a

    lora = jnp.dot(h, w_ref[0, :, o:o + n_lora], preferred_element_type=F32)
    lg_ref[0] = lora[:, n_lora - lora_g:]
    v_ref[0] = rkv[:, 2 * d_a:].astype(BF16)
    o += n_lora

    qkv = jnp.dot(h, w_ref[0, :, o:o + 3 * d_b], preferred_element_type=F32)
    q, k = qkv[:, :d_b], qkv[:, d_b:2 * d_b]
    inv_n = 1.0 / HEAD_DIM
    qn = q * lax.rsqrt(_group_sum(q * q, gs) * inv_n + RMS_EPS) * qg_ref[...] * (HEAD_DIM ** -0.5)
    kn = k * lax.rsqrt(_group_sum(k * k, gs) * inv_n + RMS_EPS) * kg_ref[...]
    qkv_ref[0] = jnp.concatenate([qn, kn, qkv[:, 2 * d_b:]], axis=1).astype(BF16)
    o += 3 * d_b

    gates_ref[0] = _sigmoid(jnp.dot(h, w_ref[0, :, o:], preferred_element_type=F32)).astype(BF16)
    _rwkv_operand_rows((0, 1), rkv, kk, lora, w0_ref, wlbh_ref, wlbl_ref, a0_ref, albh_ref, albl_ref, ka_ref,
                       opf_ref, opr_ref, pef_ref, per_ref, tm=tm, d_a=d_a, lora_w=lora_w, lora_a=lora_a)


def _in_proj_call(xs, modsel, norm_g, w_in, k_k, q_g, k_g, gs, w0, wlb, a0, alb, k_a,
                  *, layer, n_ctx, d_a, d_b, lora_g, tm=256):
    tok_arrays, tok_specs = _token_sources(xs, tm, n_ctx, 0)
    bsz, d = tok_arrays[1].shape[0], tok_arrays[1].shape[2]
    tt = sum(a.shape[1] for a in xs) if isinstance(xs, tuple) else xs.shape[1]
    lora_w, lora_a = wlb.shape[1], alb.shape[1]
    n_lora = 2 * lora_w + 2 * lora_a + lora_g
    n_gate = w_in.shape[2] - 3 * d_a - n_lora - 3 * d_b
    row = lambda b, t: (b, t, 0)
    const = lambda b, t: (0, 0)
    const3 = lambda b, t: (0, 0, 0)
    wlb_hi, wlb_lo = _split(wlb)
    alb_hi, alb_lo = _split(alb)
    op_shape = jax.ShapeDtypeStruct((bsz, tt, N_OPERANDS * d_a), BF16)
    op_spec = pl.BlockSpec((1, tm, N_OPERANDS * d_a), row)
    pe_shape = jax.ShapeDtypeStruct((bsz, tt // CHUNK, 8, d_a), F32)
    pe_spec = pl.BlockSpec((1, tm // CHUNK, 8, d_a), lambda b, t: (b, t, 0, 0))
    return pl.pallas_call(
        functools.partial(_in_proj_kernel, tm=tm, n_ctx=n_ctx, d_a=d_a, d_b=d_b,
                          lora_w=lora_w, lora_a=lora_a, lora_g=lora_g),
        out_shape=(jax.ShapeDtypeStruct((bsz, tt, 3 * d_a), F32),
                   jax.ShapeDtypeStruct((bsz, tt, lora_g), F32),
                   jax.ShapeDtypeStruct((bsz, tt, 3 * d_b), BF16),
                   jax.ShapeDtypeStruct((bsz, tt, n_gate), BF16),
                   op_shape, op_shape, jax.ShapeDtypeStruct((bsz, tt, d_a), BF16), pe_shape, pe_shape),
        grid=(bsz, tt // tm),
        in_specs=tok_specs + [
                  pl.BlockSpec((1, 2, 8, d), lambda b, t: (b, 0, 0, 0)),
                  pl.BlockSpec((1, d), const),
                  pl.BlockSpec((1,) + w_in.shape[1:], lambda b, t: (layer, 0, 0)),
                  pl.BlockSpec((1, d_a), const),
                  pl.BlockSpec((1, d_b), const),
                  pl.BlockSpec((1, d_b), const),
                  pl.BlockSpec((LANE, LANE), const),
                  pl.BlockSpec((2, 1, d_a), const3),
                  pl.BlockSpec((2, lora_w, d_a), const3),
                  pl.BlockSpec((2, lora_w, d_a), const3),
                  pl.BlockSpec((2, 1, d_a), const3),
                  pl.BlockSpec((2, lora_a, d_a), const3),
                  pl.BlockSpec((2, lora_a, d_a), const3),
                  pl.BlockSpec((1, d_a), const)],
        out_specs=(pl.BlockSpec((1, tm, 3 * d_a), row),
                   pl.BlockSpec((1, tm, lora_g), row),
                   pl.BlockSpec((1, tm, 3 * d_b), row),
                   pl.BlockSpec((1, tm, n_gate), row),
                   op_spec, op_spec, pl.BlockSpec((1, tm, d_a), row), pe_spec, pe_spec),
        compiler_params=pltpu.CompilerParams(dimension_semantics=("parallel", "parallel"),
                                             vmem_limit_bytes=VMEM_LIMIT),
        name="in_proj",
    )(*tok_arrays, modsel, norm_g, w_in, k_k, q_g, k_g, gs,
      w0.reshape(2, 1, d_a), wlb_hi, wlb_lo, a0.reshape(2, 1, d_a), alb_hi, alb_lo, k_a)


N_OPERANDS = 4


def _dot3_presplit(x, w_hi, w_lo):
    xh, xl = _split(x)
    return (jnp.dot(xh, w_hi, preferred_element_type=F32) + jnp.dot(xl, w_hi, preferred_element_type=F32)
            + jnp.dot(xh, w_lo, preferred_element_type=F32))


def _rwkv_operand_rows(dirs, rkv, kk, lora, w0_ref, wlbh_ref, wlbl_ref, a0_ref, albh_ref, albl_ref, ka_ref,
                       opf_ref, opr_ref, pef_ref, per_ref, *, tm, d_a, lora_w, lora_a):
    r, k = rkv[:, :d_a], rkv[:, d_a:2 * d_a]
    chunk_bits = int(math.log2(CHUNK))
    row = lax.broadcasted_iota(jnp.int32, (tm, tm), 0)
    col = lax.broadcasted_iota(jnp.int32, (tm, tm), 1)
    same = (row >> chunk_bits) == (col >> chunk_bits)
    for d in dirs:
        op_ref, pe_ref = ((opf_ref, pef_ref), (opr_ref, per_ref))[d]
        lw = lora[:, d * lora_w:(d + 1) * lora_w]
        la = lora[:, 2 * lora_w + d * lora_a:2 * lora_w + (d + 1) * lora_a]
        wl = w0_ref[d] + _dot3_presplit(jnp.tanh(lw), wlbh_ref[d], wlbl_ref[d])
        ld = -DECAY_SCALE * _sigmoid(wl)
        al = _sigmoid(a0_ref[d] + _dot3_presplit(la, albh_ref[d], albl_ref[d]))
        kd = k * (1.0 + (al - 1.0) * ka_ref[...])
        before = (row <= col) if d else (row >= col)
        tri = jnp.where(same & before, 1.0, 0.0).astype(BF16)
        ld_hi, ld_lo = _split(ld)
        cum = jnp.dot(tri, ld_hi, preferred_element_type=F32) + jnp.dot(tri, ld_lo, preferred_element_type=F32)
        p_inv = jnp.exp(-cum)
        op_ref[0] = jnp.concatenate([-kk * jnp.exp(cum - ld), r * jnp.exp(cum), kk * al * p_inv, kd * p_inv],
                                    axis=1).astype(BF16)
        for ci in range(tm // CHUNK):
            last = ci * CHUNK + (0 if d else CHUNK - 1)
            pe_ref[0, ci] = jnp.exp(jnp.broadcast_to(cum[last:last + 1, :], (8, d_a)))


def _rwkv_pair_kernel(opf_ref, opr_ref, vf_ref, vr_ref, pef_ref, per_ref, yf_ref, yr_ref, st_ref, *, d_a):
    n = HEAD_DIM
    pw = 2 * n
    pairs = d_a // pw
    c = CHUNK
    nb = opf_ref.shape[0]

    @pl.when(pl.program_id(1) == 0)
    def _():
        st_ref[...] = jnp.zeros_like(st_ref)

    chains = [(b, d, p) for b in range(nb) for d in range(2) for p in range(pairs)]
    op_refs = (opf_ref, opr_ref)
    part = lambda i: [op_refs[d][b, :, i * d_a + p * pw:i * d_a + (p + 1) * pw] for b, d, p in chains]
    v_refs = (vf_ref, vr_ref)
    pe_refs = (pef_ref, per_ref)
    v = [v_refs[d][b, :, p * pw:(p + 1) * pw] for b, d, p in chains]
    p_end = [pe_refs[d][b, 0, 0:1, p * pw:(p + 1) * pw] for b, d, p in chains]
    each = lambda f, *ls: [f(*xs) for xs in zip(*ls)]
    rows2 = lambda a, b: jnp.concatenate([a, b], axis=0)
    bf = lambda t: t.astype(BF16)

    def bd(x):
        x = bf(x)
        left = lax.broadcasted_iota(jnp.int32, x.shape, 1) < n
        zero = jnp.zeros_like(x)
        return rows2(jnp.where(left, x, zero), jnp.where(left, zero, x))

    pdot = lambda a, b: jnp.dot(bf(a), bd(b), preferred_element_type=F32)
    ar = each(rows2, part(0), part(1))
    bk = each(rows2, part(2), part(3))
    b_k = each(lambda w: rows2(bd(w[:c]), bd(w[c:])), bk)
    s = [st_ref[b, d, p] for b, d, p in chains]

    mm_all = each(lambda a, w, ss: _bdot_nt(a, rows2(w, bd(ss))), ar, b_k, s)
    row = lax.broadcasted_iota(jnp.int32, (c, 2 * pw), 0)
    col = lax.broadcasted_iota(jnp.int32, (c, 2 * pw), 1) & (c - 1)
    incl = [(row <= col) if d else (row >= col) for _, d, _ in chains]
    strict = [(row < col) if d else (row > col) for _, d, _ in chains]
    m_a = each(lambda mm, msk: jnp.where(msk, mm[:c, :2 * pw], 0.0), mm_all, strict)
    m_r = each(lambda mm, msk: jnp.where(msk, mm[c:, :2 * pw], 0.0), mm_all, incl)
    ms_a = each(lambda mm: mm[:c, 2 * pw:], mm_all)
    ms_r = each(lambda mm: mm[c:, 2 * pw:], mm_all)
    a_ab = each(lambda mm: mm[:, :pw], m_a)
    a_ak = each(lambda mm: mm[:, pw:], m_a)

    row = lax.broadcasted_iota(jnp.int32, (c, pw), 0)
    col = lax.broadcasted_iota(jnp.int32, (c, pw), 1) & (c - 1)
    eye = jnp.where(row == col, 1.0, 0.0)
    sub_bits = int(math.log2(SUB))
    blk = (row >> sub_bits) == (col >> sub_bits)
    ad = each(lambda a: jnp.where(blk, a, 0.0), a_ab)
    ao = each(lambda a, b: a - b, a_ab, ad)
    td = each(lambda a: eye + a, ad)
    x = each(pdot, ad, ad)
    for _ in range(sub_bits - 2):
        res = each(lambda xx, tt: pdot(rows2(xx, tt), xx), x, td)
        td = each(lambda tt, rr: tt + rr[c:], td, res)
        x = each(lambda rr: rr[:c], res)
    td = each(lambda tt, xx: tt + pdot(tt, xx), td, x)
    x = each(pdot, td, ao)
    w = td
    for _ in range(int(math.log2(c // SUB)) - 1):
        res = each(lambda xx, ww: jnp.dot(bf(xx), jnp.concatenate([bd(xx), bd(ww)], axis=1),
                                          preferred_element_type=F32), x, w)
        w = each(lambda ww, rr: ww + rr[:, pw:], w, res)
        x = each(lambda rr: rr[:, :pw], res)
    t_inv = each(lambda ww, xx: ww + pdot(xx, ww), w, x)

    akv = each(pdot, a_ak, v)
    u = each(lambda t, m0, m1: bf(pdot(t, m0 + m1)), t_inv, ms_a, akv)
    y = each(lambda m0, a, uu, vv: m0 + jnp.dot(bf(a), rows2(bd(uu), bd(vv)), preferred_element_type=F32),
             ms_r, m_r, u, v)
    left = lax.broadcasted_iota(jnp.int32, (n, pw), 1) < n
    upd = each(lambda uu, vv, w: _bdot_tn(rows2(uu, vv), w), u, v, bk)
    s_new = each(lambda ss, p, dd: (ss + jnp.where(left, dd[:n], dd[n:])) * p, s, p_end, upd)
    for (b, d, p), val in zip(chains, s_new):
        st_ref[b, d, p] = val
    for b in range(nb):
        yf_ref[b] = jnp.concatenate(y[2 * b * pairs:(2 * b + 1) * pairs], axis=1)
        yr_ref[b] = jnp.concatenate(y[(2 * b + 1) * pairs:(2 * b + 2) * pairs], axis=1)


def _rwkv_call(op_f, op_r, v, pe_f, pe_r, *, n_ctx):
    bsz, tt, d_a = v.shape
    n_chunks = tt // CHUNK
    nc_ctx = n_ctx // CHUNK
    heads = d_a // HEAD_DIM

    rev_chunk = lambda j: jnp.where(j < nc_ctx, nc_ctx - 1 - j, n_chunks - 1 - (j - nc_ctx))
    fwd = lambda b, j: (b, j, 0)
    rev = lambda b, j: (b, rev_chunk(j), 0)
    y_shape = jax.ShapeDtypeStruct((bsz, tt, d_a), F32)
    nb = next(m for m in (4, 2, 1) if bsz % m == 0)
    return pl.pallas_call(
        functools.partial(_rwkv_pair_kernel, d_a=d_a),
        out_shape=(y_shape, y_shape),
        grid=(bsz // nb, n_chunks),
        in_specs=[pl.BlockSpec((nb, CHUNK, N_OPERANDS * d_a), fwd),
                  pl.BlockSpec((nb, CHUNK, N_OPERANDS * d_a), rev),
                  pl.BlockSpec((nb, CHUNK, d_a), fwd),
                  pl.BlockSpec((nb, CHUNK, d_a), rev),
                  pl.BlockSpec((nb, 1, 8, d_a), lambda b, j: (b, j, 0, 0)),
                  pl.BlockSpec((nb, 1, 8, d_a), lambda b, j: (b, rev_chunk(j), 0, 0))],
        out_specs=(pl.BlockSpec((nb, CHUNK, d_a), fwd), pl.BlockSpec((nb, CHUNK, d_a), rev)),
        scratch_shapes=[pltpu.VMEM((nb, 2, heads // 2, HEAD_DIM, 2 * HEAD_DIM), F32)],
        compiler_params=pltpu.CompilerParams(dimension_semantics=("parallel", "arbitrary"),
                                             vmem_limit_bytes=VMEM_LIMIT),
        name="rwkv7_chunk_scan",
    )(op_f, op_r, v, v, pe_f, pe_r)


def _softmax_pv(heads_parts):
    ms = [functools.reduce(jnp.maximum, [jnp.max(s, axis=-1, keepdims=True) for s, _ in parts])
          for parts in heads_parts]
    ps = [[jnp.exp(s - m) for s, _ in parts] for parts, m in zip(heads_parts, ms)]
    dens = [sum(jnp.sum(p, axis=-1, keepdims=True) for p in pp) for pp in ps]
    nums = [sum(jnp.dot(p.astype(BF16), vals, preferred_element_type=F32) for p, (_, vals) in zip(pp, parts))
            for pp, parts in zip(ps, heads_parts)]
    return [num / den for num, den in zip(nums, dens)]


def _attn_kernel(q_ref, k_ref, v_ref, bias_ref, o_ref, *, n_ctx, rows, q_off, d_b):
    n = HEAD_DIM
    heads = d_b // n
    nq_ctx = n_ctx // GRID_W
    i = pl.program_id(1) + q_off
    nb = q_ref.shape[0]
    q = [q_ref[b] for b in range(nb)]
    kc = [k_ref[b, 0:n_ctx, :] for b in range(nb)]
    vc = [v_ref[b, 0:n_ctx, :] for b in range(nb)]
    units = [(b, h, slice(h * n, (h + 1) * n)) for b in range(nb) for h in range(heads)]

    def store(outs):
        for b in range(nb):
            o_ref[b] = jnp.concatenate(outs[b * heads:(b + 1) * heads], axis=1).astype(o_ref.dtype)

    def latent():
        li = i - nq_ctx
        r0 = jnp.clip(li - WIN_H // 2, 0, rows - WIN_H)
        start = pl.multiple_of(n_ctx + r0 * GRID_W, GRID_W)
        kl = [k_ref[b, pl.ds(start, WIN_H * GRID_W), :] for b in range(nb)]
        vl = [v_ref[b, pl.ds(start, WIN_H * GRID_W), :] for b in range(nb)]
        s_l = [_bdot_nt(q[b][:, sl], kl[b][:, sl]) + bias_ref[0, h] for b, h, sl in units]
        s_c = [_bdot_nt(q[b][:, sl], kc[b][:, sl]) for b, h, sl in units]
        store(_softmax_pv([[(sa, vl[b][:, sl]), (sb, vc[b][:, sl])]
                           for sa, sb, (b, h, sl) in zip(s_l, s_c, units)]))

    def context():
        store(_softmax_pv([[(_bdot_nt(q[b][:, sl], kc[b][:, sl]), vc[b][:, sl])] for b, h, sl in units]))

    if q_off == 0:
        pl.when(i >= nq_ctx)(latent)
        pl.when(i < nq_ctx)(context)
    else:
        latent()


def _bias_table(rpb):
    w = GRID_W
    heads = rpb.shape[0]
    j = np.arange(w)
    col_start = np.clip(j - WIN_W // 2, 0, w - WIN_W)
    in_win = (j[None, :] >= col_start[:, None]) & (j[None, :] < col_start[:, None] + WIN_W)
    rpb = rpb.astype(F32)
    ext = jnp.concatenate([jnp.repeat(rpb[..., :1], w - WIN_W, axis=-1), rpb,
                           jnp.repeat(rpb[..., -1:], w - WIN_W, axis=-1)], axis=-1)
    toe = jnp.stack([ext[..., w - 1 - qc:2 * w - 1 - qc] for qc in range(w)], axis=2)
    tz = jnp.where(in_win, toe, NEG_BIG)
    bt = jnp.stack([tz[:, v:v + WIN_H] for v in range(WIN_H)], axis=0)
    return bt.transpose(0, 1, 3, 2, 4).reshape(WIN_H, heads, w, WIN_H * w)


def _attn_call(qkv, bias_tab, *, n_ctx, with_ctx):
    bsz, tt, d3 = qkv.shape
    d_b = d3 // 3
    heads = d_b // HEAD_DIM
    rows = (tt - n_ctx) // GRID_W
    nq_ctx = n_ctx // GRID_W
    q_off = 0 if with_ctx else nq_ctx
    n_q = tt // GRID_W - q_off
    nb = next(m for m in (4, 2, 1) if bsz % m == 0)

    def variant(b, i):
        li = jnp.maximum(i + q_off - nq_ctx, 0)
        return (jnp.clip(li - WIN_H // 2, 0, rows - WIN_H) - li + WIN_H - 1, 0, 0, 0)

    return pl.pallas_call(
        functools.partial(_attn_kernel, n_ctx=n_ctx, rows=rows, q_off=q_off, d_b=d_b),
        out_shape=jax.ShapeDtypeStruct((bsz, n_q * GRID_W, d_b), BF16),
        grid=(bsz // nb, n_q),
        in_specs=[pl.BlockSpec((nb, GRID_W, d_b), lambda b, i: (b, i + q_off, 0)),
                  pl.BlockSpec((nb, tt, d_b), lambda b, i: (b, 0, 1)),
                  pl.BlockSpec((nb, tt, d_b), lambda b, i: (b, 0, 2)),
                  pl.BlockSpec((1, heads, GRID_W, WIN_H * GRID_W), variant)],
        out_specs=pl.BlockSpec((nb, GRID_W, d_b), lambda b, i: (b, i, 0)),
        compiler_params=pltpu.CompilerParams(dimension_semantics=("parallel", "arbitrary"),
                                             vmem_limit_bytes=VMEM_LIMIT),
        name="neighbourhood_attention",
    )(qkv, qkv, qkv, bias_tab)


def _merge_kernel(xc_ref, x_ref, mod_ref, y0_ref, y1_ref, rkv_ref, lg_ref, yb_ref, gates_ref,
                  rk_ref, lng_ref, lnb_ref, glb_ref, pa_ref, pb_ref, wo_ref, gs_ref, o_ref,
                  *, tm, n_ctx, t_off, d_a):
    t = pl.program_id(1) + t_off
    is_ctx = _ctx_rows(t, tm, n_ctx)
    gs = gs_ref[...]
    inv_n = 1.0 / HEAD_DIM
    y = y0_ref[0] + y1_ref[0]
    mu = _group_sum2(y, gs) * inv_n
    yc = y - mu
    var = _group_sum2(yc * yc, gs) * inv_n
    yn = yc * lax.rsqrt(var + GN_EPS) * lng_ref[...] + lnb_ref[...]
    rkv = rkv_ref[0]
    r, k, v = rkv[:, :d_a], rkv[:, d_a:2 * d_a], rkv[:, 2 * d_a:]
    bonus = _group_sum2(r * k * rk_ref[...], gs) * v
    g = _bdot(_sigmoid(lg_ref[0]), glb_ref[...])
    ya = (yn + bonus) * g
    gates = gates_ref[0].astype(F32)
    dm = gates.shape[1] // 2
    merged = gates[:, :dm] * _bdot(ya, pa_ref[...]) + gates[:, dm:] * _bdot(yb_ref[0], pb_ref[...])
    o_ref[0] = (_token_tile(xc_ref, x_ref, t, tm, n_ctx)
                + _mod_row(mod_ref, 2, is_ctx) * _bdot(merged, wo_ref[...]))


def _merge_call(xs, modsel, y_f, y_r, rkv, lg, yb, gates, r_k, ln_g, ln_b, glb, pa, pb, wo, gs,
                *, n_ctx, with_ctx, tm=256):
    bsz, tt, d_a3 = rkv.shape
    d_a = d_a3 // 3
    t_off = 0 if with_ctx else n_ctx // tm
    n_t = tt // tm - t_off
    tok_arrays, tok_specs = _token_sources(xs, tm, n_ctx, t_off)
    d = tok_arrays[1].shape[2]
    yb_off = 0 if with_ctx else -t_off
    row = lambda b, t: (b, t + t_off, 0)
    const = lambda b, t: (0, 0)
    return pl.pallas_call(
        functools.partial(_merge_kernel, tm=tm, n_ctx=n_ctx, t_off=t_off, d_a=d_a),
        out_shape=jax.ShapeDtypeStruct((bsz, n_t * tm, d), F32),
        grid=(bsz, n_t),
        in_specs=tok_specs + [
                  pl.BlockSpec((1, 2, 8, d), lambda b, t: (b, 0, 0, 0)),
                  pl.BlockSpec((1, tm, d_a), row),
                  pl.BlockSpec((1, tm, d_a), row),
                  pl.BlockSpec((1, tm, 3 * d_a), row),
                  pl.BlockSpec((1, tm, lg.shape[2]), row),
                  pl.BlockSpec((1, tm, yb.shape[2]), lambda b, t: (b, t + t_off + yb_off, 0)),
                  pl.BlockSpec((1, tm, gates.shape[2]), row),
                  pl.BlockSpec((1, d_a), const),
                  pl.BlockSpec((1, d_a), const),
                  pl.BlockSpec((1, d_a), const),
                  pl.BlockSpec(glb.shape, const),
                  pl.BlockSpec(pa.shape, const),
                  pl.BlockSpec(pb.shape, const),
                  pl.BlockSpec(wo.shape, const),
                  pl.BlockSpec((LANE, LANE), const)],
        out_specs=pl.BlockSpec((1, tm, d), lambda b, t: (b, t, 0)),
        compiler_params=pltpu.CompilerParams(dimension_semantics=("parallel", "parallel"),
                                             vmem_limit_bytes=VMEM_LIMIT),
        name="branch_merge",
    )(*tok_arrays, modsel, y_f, y_r, rkv, lg, yb, gates, r_k, ln_g, ln_b, glb, pa, pb, wo, gs)


def _route(sel, scores, n_experts):
    per = n_experts // N_GROUPS
    in_top = []
    for e in range(n_experts):
        g0 = (e // per) * per
        rank = 0.0
        for o in range(g0, g0 + per):
            if o == e:
                continue
            ahead = (sel[o] >= sel[e]) if o < e else (sel[o] > sel[e])
            rank = rank + jnp.where(ahead, 1.0, 0.0)
        in_top.append(rank < TOP_K)
    grp = [sum(jnp.where(in_top[e], sel[e], 0.0) for e in range(g * per, (g + 1) * per))
           for g in range(N_GROUPS)]
    best = jnp.zeros_like(grp[0], dtype=jnp.int32)
    best_s = grp[0]
    for g in range(1, N_GROUPS):
        better = grp[g] > best_s
        best = jnp.where(better, g, best)
        best_s = jnp.where(better, grp[g], best_s)
    chosen = [in_top[e] & (best == e // per) for e in range(n_experts)]
    den = sum(jnp.where(chosen[e], scores[e], 0.0) for e in range(n_experts))
    return [jnp.where(chosen[e], scores[e] / den, 0.0) for e in range(n_experts)], best


MOE_EXPERTS_PER_STEP = 2
MOE_BLK = 128
POS_COL = 16


def _moe_route_kernel(x_ref, mod_ref, g_ref, rw_ref, rb_ref, hs_ref, gs_ref, tab_ref, cnt_ref,
                      *, tm, ts, n_ctx, n_experts):
    t = pl.program_id(1)
    is_ctx = _ctx_rows(t, tm, n_ctx)
    h = _rms_mod(x_ref[0], g_ref[...], _mod_row(mod_ref, 3, is_ctx), _mod_row(mod_ref, 4, is_ctx))
    hh, hl = _split(h)
    wh, wl = _split(rw_ref[...])
    nt = lambda a, b: lax.dot_general(a, b, (((1,), (1,)), ((), ())), preferred_element_type=F32)
    logits = nt(wh, hh) + nt(wh, hl) + nt(wl, hh)
    scores = _sigmoid(logits)
    sel = scores + rb_ref[...]
    gates, best = _route([sel[i:i + 1, :] for i in range(n_experts)],
                         [scores[i:i + 1, :] for i in range(n_experts)], n_experts)

    in_grp = [jnp.where(best == g, 1.0, 0.0) for g in range(N_GROUPS)]
    grp8 = jnp.concatenate(in_grp + [jnp.zeros((8 - N_GROUPS, tm), F32)], axis=0).astype(BF16)
    row = lax.broadcasted_iota(jnp.int32, (tm, tm), 0)
    col = lax.broadcasted_iota(jnp.int32, (tm, tm), 1)
    upper = jnp.where(row <= col, 1.0, 0.0).astype(BF16)
    run = jnp.dot(grp8, upper, preferred_element_type=F32)
    pos = jnp.zeros((1, tm), F32)
    seg = jnp.zeros((1, 1), F32)
    cnts = []
    for g in range(N_GROUPS):
        cnt = run[g:g + 1, tm - 1:tm]
        cnts.append(cnt)
        pos = pos + in_grp[g] * (seg + run[g:g + 1, :] - 1.0)
        seg = seg + jnp.floor((cnt + (MOE_BLK - 1)) * (1.0 / MOE_BLK)) * MOE_BLK
    cnt_ref[0, 0] = jnp.concatenate([jnp.broadcast_to(c, (1, LANE)) for c in cnts]
                                    + [jnp.zeros((8 - N_GROUPS, LANE), F32)], axis=0)
    tab = jnp.concatenate(gates + [pos, jnp.zeros((LANE - n_experts - 1, tm), F32)], axis=0).T
    tab_ref[0] = tab
    sel = jnp.where(lax.broadcasted_iota(jnp.int32, (ts, tm), 0) == pos.astype(jnp.int32), 1.0, 0.0).astype(BF16)
    hs_ref[0, 0] = jnp.dot(sel, h.astype(BF16), preferred_element_type=F32).astype(BF16)
    t1 = tab.astype(BF16)
    r1 = tab - t1.astype(F32)
    t2 = r1.astype(BF16)
    t3 = (r1 - t2.astype(F32)).astype(BF16)
    gs_ref[0, 0] = (jnp.dot(sel, t1, preferred_element_type=F32) + jnp.dot(sel, t2, preferred_element_type=F32)
                    + jnp.dot(sel, t3, preferred_element_type=F32))


def _moe_expert_kernel(lo_ref, hi_ref, x_ref, mod_ref, hs_ref, gs_ref, tab_ref, w1_ref, w3_ref, w2_ref, o_ref,
                       acc_ref, *, tm, ts, n_ctx, n_experts):
    b = pl.program_id(0)
    t = pl.program_id(1)
    e = pl.program_id(2)
    n_t = pl.num_programs(1)

    @pl.when(e == 0)
    def _():
        acc_ref[...] = jnp.zeros_like(acc_ref)

    per_step = w1_ref.shape[1]
    e_first = e * per_step
    seg = (b * n_t + t) * N_GROUPS + e_first // (n_experts // N_GROUPS)
    lane = lax.broadcasted_iota(jnp.int32, (MOE_BLK, LANE), 1)

    def block(j, carry):
        rows = pl.ds(pl.multiple_of(j * MOE_BLK, MOE_BLK), MOE_BLK)
        hb = hs_ref[0, 0, rows, :]
        gs = gs_ref[0, 0, rows, :]
        hids = []
        for i in range(per_step):
            a = jnp.dot(hb, w1_ref[0, i], preferred_element_type=F32)
            bb = jnp.dot(hb, w3_ref[0, i], preferred_element_type=F32)
            gcol = jnp.sum(jnp.where(lane == e_first + i, gs, 0.0), axis=-1, keepdims=True)
            hids.append((a * _sigmoid(a) * bb * gcol).astype(BF16))
        w2 = w2_ref[0].reshape(per_step * w2_ref.shape[2], w2_ref.shape[3])
        acc_ref[rows, :] += jnp.dot(jnp.concatenate(hids, axis=1), w2, preferred_element_type=F32)
        return carry

    lax.fori_loop(lo_ref[seg], hi_ref[seg], block, 0)

    @pl.when(e == pl.num_programs(2) - 1)
    def _():
        is_ctx = _ctx_rows(t, tm, n_ctx)
        pos = tab_ref[0][:, POS_COL:POS_COL + 1].astype(jnp.int32)
        back = jnp.where(lax.broadcasted_iota(jnp.int32, (tm, ts), 1) == pos, 1.0, 0.0).astype(BF16)
        y = jnp.dot(back, acc_ref[...].astype(BF16), preferred_element_type=F32)
        o_ref[0] = x_ref[0] + _mod_row(mod_ref, 5, is_ctx) * y


def _moe_call(xs, modsel, norm_g, router_wt, router_b, w1, w3, w2, *, layer, n_ctx, tm):
    bsz, tt, d = xs.shape
    n_experts = router_wt.shape[0]
    n_t = tt // tm
    ts = tm + N_GROUPS * MOE_BLK
    row2 = lambda b, t: (b, t, 0)
    const2 = lambda b, t: (0, 0)
    tile2 = lambda b, t: (b, t, 0, 0)
    hs, gs, tab, cnt = pl.pallas_call(
        functools.partial(_moe_route_kernel, tm=tm, ts=ts, n_ctx=n_ctx, n_experts=n_experts),
        out_shape=(jax.ShapeDtypeStruct((bsz, n_t, ts, d), BF16),
                   jax.ShapeDtypeStruct((bsz, n_t, ts, LANE), F32),
                   jax.ShapeDtypeStruct((bsz, tt, LANE), F32),
                   jax.ShapeDtypeStruct((bsz, n_t, 8, LANE), F32)),
        grid=(bsz, n_t),
        in_specs=[pl.BlockSpec((1, tm, d), row2),
                  pl.BlockSpec((1, 2, 8, d), lambda b, t: (b, 0, 0, 0)),
                  pl.BlockSpec((1, d), const2),
                  pl.BlockSpec(router_wt.shape, const2),
                  pl.BlockSpec((n_experts, 1), const2)],
        out_specs=(pl.BlockSpec((1, 1, ts, d), tile2),
                   pl.BlockSpec((1, 1, ts, LANE), tile2),
                   pl.BlockSpec((1, tm, LANE), row2),
                   pl.BlockSpec((1, 1, 8, LANE), tile2)),
        compiler_params=pltpu.CompilerParams(dimension_semantics=("parallel", "parallel"),
                                             vmem_limit_bytes=VMEM_LIMIT),
        name="moe_route",
    )(xs, modsel, norm_g, router_wt, router_b)

    n_blk = (cnt[:, :, :N_GROUPS, 0].astype(jnp.int32) + (MOE_BLK - 1)) // MOE_BLK
    hi = jnp.cumsum(n_blk, axis=-1)
    lo = hi - n_blk

    row = lambda b, t, e, lo_r, hi_r: (b, t, 0)
    wexp = lambda b, t, e, lo_r, hi_r: (layer, e, 0, 0)
    tile4 = lambda b, t, e, lo_r, hi_r: (b, t, 0, 0)
    return pl.pallas_call(
        functools.partial(_moe_expert_kernel, tm=tm, ts=ts, n_ctx=n_ctx, n_experts=n_experts),
        out_shape=jax.ShapeDtypeStruct((bsz, tt, d), F32),
        grid_spec=pltpu.PrefetchScalarGridSpec(
            num_scalar_prefetch=2,
            grid=(bsz, n_t, n_experts // MOE_EXPERTS_PER_STEP),
            in_specs=[pl.BlockSpec((1, tm, d), row),
                      pl.BlockSpec((1, 2, 8, d), lambda b, t, e, lo_r, hi_r: (b, 0, 0, 0)),
                      pl.BlockSpec((1, 1, ts, d), tile4),
                      pl.BlockSpec((1, 1, ts, LANE), tile4),
                      pl.BlockSpec((1, tm, LANE), row),
                      pl.BlockSpec((1, MOE_EXPERTS_PER_STEP) + w1.shape[2:], wexp),
                      pl.BlockSpec((1, MOE_EXPERTS_PER_STEP) + w3.shape[2:], wexp),
                      pl.BlockSpec((1, MOE_EXPERTS_PER_STEP) + w2.shape[2:], wexp)],
            out_specs=pl.BlockSpec((1, tm, d), row),
            scratch_shapes=[pltpu.VMEM((ts, d), F32)]),
        compiler_params=pltpu.CompilerParams(
            dimension_semantics=("parallel", "parallel", "arbitrary"),
            vmem_limit_bytes=VMEM_LIMIT),
        name="moe_experts",
    )(lo.reshape(-1), hi.reshape(-1), xs, modsel, hs, gs, tab, w1, w3, w2)


def kernel(x, c, ctx, c_ctx, mod_w, mod_b, norm1_g, norm2_g, w_in, rw_w0, rw_w_lora_b, rw_a0, rw_a_lora_b,
           rw_g_lora_b, rw_k_k, rw_k_a, rw_r_k, rw_ln_g, rw_ln_b, na_q_g, na_k_g, na_rpb, proj_a, proj_b,
           w_out, router_w, router_bias, moe_w1, moe_w3, moe_w2):
    bsz, seq, d = x.shape
    n_ctx = ctx.shape[1]
    depth = mod_w.shape[0]
    d_a = rw_w0.shape[2]
    d_b = proj_b.shape[1]
    lora_g = rw_g_lora_b.shape[1]
    heads_b = d_b // HEAD_DIM
    assert seq % (GRID_W * WIN_H) == 0 and n_ctx % 256 == 0 and bsz + 1 <= 16

    cs = jnp.zeros((16, d), F32).at[:bsz].set(c).at[bsz].set(c_ctx)
    mod = _mod_call(cs, mod_w, mod_b).reshape(depth, 16, 6, d)

    lane = jnp.arange(LANE)
    gs = (lane[:, None] // HEAD_DIM == lane[None, :] // HEAD_DIM).astype(BF16)
    router_wt = router_w.T
    router_b = router_bias.reshape(-1, 1)

    w_in_b, w1_b, w3_b, w2_b = (_to_bf16(w) for w in (w_in, moe_w1, moe_w3, moe_w2))
    xs = (ctx, x)
    for l in range(depth):
        last = l == depth - 1
        m_c = jnp.broadcast_to(mod[l, bsz][None], (bsz, 6, d))
        modsel = jnp.pad(jnp.stack([m_c, mod[l, :bsz]], axis=1), ((0, 0), (0, 0), (0, 2), (0, 0)))

        rkv, lg, qkv, gates, op_f, op_r, v_a, pe_f, pe_r = _in_proj_call(
            xs, modsel, norm1_g[l][None], w_in_b, rw_k_k[l][None],
            jnp.tile(na_q_g[l], heads_b)[None], jnp.tile(na_k_g[l], heads_b)[None], gs,
            rw_w0[l], rw_w_lora_b[l], rw_a0[l], rw_a_lora_b[l], rw_k_a[l][None],
            layer=l, n_ctx=n_ctx, d_a=d_a, d_b=d_b, lora_g=lora_g)
        y_f, y_r = _rwkv_call(op_f, op_r, v_a, pe_f, pe_r, n_ctx=n_ctx)
        yb = _attn_call(qkv, _bias_table(na_rpb[l]), n_ctx=n_ctx, with_ctx=not last)
        xs = _merge_call(xs, modsel, y_f, y_r, rkv, lg, yb, gates, rw_r_k[l].reshape(1, d_a),
                         rw_ln_g[l][None], rw_ln_b[l][None], rw_g_lora_b[l].astype(BF16),
                         proj_a[l].astype(BF16), proj_b[l].astype(BF16), w_out[l].astype(BF16), gs,
                         n_ctx=n_ctx, with_ctx=not last)
        moe_ctx = 0 if last else n_ctx
        tm = next(m for m in (1024, 768, 512, 256) if xs.shape[1] % m == 0)
        xs = _moe_call(xs, modsel, norm2_g[l][None], router_wt, router_b, w1_b, w3_b, w2_b,
                       layer=l, n_ctx=moe_ctx, tm=tm)
    return xs
```

```python
import functools
import math

import jax
import jax.numpy as jnp
import numpy as np
from jax import lax
from jax.experimental import pallas as pl
from jax.experimental.pallas import tpu as pltpu

F32 = jnp.float32
BF16 = jnp.bfloat16

HEAD_DIM = 64
GRID_W = 64
WIN_H = 8
WIN_W = 16
N_GROUPS = 4
TOP_K = 2
RMS_EPS = 1e-6
GN_EPS = 64e-5
LANE = 128
CHUNK = 64
SUB = 16
NEG_BIG = -1e30
DECAY_SCALE = math.exp(-0.5)
VMEM_LIMIT = 56 * 1024 * 1024


def _bdot(a, b):
    return jnp.dot(a.astype(BF16), b.astype(BF16), preferred_element_type=F32)


def _bdot_nt(a, b):
    return lax.dot_general(a.astype(BF16), b.astype(BF16), (((1,), (1,)), ((), ())),
                           preferred_element_type=F32)


def _bdot_tn(a, b):
    return lax.dot_general(a.astype(BF16), b.astype(BF16), (((0,), (0,)), ((), ())),
                           preferred_element_type=F32)


def _split(x):
    hi = x.astype(BF16)
    lo = (x - hi.astype(F32)).astype(BF16)
    return hi, lo


def _dot3(a, b):
    ah, al = _split(a)
    bh, bl = _split(b)
    return (jnp.dot(ah, bh, preferred_element_type=F32)
            + jnp.dot(al, bh, preferred_element_type=F32)
            + jnp.dot(ah, bl, preferred_element_type=F32))


def _sigmoid(x):
    return 1.0 / (1.0 + jnp.exp(-x))


def _group_sum(x, g128):
    parts = [_bdot(x[:, j * LANE:(j + 1) * LANE], g128) for j in range(x.shape[1] // LANE)]
    return jnp.concatenate(parts, axis=1)


def _group_sum2(x, g128):
    hi, lo = _split(x)
    parts = []
    for j in range(x.shape[1] // LANE):
        sl = slice(j * LANE, (j + 1) * LANE)
        parts.append(jnp.dot(hi[:, sl], g128, preferred_element_type=F32)
                     + jnp.dot(lo[:, sl], g128, preferred_element_type=F32))
    return jnp.concatenate(parts, axis=1)


def _mod_row(mod_ref, idx, is_ctx):
    mx = mod_ref[0, 1, idx:idx + 1, :]
    if is_ctx is None:
        return mx
    return jnp.where(is_ctx, mod_ref[0, 0, idx:idx + 1, :], mx)


def _ctx_rows(tile, tm, n_ctx):
    if n_ctx == 0:
        return None
    rows = tile * tm + lax.broadcasted_iota(jnp.int32, (tm, 1), 0)
    return rows < n_ctx


def _token_sources(src, tm, n_ctx, t_off):
    assert n_ctx % tm == 0
    n_ct = n_ctx // tm
    if isinstance(src, tuple):
        ctx, x = src
        d = x.shape[2]
        lat = pl.BlockSpec((1, tm, d), lambda b, t: (b, jnp.maximum(t + t_off - n_ct, 0), 0))
    else:
        ctx = x = src
        d = x.shape[2]
        lat = pl.BlockSpec((1, tm, d), lambda b, t: (b, t + t_off, 0))
    con = pl.BlockSpec((1, tm, d), lambda b, t: (b, jnp.minimum(t + t_off, n_ct - 1), 0))
    return (ctx, x), [con, lat]


def _token_tile(ctx_ref, lat_ref, tile, tm, n_ctx):
    return jnp.where(tile * tm < n_ctx, ctx_ref[0], lat_ref[0])


def _rms_mod(x, gain, shift, scale):
    xn = x * lax.rsqrt(jnp.mean(x * x, axis=-1, keepdims=True) + RMS_EPS) * gain
    return xn * (1.0 + scale) + shift


CAST_BLOCK_BYTES = 8 * 1024 * 1024


def _cast_kernel(w_ref, o_ref):
    o_ref[...] = w_ref[...].astype(o_ref.dtype)


def _to_bf16(w):
    c = w.shape[-1]
    w2 = w.reshape(-1, c)
    rows = w2.shape[0]
    rb = max(16, min(rows, CAST_BLOCK_BYTES // (c * 4)) // 16 * 16)
    while rows % rb:
        rb -= 16
    out = pl.pallas_call(
        _cast_kernel,
        out_shape=jax.ShapeDtypeStruct(w2.shape, BF16),
        grid=(rows // rb,),
        in_specs=[pl.BlockSpec((rb, c), lambda i: (i, 0))],
        out_specs=pl.BlockSpec((rb, c), lambda i: (i, 0)),
        compiler_params=pltpu.CompilerParams(dimension_semantics=("parallel",), vmem_limit_bytes=VMEM_LIMIT),
        name="to_bf16",
    )(w2)
    return out.reshape(w.shape)


def _mod_kernel(c_ref, w_ref, b_ref, o_ref):
    c = c_ref[...]
    o_ref[0] = _dot3(c * _sigmoid(c), w_ref[0]) + b_ref[0]


def _mod_call(cs, mod_w, mod_b):
    depth, d, n = mod_w.shape
    tn = n // 4
    return pl.pallas_call(
        _mod_kernel,
        out_shape=jax.ShapeDtypeStruct((depth, cs.shape[0], n), F32),
        grid=(depth, n // tn),
        in_specs=[pl.BlockSpec(cs.shape, lambda l, j: (0, 0)),
                  pl.BlockSpec((1, d, tn), lambda l, j: (l, 0, j)),
                  pl.BlockSpec((1, 1, tn), lambda l, j: (l, 0, j))],
        out_specs=pl.BlockSpec((1, cs.shape[0], tn), lambda l, j: (l, 0, j)),
        compiler_params=pltpu.CompilerParams(dimension_semantics=("parallel", "parallel"),
                                             vmem_limit_bytes=VMEM_LIMIT),
        name="adaln_vectors",
    )(cs, mod_w, mod_b.reshape(depth, 1, n))


def _in_proj_kernel(xc_ref, x_ref, mod_ref, g_ref, w_ref, kkg_ref, qg_ref, kg_ref, gs_ref,
                    w0_ref, wlbh_ref, wlbl_ref, a0_ref, albh_ref, albl_ref, ka_ref,
                    rkv_ref, lg_ref, qkv_ref, gates_ref, opf_ref, opr_ref, v_ref, pef_ref, per_ref,
                    *, tm, n_ctx, d_a, d_b, lora_w, lora_a, lora_g):
    t = pl.program_id(1)
    is_ctx = _ctx_rows(t, tm, n_ctx)
    h = _rms_mod(_token_tile(xc_ref, x_ref, t, tm, n_ctx), g_ref[...],
                 _mod_row(mod_ref, 0, is_ctx), _mod_row(mod_ref, 1, is_ctx))
    h = h.astype(BF16)
    gs = gs_ref[...]
    n_lora = 2 * lora_w + 2 * lora_a + lora_g

    o = 0
    rkv = jnp.dot(h, w_ref[0, :, o:o + 3 * d_a], preferred_element_type=F32)
    rkv_ref[0] = rkv
    kkv = rkv[:, d_a:2 * d_a] * kkg_ref[...]
    ss = _group_sum2(kkv * kkv, gs)
    kk = kkv * lax.rsqrt(jnp.maximum(ss, 1e-12))
    o += 3 * d_a

    lora = jnp.dot(h, w_ref[0, :, o:o + n_lora], preferred_element_type=F32)
    lg_ref[0] = lora[:, n_lora - lora_g:]
    v_ref[0] = rkv[:, 2 * d_a:].astype(BF16)
    o += n_lora

    qkv = jnp.dot(h, w_ref[0, :, o:o + 3 * d_b], preferred_element_type=F32)
    q, k = qkv[:, :d_b], qkv[:, d_b:2 * d_b]
    inv_n = 1.0 / HEAD_DIM
    qn = q * lax.rsqrt(_group_sum(q * q, gs) * inv_n + RMS_EPS) * qg_ref[...] * (HEAD_DIM ** -0.5)
    kn = k * lax.rsqrt(_group_sum(k * k, gs) * inv_n + RMS_EPS) * kg_ref[...]
    qkv_ref[0] = jnp.concatenate([qn, kn, qkv[:, 2 * d_b:]], axis=1).astype(BF16)
    o += 3 * d_b

    gates_ref[0] = _sigmoid(jnp.dot(h, w_ref[0, :, o:], preferred_element_type=F32)).astype(BF16)
    _rwkv_operand_rows((0, 1), rkv, kk, lora, w0_ref, wlbh_ref, wlbl_ref, a0_ref, albh_ref, albl_ref, ka_ref,
                       opf_ref, opr_ref, pef_ref, per_ref, tm=tm, d_a=d_a, lora_w=lora_w, lora_a=lora_a)


def _in_proj_call(xs, modsel, norm_g, w_in, k_k, q_g, k_g, gs, w0, wlb, a0, alb, k_a,
                  *, layer, n_ctx, d_a, d_b, lora_g, tm=256):
    tok_arrays, tok_specs = _token_sources(xs, tm, n_ctx, 0)
    bsz, d = tok_arrays[1].shape[0], tok_arrays[1].shape[2]
    tt = sum(a.shape[1] for a in xs) if isinstance(xs, tuple) else xs.shape[1]
    lora_w, lora_a = wlb.shape[1], alb.shape[1]
    n_lora = 2 * lora_w + 2 * lora_a + lora_g
    n_gate = w_in.shape[2] - 3 * d_a - n_lora - 3 * d_b
    row = lambda b, t: (b, t, 0)
    const = lambda b, t: (0, 0)
    const3 = lambda b, t: (0, 0, 0)
    wlb_hi, wlb_lo = _split(wlb)
    alb_hi, alb_lo = _split(alb)
    op_shape = jax.ShapeDtypeStruct((bsz, tt, N_OPERANDS * d_a), BF16)
    op_spec = pl.BlockSpec((1, tm, N_OPERANDS * d_a), row)
    pe_shape = jax.ShapeDtypeStruct((bsz, tt // CHUNK, 8, d_a), F32)
    pe_spec = pl.BlockSpec((1, tm // CHUNK, 8, d_a), lambda b, t: (b, t, 0, 0))
    return pl.pallas_call(
        functools.partial(_in_proj_kernel, tm=tm, n_ctx=n_ctx, d_a=d_a, d_b=d_b,
                          lora_w=lora_w, lora_a=lora_a, lora_g=lora_g),
        out_shape=(jax.ShapeDtypeStruct((bsz, tt, 3 * d_a), F32),
                   jax.ShapeDtypeStruct((bsz, tt, lora_g), F32),
                   jax.ShapeDtypeStruct((bsz, tt, 3 * d_b), BF16),
                   jax.ShapeDtypeStruct((bsz, tt, n_gate), BF16),
                   op_shape, op_shape, jax.ShapeDtypeStruct((bsz, tt, d_a), BF16), pe_shape, pe_shape),
        grid=(bsz, tt // tm),
        in_specs=tok_specs + [
                  pl.BlockSpec((1, 2, 8, d), lambda b, t: (b, 0, 0, 0)),
                  pl.BlockSpec((1, d), const),
                  pl.BlockSpec((1,) + w_in.shape[1:], lambda b, t: (layer, 0, 0)),
                  pl.BlockSpec((1, d_a), const),
                  pl.BlockSpec((1, d_b), const),
                  pl.BlockSpec((1, d_b), const),
                  pl.BlockSpec((LANE, LANE), const),
                  pl.BlockSpec((2, 1, d_a), const3),
                  pl.BlockSpec((2, lora_w, d_a), const3),
                  pl.BlockSpec((2, lora_w, d_a), const3),
                  pl.BlockSpec((2, 1, d_a), const3),
                  pl.BlockSpec((2, lora_a, d_a), const3),
                  pl.BlockSpec((2, lora_a, d_a), const3),
                  pl.BlockSpec((1, d_a), const)],
        out_specs=(pl.BlockSpec((1, tm, 3 * d_a), row),
                   pl.BlockSpec((1, tm, lora_g), row),
                   pl.BlockSpec((1, tm, 3 * d_b), row),
                   pl.BlockSpec((1, tm, n_gate), row),
                   op_spec, op_spec, pl.BlockSpec((1, tm, d_a), row), pe_spec, pe_spec),
        compiler_params=pltpu.CompilerParams(dimension_semantics=("parallel", "parallel"),
                                             vmem_limit_bytes=VMEM_LIMIT),
        name="in_proj",
    )(*tok_arrays, modsel, norm_g, w_in, k_k, q_g, k_g, gs,
      w0.reshape(2, 1, d_a), wlb_hi, wlb_lo, a0.reshape(2, 1, d_a), alb_hi, alb_lo, k_a)


N_OPERANDS = 4


def _dot3_presplit(x, w_hi, w_lo):
    xh, xl = _split(x)
    return (jnp.dot(xh, w_hi, preferred_element_type=F32) + jnp.dot(xl, w_hi, preferred_element_type=F32)
            + jnp.dot(xh, w_lo, preferred_element_type=F32))


def _rwkv_operand_rows(dirs, rkv, kk, lora, w0_ref, wlbh_ref, wlbl_ref, a0_ref, albh_ref, albl_ref, ka_ref,
                       opf_ref, opr_ref, pef_ref, per_ref, *, tm, d_a, lora_w, lora_a):
    r, k = rkv[:, :d_a], rkv[:, d_a:2 * d_a]
    chunk_bits = int(math.log2(CHUNK))
    row = lax.broadcasted_iota(jnp.int32, (tm, tm), 0)
    col = lax.broadcasted_iota(jnp.int32, (tm, tm), 1)
    same = (row >> chunk_bits) == (col >> chunk_bits)
    for d in dirs:
        op_ref, pe_ref = ((opf_ref, pef_ref), (opr_ref, per_ref))[d]
        lw = lora[:, d * lora_w:(d + 1) * lora_w]
        la = lora[:, 2 * lora_w + d * lora_a:2 * lora_w + (d + 1) * lora_a]
        wl = w0_ref[d] + _dot3_presplit(jnp.tanh(lw), wlbh_ref[d], wlbl_ref[d])
        ld = -DECAY_SCALE * _sigmoid(wl)
        al = _sigmoid(a0_ref[d] + _dot3_presplit(la, albh_ref[d], albl_ref[d]))
        kd = k * (1.0 + (al - 1.0) * ka_ref[...])
        before = (row <= col) if d else (row >= col)
        tri = jnp.where(same & before, 1.0, 0.0).astype(BF16)
        ld_hi, ld_lo = _split(ld)
        cum = jnp.dot(tri, ld_hi, preferred_element_type=F32) + jnp.dot(tri, ld_lo, preferred_element_type=F32)
        p_inv = jnp.exp(-cum)
        op_ref[0] = jnp.concatenate([-kk * jnp.exp(cum - ld), r * jnp.exp(cum), kk * al * p_inv, kd * p_inv],
                                    axis=1).astype(BF16)
        for ci in range(tm // CHUNK):
            last = ci * CHUNK + (0 if d else CHUNK - 1)
            pe_ref[0, ci] = jnp.exp(jnp.broadcast_to(cum[last:last + 1, :], (8, d_a)))


def _rwkv_pair_kernel(opf_ref, opr_ref, vf_ref, vr_ref, pef_ref, per_ref, yf_ref, yr_ref, st_ref, *, d_a):
    n = HEAD_DIM
    pw = 2 * n
    pairs = d_a // pw
    c = CHUNK
    nb = opf_ref.shape[0]

    @pl.when(pl.program_id(1) == 0)
    def _():
        st_ref[...] = jnp.zeros_like(st_ref)

    chains = [(b, d, p) for b in range(nb) for d in range(2) for p in range(pairs)]
    op_refs = (opf_ref, opr_ref)
    part = lambda i: [op_refs[d][b, :, i * d_a + p * pw:i * d_a + (p + 1) * pw] for b, d, p in chains]
    v_refs = (vf_ref, vr_ref)
    pe_refs = (pef_ref, per_ref)
    v = [v_refs[d][b, :, p * pw:(p + 1) * pw] for b, d, p in chains]
    p_end = [pe_refs[d][b, 0, 0:1, p * pw:(p + 1) * pw] for b, d, p in chains]
    each = lambda f, *ls: [f(*xs) for xs in zip(*ls)]
    rows2 = lambda a, b: jnp.concatenate([a, b], axis=0)
    bf = lambda t: t.astype(BF16)

    def bd(x):
        x = bf(x)
        left = lax.broadcasted_iota(jnp.int32, x.shape, 1) < n
        zero = jnp.zeros_like(x)
        return rows2(jnp.where(left, x, zero), jnp.where(left, zero, x))

    pdot = lambda a, b: jnp.dot(bf(a), bd(b), preferred_element_type=F32)
    ar = each(rows2, part(0), part(1))
    bk = each(rows2, part(2), part(3))
    b_k = each(lambda w: rows2(bd(w[:c]), bd(w[c:])), bk)
    s = [st_ref[b, d, p] for b, d, p in chains]

    mm_all = each(lambda a, w, ss: _bdot_nt(a, rows2(w, bd(ss))), ar, b_k, s)
    row = lax.broadcasted_iota(jnp.int32, (c, 2 * pw), 0)
    col = lax.broadcasted_iota(jnp.int32, (c, 2 * pw), 1) & (c - 1)
    incl = [(row <= col) if d else (row >= col) for _, d, _ in chains]
    strict = [(row < col) if d else (row > col) for _, d, _ in chains]
    m_a = each(lambda mm, msk: jnp.where(msk, mm[:c, :2 * pw], 0.0), mm_all, strict)
    m_r = each(lambda mm, msk: jnp.where(msk, mm[c:, :2 * pw], 0.0), mm_all, incl)
    ms_a = each(lambda mm: mm[:c, 2 * pw:], mm_all)
    ms_r = each(lambda mm: mm[c:, 2 * pw:], mm_all)
    a_ab = each(lambda mm: mm[:, :pw], m_a)
    a_ak = each(lambda mm: mm[:, pw:], m_a)

    row = lax.broadcasted_iota(jnp.int32, (c, pw), 0)
    col = lax.broadcasted_iota(jnp.int32, (c, pw), 1) & (c - 1)
    eye = jnp.where(row == col, 1.0, 0.0)
    sub_bits = int(math.log2(SUB))
    blk = (row >> sub_bits) == (col >> sub_bits)
    ad = each(lambda a: jnp.where(blk, a, 0.0), a_ab)
    ao = each(lambda a, b: a - b, a_ab, ad)
    td = each(lambda a: eye + a, ad)
    x = each(pdot, ad, ad)
    for _ in range(sub_bits - 2):
        res = each(lambda xx, tt: pdot(rows2(xx, tt), xx), x, td)
        td = each(lambda tt, rr: tt + rr[c:], td, res)
        x = each(lambda rr: rr[:c], res)
    td = each(lambda tt, xx: tt + pdot(tt, xx), td, x)
    x = each(pdot, td, ao)
    w = td
    for _ in range(int(math.log2(c // SUB)) - 1):
        res = each(lambda xx, ww: jnp.dot(bf(xx), jnp.concatenate([bd(xx), bd(ww)], axis=1),
                                          preferred_element_type=F32), x, w)
        w = each(lambda ww, rr: ww + rr[:, pw:], w, res)
        x = each(lambda rr: rr[:, :pw], res)
    t_inv = each(lambda ww, xx: ww + pdot(xx, ww), w, x)

    akv = each(pdot, a_ak, v)
    u = each(lambda t, m0, m1: bf(pdot(t, m0 + m1)), t_inv, ms_a, akv)
    y = each(lambda m0, a, uu, vv: m0 + jnp.dot(bf(a), rows2(bd(uu), bd(vv)), preferred_element_type=F32),
             ms_r, m_r, u, v)
    left = lax.broadcasted_iota(jnp.int32, (n, pw), 1) < n
    upd = each(lambda uu, vv, w: _bdot_tn(rows2(uu, vv), w), u, v, bk)
    s_new = each(lambda ss, p, dd: (ss + jnp.where(left, dd[:n], dd[n:])) * p, s, p_end, upd)
    for (b, d, p), val in zip(chains, s_new):
        st_ref[b, d, p] = val
    for b in range(nb):
        yf_ref[b] = jnp.concatenate(y[2 * b * pairs:(2 * b + 1) * pairs], axis=1)
        yr_ref[b] = jnp.concatenate(y[(2 * b + 1) * pairs:(2 * b + 2) * pairs], axis=1)


def _rwkv_call(op_f, op_r, v, pe_f, pe_r, *, n_ctx):
    bsz, tt, d_a = v.shape
    n_chunks = tt // CHUNK
    nc_ctx = n_ctx // CHUNK
    heads = d_a // HEAD_DIM

    rev_chunk = lambda j: jnp.where(j < nc_ctx, nc_ctx - 1 - j, n_chunks - 1 - (j - nc_ctx))
    fwd = lambda b, j: (b, j, 0)
    rev = lambda b, j: (b, rev_chunk(j), 0)
    y_shape = jax.ShapeDtypeStruct((bsz, tt, d_a), F32)
    nb = next(m for m in (4, 2, 1) if bsz % m == 0)
    return pl.pallas_call(
        functools.partial(_rwkv_pair_kernel, d_a=d_a),
        out_shape=(y_shape, y_shape),
        grid=(bsz // nb, n_chunks),
        in_specs=[pl.BlockSpec((nb, CHUNK, N_OPERANDS * d_a), fwd),
                  pl.BlockSpec((nb, CHUNK, N_OPERANDS * d_a), rev),
                  pl.BlockSpec((nb, CHUNK, d_a), fwd),
                  pl.BlockSpec((nb, CHUNK, d_a), rev),
                  pl.BlockSpec((nb, 1, 8, d_a), lambda b, j: (b, j, 0, 0)),
                  pl.BlockSpec((nb, 1, 8, d_a), lambda b, j: (b, rev_chunk(j), 0, 0))],
        out_specs=(pl.BlockSpec((nb, CHUNK, d_a), fwd), pl.BlockSpec((nb, CHUNK, d_a), rev)),
        scratch_shapes=[pltpu.VMEM((nb, 2, heads // 2, HEAD_DIM, 2 * HEAD_DIM), F32)],
        compiler_params=pltpu.CompilerParams(dimension_semantics=("parallel", "arbitrary"),
                                             vmem_limit_bytes=VMEM_LIMIT),
        name="rwkv7_chunk_scan",
    )(op_f, op_r, v, v, pe_f, pe_r)


def _softmax_pv(heads_parts):
    ms = [functools.reduce(jnp.maximum, [jnp.max(s, axis=-1, keepdims=True) for s, _ in parts])
          for parts in heads_parts]
    ps = [[jnp.exp(s - m) for s, _ in parts] for parts, m in zip(heads_parts, ms)]
    dens = [sum(jnp.sum(p, axis=-1, keepdims=True) for p in pp) for pp in ps]
    nums = [sum(jnp.dot(p.astype(BF16), vals, preferred_element_type=F32) for p, (_, vals) in zip(pp, parts))
            for pp, parts in zip(ps, heads_parts)]
    return [num / den for num, den in zip(nums, dens)]


def _attn_kernel(q_ref, k_ref, v_ref, bias_ref, o_ref, *, n_ctx, rows, q_off, d_b):
    n = HEAD_DIM
    heads = d_b // n
    nq_ctx = n_ctx // GRID_W
    i = pl.program_id(1) + q_off
    nb = q_ref.shape[0]
    q = [q_ref[b] for b in range(nb)]
    kc = [k_ref[b, 0:n_ctx, :] for b in range(nb)]
    vc = [v_ref[b, 0:n_ctx, :] for b in range(nb)]
    units = [(b, h, slice(h * n, (h + 1) * n)) for b in range(nb) for h in range(heads)]

    def store(outs):
        for b in range(nb):
            o_ref[b] = jnp.concatenate(outs[b * heads:(b + 1) * heads], axis=1).astype(o_ref.dtype)

    def latent():
        li = i - nq_ctx
        r0 = jnp.clip(li - WIN_H // 2, 0, rows - WIN_H)
        start = pl.multiple_of(n_ctx + r0 * GRID_W, GRID_W)
        kl = [k_ref[b, pl.ds(start, WIN_H * GRID_W), :] for b in range(nb)]
        vl = [v_ref[b, pl.ds(start, WIN_H * GRID_W), :] for b in range(nb)]
        s_l = [_bdot_nt(q[b][:, sl], kl[b][:, sl]) + bias_ref[0, h] for b, h, sl in units]
        s_c = [_bdot_nt(q[b][:, sl], kc[b][:, sl]) for b, h, sl in units]
        store(_softmax_pv([[(sa, vl[b][:, sl]), (sb, vc[b][:, sl])]
                           for sa, sb, (b, h, sl) in zip(s_l, s_c, units)]))

    def context():
        store(_softmax_pv([[(_bdot_nt(q[b][:, sl], kc[b][:, sl]), vc[b][:, sl])] for b, h, sl in units]))

    if q_off == 0:
        pl.when(i >= nq_ctx)(latent)
        pl.when(i < nq_ctx)(context)
    else:
        latent()


def _bias_table(rpb):
    w = GRID_W
    j = np.arange(w)
    col_start = np.clip(j - WIN_W // 2, 0, w - WIN_W)
    in_win = (j[None, :] >= col_start[:, None]) & (j[None, :] < col_start[:, None] + WIN_W)
    rpb = rpb.astype(F32)
    ext = jnp.concatenate([jnp.repeat(rpb[..., :1], w - WIN_W, axis=-1), rpb,
                           jnp.repeat(rpb[..., -1:], w - WIN_W, axis=-1)], axis=-1)
    toe = jnp.stack([ext[..., w - 1 - qc:2 * w - 1 - qc] for qc in range(w)], axis=2)
    tz = jnp.where(in_win, toe, NEG_BIG)
    return jnp.stack([jnp.concatenate([tz[:, v + a] for a in range(WIN_H)], axis=-1) for v in range(WIN_H)],
                     axis=0)


def _attn_call(qkv, bias_tab, *, n_ctx, with_ctx):
    bsz, tt, d3 = qkv.shape
    d_b = d3 // 3
    heads = d_b // HEAD_DIM
    rows = (tt - n_ctx) // GRID_W
    nq_ctx = n_ctx // GRID_W
    q_off = 0 if with_ctx else nq_ctx
    n_q = tt // GRID_W - q_off
    nb = next(m for m in (4, 2, 1) if bsz % m == 0)

    def variant(b, i):
        li = jnp.maximum(i + q_off - nq_ctx, 0)
        return (jnp.clip(li - WIN_H // 2, 0, rows - WIN_H) - li + WIN_H - 1, 0, 0, 0)

    return pl.pallas_call(
        functools.partial(_attn_kernel, n_ctx=n_ctx, rows=rows, q_off=q_off, d_b=d_b),
        out_shape=jax.ShapeDtypeStruct((bsz, n_q * GRID_W, d_b), BF16),
        grid=(bsz // nb, n_q),
        in_specs=[pl.BlockSpec((nb, GRID_W, d_b), lambda b, i: (b, i + q_off, 0)),
                  pl.BlockSpec((nb, tt, d_b), lambda b, i: (b, 0, 1)),
                  pl.BlockSpec((nb, tt, d_b), lambda b, i: (b, 0, 2)),
                  pl.BlockSpec((1, heads, GRID_W, WIN_H * GRID_W), variant)],
        out_specs=pl.BlockSpec((nb, GRID_W, d_b), lambda b, i: (b, i, 0)),
        compiler_params=pltpu.CompilerParams(dimension_semantics=("parallel", "arbitrary"),
                                             vmem_limit_bytes=VMEM_LIMIT),
        name="neighbourhood_attention",
    )(qkv, qkv, qkv, bias_tab)


def _merge_kernel(xc_ref, x_ref, mod_ref, y0_ref, y1_ref, rkv_ref, lg_ref, yb_ref, gates_ref,
                  rk_ref, lng_ref, lnb_ref, glb_ref, pa_ref, pb_ref, wo_ref, gs_ref, o_ref,
                  *, tm, n_ctx, t_off, d_a):
    t = pl.program_id(1) + t_off
    is_ctx = _ctx_rows(t, tm, n_ctx)
    gs = gs_ref[...]
    inv_n = 1.0 / HEAD_DIM
    y = y0_ref[0] + y1_ref[0]
    mu = _group_sum2(y, gs) * inv_n
    yc = y - mu
    var = _group_sum2(yc * yc, gs) * inv_n
    yn = yc * lax.rsqrt(var + GN_EPS) * lng_ref[...] + lnb_ref[...]
    rkv = rkv_ref[0]
    r, k, v = rkv[:, :d_a], rkv[:, d_a:2 * d_a], rkv[:, 2 * d_a:]
    bonus = _group_sum2(r * k * rk_ref[...], gs) * v
    g = _bdot(_sigmoid(lg_ref[0]), glb_ref[...])
    ya = (yn + bonus) * g
    gates = gates_ref[0].astype(F32)
    dm = gates.shape[1] // 2
    merged = gates[:, :dm] * _bdot(ya, pa_ref[...]) + gates[:, dm:] * _bdot(yb_ref[0], pb_ref[...])
    o_ref[0] = (_token_tile(xc_ref, x_ref, t, tm, n_ctx)
                + _mod_row(mod_ref, 2, is_ctx) * _bdot(merged, wo_ref[...]))


def _merge_call(xs, modsel, y_f, y_r, rkv, lg, yb, gates, r_k, ln_g, ln_b, glb, pa, pb, wo, gs,
                *, n_ctx, with_ctx, tm=256):
    bsz, tt, d_a3 = rkv.shape
    d_a = d_a3 // 3
    t_off = 0 if with_ctx else n_ctx // tm
    n_t = tt // tm - t_off
    tok_arrays, tok_specs = _token_sources(xs, tm, n_ctx, t_off)
    d = tok_arrays[1].shape[2]
    yb_off = 0 if with_ctx else -t_off
    row = lambda b, t: (b, t + t_off, 0)
    const = lambda b, t: (0, 0)
    return pl.pallas_call(
        functools.partial(_merge_kernel, tm=tm, n_ctx=n_ctx, t_off=t_off, d_a=d_a),
        out_shape=jax.ShapeDtypeStruct((bsz, n_t * tm, d), F32),
        grid=(bsz, n_t),
        in_specs=tok_specs + [
                  pl.BlockSpec((1, 2, 8, d), lambda b, t: (b, 0, 0, 0)),
                  pl.BlockSpec((1, tm, d_a), row),
                  pl.BlockSpec((1, tm, d_a), row),
                  pl.BlockSpec((1, tm, 3 * d_a), row),
                  pl.BlockSpec((1, tm, lg.shape[2]), row),
                  pl.BlockSpec((1, tm, yb.shape[2]), lambda b, t: (b, t + t_off + yb_off, 0)),
                  pl.BlockSpec((1, tm, gates.shape[2]), row),
                  pl.BlockSpec((1, d_a), const),
                  pl.BlockSpec((1, d_a), const),
                  pl.BlockSpec((1, d_a), const),
                  pl.BlockSpec(glb.shape, const),
                  pl.BlockSpec(pa.shape, const),
                  pl.BlockSpec(pb.shape, const),
                  pl.BlockSpec(wo.shape, const),
                  pl.BlockSpec((LANE, LANE), const)],
        out_specs=pl.BlockSpec((1, tm, d), lambda b, t: (b, t, 0)),
        compiler_params=pltpu.CompilerParams(dimension_semantics=("parallel", "parallel"),
                                             vmem_limit_bytes=VMEM_LIMIT),
        name="branch_merge",
    )(*tok_arrays, modsel, y_f, y_r, rkv, lg, yb, gates, r_k, ln_g, ln_b, glb, pa, pb, wo, gs)


def _route(sel, scores, n_experts):
    per = n_experts // N_GROUPS
    in_top = []
    for e in range(n_experts):
        g0 = (e // per) * per
        rank = 0.0
        for o in range(g0, g0 + per):
            if o == e:
                continue
            ahead = (sel[o] >= sel[e]) if o < e else (sel[o] > sel[e])
            rank = rank + jnp.where(ahead, 1.0, 0.0)
        in_top.append(rank < TOP_K)
    grp = [sum(jnp.where(in_top[e], sel[e], 0.0) for e in range(g * per, (g + 1) * per))
           for g in range(N_GROUPS)]
    best = jnp.zeros_like(grp[0], dtype=jnp.int32)
    best_s = grp[0]
    for g in range(1, N_GROUPS):
        better = grp[g] > best_s
        best = jnp.where(better, g, best)
        best_s = jnp.where(better, grp[g], best_s)
    chosen = [in_top[e] & (best == e // per) for e in range(n_experts)]
    den = sum(jnp.where(chosen[e], scores[e], 0.0) for e in range(n_experts))
    return [jnp.where(chosen[e], scores[e] / den, 0.0) for e in range(n_experts)], best


MOE_EXPERTS_PER_STEP = 2
MOE_BLK = 128
POS_COL = 16


def _moe_route_kernel(x_ref, mod_ref, g_ref, rw_ref, rb_ref, hs_ref, gs_ref, tab_ref, cnt_ref,
                      *, tm, ts, n_ctx, n_experts):
    t = pl.program_id(1)
    is_ctx = _ctx_rows(t, tm, n_ctx)
    h = _rms_mod(x_ref[0], g_ref[...], _mod_row(mod_ref, 3, is_ctx), _mod_row(mod_ref, 4, is_ctx))
    hh, hl = _split(h)
    wh, wl = _split(rw_ref[...])
    nt = lambda a, b: lax.dot_general(a, b, (((1,), (1,)), ((), ())), preferred_element_type=F32)
    logits = nt(wh, hh) + nt(wh, hl) + nt(wl, hh)
    scores = _sigmoid(logits)
    sel = scores + rb_ref[...]
    gates, best = _route([sel[i:i + 1, :] for i in range(n_experts)],
                         [scores[i:i + 1, :] for i in range(n_experts)], n_experts)

    in_grp = [jnp.where(best == g, 1.0, 0.0) for g in range(N_GROUPS)]
    grp8 = jnp.concatenate(in_grp + [jnp.zeros((8 - N_GROUPS, tm), F32)], axis=0).astype(BF16)
    row = lax.broadcasted_iota(jnp.int32, (tm, tm), 0)
    col = lax.broadcasted_iota(jnp.int32, (tm, tm), 1)
    upper = jnp.where(row <= col, 1.0, 0.0).astype(BF16)
    run = jnp.dot(grp8, upper, preferred_element_type=F32)
    pos = jnp.zeros((1, tm), F32)
    seg = jnp.zeros((1, 1), F32)
    cnts = []
    for g in range(N_GROUPS):
        cnt = run[g:g + 1, tm - 1:tm]
        cnts.append(cnt)
        pos = pos + in_grp[g] * (seg + run[g:g + 1, :] - 1.0)
        seg = seg + jnp.floor((cnt + (MOE_BLK - 1)) * (1.0 / MOE_BLK)) * MOE_BLK
    cnt_ref[0, 0] = jnp.concatenate([jnp.broadcast_to(c, (1, LANE)) for c in cnts]
                                    + [jnp.zeros((8 - N_GROUPS, LANE), F32)], axis=0)
    tab = jnp.concatenate(gates + [pos, jnp.zeros((LANE - n_experts - 1, tm), F32)], axis=0).T
    tab_ref[0] = tab
    sel = jnp.where(lax.broadcasted_iota(jnp.int32, (ts, tm), 0) == pos.astype(jnp.int32), 1.0, 0.0).astype(BF16)
    hs_ref[0, 0] = jnp.dot(sel, h.astype(BF16), preferred_element_type=F32).astype(BF16)
    t1 = tab.astype(BF16)
    r1 = tab - t1.astype(F32)
    t2 = r1.astype(BF16)
    t3 = (r1 - t2.astype(F32)).astype(BF16)
    gs_ref[0, 0] = (jnp.dot(sel, t1, preferred_element_type=F32) + jnp.dot(sel, t2, preferred_element_type=F32)
                    + jnp.dot(sel, t3, preferred_element_type=F32))


def _moe_expert_kernel(lo_ref, hi_ref, x_ref, mod_ref, hs_ref, gs_ref, tab_ref, w1_ref, w3_ref, w2_ref, o_ref,
                       acc_ref, *, tm, ts, n_ctx, n_experts):
    b = pl.program_id(0)
    t = pl.program_id(1)
    e = pl.program_id(2)
    n_t = pl.num_programs(1)

    @pl.when(e == 0)
    def _():
        acc_ref[...] = jnp.zeros_like(acc_ref)

    per_step = w1_ref.shape[1]
    e_first = e * per_step
    seg = (b * n_t + t) * N_GROUPS + e_first // (n_experts // N_GROUPS)
    lane = lax.broadcasted_iota(jnp.int32, (MOE_BLK, LANE), 1)

    def block(j, carry):
        rows = pl.ds(pl.multiple_of(j * MOE_BLK, MOE_BLK), MOE_BLK)
        hb = hs_ref[0, 0, rows, :]
        gs = gs_ref[0, 0, rows, :]
        hids = []
        for i in range(per_step):
            a = jnp.dot(hb, w1_ref[0, i], preferred_element_type=F32)
            bb = jnp.dot(hb, w3_ref[0, i], preferred_element_type=F32)
            gcol = jnp.sum(jnp.where(lane == e_first + i, gs, 0.0), axis=-1, keepdims=True)
            hids.append((a * _sigmoid(a) * bb * gcol).astype(BF16))
        w2 = w2_ref[0].reshape(per_step * w2_ref.shape[2], w2_ref.shape[3])
        acc_ref[rows, :] += jnp.dot(jnp.concatenate(hids, axis=1), w2, preferred_element_type=F32)
        return carry

    lax.fori_loop(lo_ref[seg], hi_ref[seg], block, 0)

    @pl.when(e == pl.num_programs(2) - 1)
    def _():
        is_ctx = _ctx_rows(t, tm, n_ctx)
        pos = tab_ref[0][:, POS_COL:POS_COL + 1].astype(jnp.int32)
        back = jnp.where(lax.broadcasted_iota(jnp.int32, (tm, ts), 1) == pos, 1.0, 0.0).astype(BF16)
        y = jnp.dot(back, acc_ref[...].astype(BF16), preferred_element_type=F32)
        o_ref[0] = x_ref[0] + _mod_row(mod_ref, 5, is_ctx) * y


def _moe_call(xs, modsel, norm_g, router_wt, router_b, w1, w3, w2, *, layer, n_ctx, tm):
    bsz, tt, d = xs.shape
    n_experts = router_wt.shape[0]
    n_t = tt // tm
    ts = tm + N_GROUPS * MOE_BLK
    row2 = lambda b, t: (b, t, 0)
    const2 = lambda b, t: (0, 0)
    tile2 = lambda b, t: (b, t, 0, 0)
    hs, gs, tab, cnt = pl.pallas_call(
        functools.partial(_moe_route_kernel, tm=tm, ts=ts, n_ctx=n_ctx, n_experts=n_experts),
        out_shape=(jax.ShapeDtypeStruct((bsz, n_t, ts, d), BF16),
                   jax.ShapeDtypeStruct((bsz, n_t, ts, LANE), F32),
                   jax.ShapeDtypeStruct((bsz, tt, LANE), F32),
                   jax.ShapeDtypeStruct((bsz, n_t, 8, LANE), F32)),
        grid=(bsz, n_t),
        in_specs=[pl.BlockSpec((1, tm, d), row2),
                  pl.BlockSpec((1, 2, 8, d), lambda b, t: (b, 0, 0, 0)),
                  pl.BlockSpec((1, d), const2),
                  pl.BlockSpec(router_wt.shape, const2),
                  pl.BlockSpec((n_experts, 1), const2)],
        out_specs=(pl.BlockSpec((1, 1, ts, d), tile2),
                   pl.BlockSpec((1, 1, ts, LANE), tile2),
                   pl.BlockSpec((1, tm, LANE), row2),
                   pl.BlockSpec((1, 1, 8, LANE), tile2)),
        compiler_params=pltpu.CompilerParams(dimension_semantics=("parallel", "parallel"),
                                             vmem_limit_bytes=VMEM_LIMIT),
        name="moe_route",
    )(xs, modsel, norm_g, router_wt, router_b)

    n_blk = (cnt[:, :, :N_GROUPS, 0].astype(jnp.int32) + (MOE_BLK - 1)) // MOE_BLK
    hi = jnp.cumsum(n_blk, axis=-1)
    lo = hi - n_blk

    row = lambda b, t, e, lo_r, hi_r: (b, t, 0)
    wexp = lambda b, t, e, lo_r, hi_r: (layer, e, 0, 0)
    tile4 = lambda b, t, e, lo_r, hi_r: (b, t, 0, 0)
    return pl.pallas_call(
        functools.partial(_moe_expert_kernel, tm=tm, ts=ts, n_ctx=n_ctx, n_experts=n_experts),
        out_shape=jax.ShapeDtypeStruct((bsz, tt, d), F32),
        grid_spec=pltpu.PrefetchScalarGridSpec(
            num_scalar_prefetch=2,
            grid=(bsz, n_t, n_experts // MOE_EXPERTS_PER_STEP),
            in_specs=[pl.BlockSpec((1, tm, d), row),
                      pl.BlockSpec((1, 2, 8, d), lambda b, t, e, lo_r, hi_r: (b, 0, 0, 0)),
                      pl.BlockSpec((1, 1, ts, d), tile4),
                      pl.BlockSpec((1, 1, ts, LANE), tile4),
                      pl.BlockSpec((1, tm, LANE), row),
                      pl.BlockSpec((1, MOE_EXPERTS_PER_STEP) + w1.shape[2:], wexp),
                      pl.BlockSpec((1, MOE_EXPERTS_PER_STEP) + w3.shape[2:], wexp),
                      pl.BlockSpec((1, MOE_EXPERTS_PER_STEP) + w2.shape[2:], wexp)],
            out_specs=pl.BlockSpec((1, tm, d), row),
            scratch_shapes=[pltpu.VMEM((ts, d), F32)]),
        compiler_params=pltpu.CompilerParams(
            dimension_semantics=("parallel", "parallel", "arbitrary"),
            vmem_limit_bytes=VMEM_LIMIT),
        name="moe_experts",
    )(lo.reshape(-1), hi.reshape(-1), xs, modsel, hs, gs, tab, w1, w3, w2)


def kernel(x, c, ctx, c_ctx, mod_w, mod_b, norm1_g, norm2_g, w_in, rw_w0, rw_w_lora_b, rw_a0, rw_a_lora_b,
           rw_g_lora_b, rw_k_k, rw_k_a, rw_r_k, rw_ln_g, rw_ln_b, na_q_g, na_k_g, na_rpb, proj_a, proj_b,
           w_out, router_w, router_bias, moe_w1, moe_w3, moe_w2):
    bsz, seq, d = x.shape
    n_ctx = ctx.shape[1]
    depth = mod_w.shape[0]
    d_a = rw_w0.shape[2]
    d_b = proj_b.shape[1]
    lora_g = rw_g_lora_b.shape[1]
    heads_b = d_b // HEAD_DIM
    assert seq % (GRID_W * WIN_H) == 0 and n_ctx % 256 == 0 and bsz + 1 <= 16

    cs = jnp.zeros((16, d), F32).at[:bsz].set(c).at[bsz].set(c_ctx)
    mod = _mod_call(cs, mod_w, mod_b).reshape(depth, 16, 6, d)

    lane = jnp.arange(LANE)
    gs = (lane[:, None] // HEAD_DIM == lane[None, :] // HEAD_DIM).astype(BF16)
    router_wt = router_w.T
    router_b = router_bias.reshape(-1, 1)

    w_in_b, w1_b, w3_b, w2_b = (_to_bf16(w) for w in (w_in, moe_w1, moe_w3, moe_w2))
    xs = (ctx, x)
    for l in range(depth):
        last = l == depth - 1
        m_c = jnp.broadcast_to(mod[l, bsz][None], (bsz, 6, d))
        modsel = jnp.pad(jnp.stack([m_c, mod[l, :bsz]], axis=1), ((0, 0), (0, 0), (0, 2), (0, 0)))

        rkv, lg, qkv, gates, op_f, op_r, v_a, pe_f, pe_r = _in_proj_call(
            xs, modsel, norm1_g[l][None], w_in_b, rw_k_k[l][None],
            jnp.tile(na_q_g[l], heads_b)[None], jnp.tile(na_k_g[l], heads_b)[None], gs,
            rw_w0[l], rw_w_lora_b[l], rw_a0[l], rw_a_lora_b[l], rw_k_a[l][None],
            layer=l, n_ctx=n_ctx, d_a=d_a, d_b=d_b, lora_g=lora_g)
        y_f, y_r = _rwkv_call(op_f, op_r, v_a, pe_f, pe_r, n_ctx=n_ctx)
        yb = _attn_call(qkv, _bias_table(na_rpb[l]), n_ctx=n_ctx, with_ctx=not last)
        xs = _merge_call(xs, modsel, y_f, y_r, rkv, lg, yb, gates, rw_r_k[l].reshape(1, d_a),
                         rw_ln_g[l][None], rw_ln_b[l][None], rw_g_lora_b[l].astype(BF16),
                         proj_a[l].astype(BF16), proj_b[l].astype(BF16), w_out[l].astype(BF16), gs,
                         n_ctx=n_ctx, with_ctx=not last)
        moe_ctx = 0 if last else n_ctx
        tm = next(m for m in (1024, 768, 512, 256) if xs.shape[1] % m == 0)
        xs = _moe_call(xs, modsel, norm2_g[l][None], router_wt, router_b, w1_b, w3_b, w2_b,
                       layer=l, n_ctx=moe_ctx, tm=tm)
    return xs
```

```python
import functools
import math

import jax
import jax.numpy as jnp
import numpy as np
from jax import lax
from jax.experimental import pallas as pl
from jax.experimental.pallas import tpu as pltpu

F32 = jnp.float32
BF16 = jnp.bfloat16

HEAD_DIM = 64
GRID_W = 64
WIN_H = 8
WIN_W = 16
N_GROUPS = 4
TOP_K = 2
RMS_EPS = 1e-6
GN_EPS = 64e-5
LANE = 128
CHUNK = 64
SUB = 16
NEG_BIG = -1e30
DECAY_SCALE = math.exp(-0.5)
VMEM_LIMIT = 56 * 1024 * 1024


def _bdot(a, b):
    return jnp.dot(a.astype(BF16), b.astype(BF16), preferred_element_type=F32)


def _bdot_nt(a, b):
    return lax.dot_general(a.astype(BF16), b.astype(BF16), (((1,), (1,)), ((), ())),
                           preferred_element_type=F32)


def _bdot_tn(a, b):
    return lax.dot_general(a.astype(BF16), b.astype(BF16), (((0,), (0,)), ((), ())),
                           preferred_element_type=F32)


def _split(x):
    hi = x.astype(BF16)
    lo = (x - hi.astype(F32)).astype(BF16)
    return hi, lo


def _dot3(a, b):
    ah, al = _split(a)
    bh, bl = _split(b)
    return (jnp.dot(ah, bh, preferred_element_type=F32)
            + jnp.dot(al, bh, preferred_element_type=F32)
            + jnp.dot(ah, bl, preferred_element_type=F32))


def _sigmoid(x):
    return 1.0 / (1.0 + jnp.exp(-x))


def _group_sum(x, g128):
    parts = [_bdot(x[:, j * LANE:(j + 1) * LANE], g128) for j in range(x.shape[1] // LANE)]
    return jnp.concatenate(parts, axis=1)


def _group_sum2(x, g128):
    hi, lo = _split(x)
    parts = []
    for j in range(x.shape[1] // LANE):
        sl = slice(j * LANE, (j + 1) * LANE)
        parts.append(jnp.dot(hi[:, sl], g128, preferred_element_type=F32)
                     + jnp.dot(lo[:, sl], g128, preferred_element_type=F32))
    return jnp.concatenate(parts, axis=1)


def _mod_row(mod_ref, idx, is_ctx):
    mx = mod_ref[0, 1, idx:idx + 1, :]
    if is_ctx is None:
        return mx
    return jnp.where(is_ctx, mod_ref[0, 0, idx:idx + 1, :], mx)


def _ctx_rows(tile, tm, n_ctx):
    if n_ctx == 0:
        return None
    rows = tile * tm + lax.broadcasted_iota(jnp.int32, (tm, 1), 0)
    return rows < n_ctx


def _token_sources(src, tm, n_ctx, t_off):
    assert n_ctx % tm == 0
    n_ct = n_ctx // tm
    if isinstance(src, tuple):
        ctx, x = src
        d = x.shape[2]
        lat = pl.BlockSpec((1, tm, d), lambda b, t: (b, jnp.maximum(t + t_off - n_ct, 0), 0))
    else:
        ctx = x = src
        d = x.shape[2]
        lat = pl.BlockSpec((1, tm, d), lambda b, t: (b, t + t_off, 0))
    con = pl.BlockSpec((1, tm, d), lambda b, t: (b, jnp.minimum(t + t_off, n_ct - 1), 0))
    return (ctx, x), [con, lat]


def _token_tile(ctx_ref, lat_ref, tile, tm, n_ctx):
    return jnp.where(tile * tm < n_ctx, ctx_ref[0], lat_ref[0])


def _rms_mod(x, gain, shift, scale):
    xn = x * lax.rsqrt(jnp.mean(x * x, axis=-1, keepdims=True) + RMS_EPS) * gain
    return xn * (1.0 + scale) + shift


CAST_BLOCK_BYTES = 8 * 1024 * 1024


def _cast_kernel(w_ref, o_ref):
    o_ref[...] = w_ref[...].astype(o_ref.dtype)


def _to_bf16(w):
    c = w.shape[-1]
    w2 = w.reshape(-1, c)
    rows = w2.shape[0]
    rb = max(16, min(rows, CAST_BLOCK_BYTES // (c * 4)) // 16 * 16)
    while rows % rb:
        rb -= 16
    out = pl.pallas_call(
        _cast_kernel,
        out_shape=jax.ShapeDtypeStruct(w2.shape, BF16),
        grid=(rows // rb,),
        in_specs=[pl.BlockSpec((rb, c), lambda i: (i, 0))],
        out_specs=pl.BlockSpec((rb, c), lambda i: (i, 0)),
        compiler_params=pltpu.CompilerParams(dimension_semantics=("parallel",), vmem_limit_bytes=VMEM_LIMIT),
        name="to_bf16",
    )(w2)
    return out.reshape(w.shape)


def _mod_kernel(c_ref, w_ref, b_ref, o_ref):
    c = c_ref[...]
    o_ref[0] = _dot3(c * _sigmoid(c), w_ref[0]) + b_ref[0]


def _mod_call(cs, mod_w, mod_b):
    depth, d, n = mod_w.shape
    tn = n // 4
    return pl.pallas_call(
        _mod_kernel,
        out_shape=jax.ShapeDtypeStruct((depth, cs.shape[0], n), F32),
        grid=(depth, n // tn),
        in_specs=[pl.BlockSpec(cs.shape, lambda l, j: (0, 0)),
                  pl.BlockSpec((1, d, tn), lambda l, j: (l, 0, j)),
                  pl.BlockSpec((1, 1, tn), lambda l, j: (l, 0, j))],
        out_specs=pl.BlockSpec((1, cs.shape[0], tn), lambda l, j: (l, 0, j)),
        compiler_params=pltpu.CompilerParams(dimension_semantics=("parallel", "parallel"),
                                             vmem_limit_bytes=VMEM_LIMIT),
        name="adaln_vectors",
    )(cs, mod_w, mod_b.reshape(depth, 1, n))


def _in_proj_kernel(xc_ref, x_ref, mod_ref, g_ref, w_ref, kkg_ref, qg_ref, kg_ref, gs_ref,
                    w0_ref, wlbh_ref, wlbl_ref, a0_ref, albh_ref, albl_ref, ka_ref,
                    rkv_ref, lg_ref, qkv_ref, gates_ref, opf_ref, opr_ref, v_ref, pef_ref, per_ref,
                    *, tm, n_ctx, d_a, d_b, lora_w, lora_a, lora_g):
    t = pl.program_id(1)
    is_ctx = _ctx_rows(t, tm, n_ctx)
    h = _rms_mod(_token_tile(xc_ref, x_ref, t, tm, n_ctx), g_ref[...],
                 _mod_row(mod_ref, 0, is_ctx), _mod_row(mod_ref, 1, is_ctx))
    h = h.astype(BF16)
    gs = gs_ref[...]
    n_lora = 2 * lora_w + 2 * lora_a + lora_g

    o = 0
    rkv = jnp.dot(h, w_ref[0, :, o:o + 3 * d_a], preferred_element_type=F32)
    rkv_ref[0] = rkv
    kkv = rkv[:, d_a:2 * d_a] * kkg_ref[...]
    ss = _group_sum2(kkv * kkv, gs)
    kk = kkv * lax.rsqrt(jnp.maximum(ss, 1e-12))
    o += 3 * d_a

    lora = jnp.dot(h, w_ref[0, :, o:o + n_lora], preferred_element_type=F32)
    lg_ref[0] = lora[:, n_lora - lora_g:]
    v_ref[0] = rkv[:, 2 * d_a:].astype(BF16)
    o += n_lora

    qkv = jnp.dot(h, w_ref[0, :, o:o + 3 * d_b], preferred_element_type=F32)
    q, k = qkv[:, :d_b], qkv[:, d_b:2 * d_b]
    inv_n = 1.0 / HEAD_DIM
    qn = q * lax.rsqrt(_group_sum(q * q, gs) * inv_n + RMS_EPS) * qg_ref[...] * (HEAD_DIM ** -0.5)
    kn = k * lax.rsqrt(_group_sum(k * k, gs) * inv_n + RMS_EPS) * kg_ref[...]
    qkv_ref[0] = jnp.concatenate([qn, kn, qkv[:, 2 * d_b:]], axis=1).astype(BF16)
    o += 3 * d_b

    gates_ref[0] = _sigmoid(jnp.dot(h, w_ref[0, :, o:], preferred_element_type=F32)).astype(BF16)
    _rwkv_operand_rows((0, 1), rkv, kk, lora, w0_ref, wlbh_ref, wlbl_ref, a0_ref, albh_ref, albl_ref, ka_ref,
                       opf_ref, opr_ref, pef_ref, per_ref, tm=tm, d_a=d_a, lora_w=lora_w, lora_a=lora_a)


def _in_proj_call(xs, modsel, norm_g, w_in, k_k, q_g, k_g, gs, w0, wlb, a0, alb, k_a,
                  *, layer, n_ctx, d_a, d_b, lora_g, tm=256):
    tok_arrays, tok_specs = _token_sources(xs, tm, n_ctx, 0)
    bsz, d = tok_arrays[1].shape[0], tok_arrays[1].shape[2]
    tt = sum(a.shape[1] for a in xs) if isinstance(xs, tuple) else xs.shape[1]
    lora_w, lora_a = wlb.shape[1], alb.shape[1]
    n_lora = 2 * lora_w + 2 * lora_a + lora_g
    n_gate = w_in.shape[2] - 3 * d_a - n_lora - 3 * d_b
    row = lambda b, t: (b, t, 0)
    const = lambda b, t: (0, 0)
    const3 = lambda b, t: (0, 0, 0)
    wlb_hi, wlb_lo = _split(wlb)
    alb_hi, alb_lo = _split(alb)
    op_shape = jax.ShapeDtypeStruct((bsz, tt, N_OPERANDS * d_a), BF16)
    op_spec = pl.BlockSpec((1, tm, N_OPERANDS * d_a), row)
    pe_shape = jax.ShapeDtypeStruct((bsz, tt // CHUNK, 8, d_a), F32)
    pe_spec = pl.BlockSpec((1, tm // CHUNK, 8, d_a), lambda b, t: (b, t, 0, 0))
    return pl.pallas_call(
        functools.partial(_in_proj_kernel, tm=tm, n_ctx=n_ctx, d_a=d_a, d_b=d_b,
                          lora_w=lora_w, lora_a=lora_a, lora_g=lora_g),
        out_shape=(jax.ShapeDtypeStruct((bsz, tt, 3 * d_a), F32),
                   jax.ShapeDtypeStruct((bsz, tt, lora_g), F32),
                   jax.ShapeDtypeStruct((bsz, tt, 3 * d_b), BF16),
                   jax.ShapeDtypeStruct((bsz, tt, n_gate), BF16),
                   op_shape, op_shape, jax.ShapeDtypeStruct((bsz, tt, d_a), BF16), pe_shape, pe_shape),
        grid=(bsz, tt // tm),
        in_specs=tok_specs + [
                  pl.BlockSpec((1, 2, 8, d), lambda b, t: (b, 0, 0, 0)),
                  pl.BlockSpec((1, d), const),
                  pl.BlockSpec((1,) + w_in.shape[1:], lambda b, t: (layer, 0, 0)),
                  pl.BlockSpec((1, d_a), const),
                  pl.BlockSpec((1, d_b), const),
                  pl.BlockSpec((1, d_b), const),
                  pl.BlockSpec((LANE, LANE), const),
                  pl.BlockSpec((2, 1, d_a), const3),
                  pl.BlockSpec((2, lora_w, d_a), const3),
                  pl.BlockSpec((2, lora_w, d_a), const3),
                  pl.BlockSpec((2, 1, d_a), const3),
                  pl.BlockSpec((2, lora_a, d_a), const3),
                  pl.BlockSpec((2, lora_a, d_a), const3),
                  pl.BlockSpec((1, d_a), const)],
        out_specs=(pl.BlockSpec((1, tm, 3 * d_a), row),
                   pl.BlockSpec((1, tm, lora_g), row),
                   pl.BlockSpec((1, tm, 3 * d_b), row),
                   pl.BlockSpec((1, tm, n_gate), row),
                   op_spec, op_spec, pl.BlockSpec((1, tm, d_a), row), pe_spec, pe_spec),
        compiler_params=pltpu.CompilerParams(dimension_semantics=("parallel", "parallel"),
                                             vmem_limit_bytes=VMEM_LIMIT),
        name="in_proj",
    )(*tok_arrays, modsel, norm_g, w_in, k_k, q_g, k_g, gs,
      w0.reshape(2, 1, d_a), wlb_hi, wlb_lo, a0.reshape(2, 1, d_a), alb_hi, alb_lo, k_a)


N_OPERANDS = 4


def _dot3_presplit(x, w_hi, w_lo):
    xh, xl = _split(x)
    return (jnp.dot(xh, w_hi, preferred_element_type=F32) + jnp.dot(xl, w_hi, preferred_element_type=F32)
            + jnp.dot(xh, w_lo, preferred_element_type=F32))


def _rwkv_operand_rows(dirs, rkv, kk, lora, w0_ref, wlbh_ref, wlbl_ref, a0_ref, albh_ref, albl_ref, ka_ref,
                       opf_ref, opr_ref, pef_ref, per_ref, *, tm, d_a, lora_w, lora_a):
    r, k = rkv[:, :d_a], rkv[:, d_a:2 * d_a]
    chunk_bits = int(math.log2(CHUNK))
    row = lax.broadcasted_iota(jnp.int32, (tm, tm), 0)
    col = lax.broadcasted_iota(jnp.int32, (tm, tm), 1)
    same = (row >> chunk_bits) == (col >> chunk_bits)
    for d in dirs:
        op_ref, pe_ref = ((opf_ref, pef_ref), (opr_ref, per_ref))[d]
        lw = lora[:, d * lora_w:(d + 1) * lora_w]
        la = lora[:, 2 * lora_w + d * lora_a:2 * lora_w + (d + 1) * lora_a]
        wl = w0_ref[d] + _dot3_presplit(jnp.tanh(lw), wlbh_ref[d], wlbl_ref[d])
        ld = -DECAY_SCALE * _sigmoid(wl)
        al = _sigmoid(a0_ref[d] + _dot3_presplit(la, albh_ref[d], albl_ref[d]))
        kd = k * (1.0 + (al - 1.0) * ka_ref[...])
        before = (row <= col) if d else (row >= col)
        tri = jnp.where(same & before, 1.0, 0.0).astype(BF16)
        ld_hi, ld_lo = _split(ld)
        cum = jnp.dot(tri, ld_hi, preferred_element_type=F32) + jnp.dot(tri, ld_lo, preferred_element_type=F32)
        p_inv = jnp.exp(-cum)
        op_ref[0] = jnp.concatenate([-kk * jnp.exp(cum - ld), r * jnp.exp(cum), kk * al * p_inv, kd * p_inv],
                                    axis=1).astype(BF16)
        for ci in range(tm // CHUNK):
            last = ci * CHUNK + (0 if d else CHUNK - 1)
            pe_ref[0, ci] = jnp.exp(jnp.broadcast_to(cum[last:last + 1, :], (8, d_a)))


def _rwkv_pair_kernel(opf_ref, opr_ref, vf_ref, vr_ref, pef_ref, per_ref, yf_ref, yr_ref, st_ref, *, d_a):
    n = HEAD_DIM
    pw = 2 * n
    pairs = d_a // pw
    c = CHUNK
    nb = opf_ref.shape[0]

    @pl.when(pl.program_id(1) == 0)
    def _():
        st_ref[...] = jnp.zeros_like(st_ref)

    chains = [(b, d, p) for b in range(nb) for d in range(2) for p in range(pairs)]
    op_refs = (opf_ref, opr_ref)
    part = lambda i: [op_refs[d][b, :, i * d_a + p * pw:i * d_a + (p + 1) * pw] for b, d, p in chains]
    v_refs = (vf_ref, vr_ref)
    pe_refs = (pef_ref, per_ref)
    v = [v_refs[d][b, :, p * pw:(p + 1) * pw] for b, d, p in chains]
    p_end = [pe_refs[d][b, 0, 0:1, p * pw:(p + 1) * pw] for b, d, p in chains]
    each = lambda f, *ls: [f(*xs) for xs in zip(*ls)]
    rows2 = lambda a, b: jnp.concatenate([a, b], axis=0)
    bf = lambda t: t.astype(BF16)

    def bd(x):
        x = bf(x)
        left = lax.broadcasted_iota(jnp.int32, x.shape, 1) < n
        zero = jnp.zeros_like(x)
        return rows2(jnp.where(left, x, zero), jnp.where(left, zero, x))

    pdot = lambda a, b: jnp.dot(bf(a), bd(b), preferred_element_type=F32)
    ar = each(rows2, part(0), part(1))
    bk = each(rows2, part(2), part(3))
    b_k = each(lambda w: rows2(bd(w[:c]), bd(w[c:])), bk)
    s = [st_ref[b, d, p] for b, d, p in chains]

    mm_all = each(lambda a, w, ss: _bdot_nt(a, rows2(w, bd(ss))), ar, b_k, s)
    row = lax.broadcasted_iota(jnp.int32, (c, 2 * pw), 0)
    col = lax.broadcasted_iota(jnp.int32, (c, 2 * pw), 1) & (c - 1)
    incl = [(row <= col) if d else (row >= col) for _, d, _ in chains]
    strict = [(row < col) if d else (row > col) for _, d, _ in chains]
    m_a = each(lambda mm, msk: jnp.where(msk, mm[:c, :2 * pw], 0.0), mm_all, strict)
    m_r = each(lambda mm, msk: jnp.where(msk, mm[c:, :2 * pw], 0.0), mm_all, incl)
    ms_a = each(lambda mm: mm[:c, 2 * pw:], mm_all)
    ms_r = each(lambda mm: mm[c:, 2 * pw:], mm_all)
    a_ab = each(lambda mm: mm[:, :pw], m_a)
    a_ak = each(lambda mm: mm[:, pw:], m_a)

    row = lax.broadcasted_iota(jnp.int32, (c, pw), 0)
    col = lax.broadcasted_iota(jnp.int32, (c, pw), 1) & (c - 1)
    eye = jnp.where(row == col, 1.0, 0.0)
    sub_bits = int(math.log2(SUB))
    blk = (row >> sub_bits) == (col >> sub_bits)
    ad = each(lambda a: jnp.where(blk, a, 0.0), a_ab)
    ao = each(lambda a, b: a - b, a_ab, ad)
    td = each(lambda a: eye + a, ad)
    x = each(pdot, ad, ad)
    for _ in range(sub_bits - 2):
        res = each(lambda xx, tt: pdot(rows2(xx, tt), xx), x, td)
        td = each(lambda tt, rr: tt + rr[c:], td, res)
        x = each(lambda rr: rr[:c], res)
    td = each(lambda tt, xx: tt + pdot(tt, xx), td, x)
    x = each(pdot, td, ao)
    w = td
    for _ in range(int(math.log2(c // SUB)) - 1):
        res = each(lambda xx, ww: jnp.dot(bf(xx), jnp.concatenate([bd(xx), bd(ww)], axis=1),
                                          preferred_element_type=F32), x, w)
        w = each(lambda ww, rr: ww + rr[:, pw:], w, res)
        x = each(lambda rr: rr[:, :pw], res)
    t_inv = each(lambda ww, xx: ww + pdot(xx, ww), w, x)

    akv = each(pdot, a_ak, v)
    u = each(lambda t, m0, m1: bf(pdot(t, m0 + m1)), t_inv, ms_a, akv)
    y = each(lambda m0, a, uu, vv: m0 + jnp.dot(bf(a), rows2(bd(uu), bd(vv)), preferred_element_type=F32),
             ms_r, m_r, u, v)
    left = lax.broadcasted_iota(jnp.int32, (n, pw), 1) < n
    upd = each(lambda uu, vv, w: _bdot_tn(rows2(uu, vv), w), u, v, bk)
    s_new = each(lambda ss, p, dd: (ss + jnp.where(left, dd[:n], dd[n:])) * p, s, p_end, upd)
    for (b, d, p), val in zip(chains, s_new):
        st_ref[b, d, p] = val
    for b in range(nb):
        yf_ref[b] = jnp.concatenate(y[2 * b * pairs:(2 * b + 1) * pairs], axis=1)
        yr_ref[b] = jnp.concatenate(y[(2 * b + 1) * pairs:(2 * b + 2) * pairs], axis=1)


def _rwkv_call(op_f, op_r, v, pe_f, pe_r, *, n_ctx):
    bsz, tt, d_a = v.shape
    n_chunks = tt // CHUNK
    nc_ctx = n_ctx // CHUNK
    heads = d_a // HEAD_DIM

    rev_chunk = lambda j: jnp.where(j < nc_ctx, nc_ctx - 1 - j, n_chunks - 1 - (j - nc_ctx))
    fwd = lambda b, j: (b, j, 0)
    rev = lambda b, j: (b, rev_chunk(j), 0)
    y_shape = jax.ShapeDtypeStruct((bsz, tt, d_a), F32)
    nb = next(m for m in (4, 2, 1) if bsz % m == 0)
    return pl.pallas_call(
        functools.partial(_rwkv_pair_kernel, d_a=d_a),
        out_shape=(y_shape, y_shape),
        grid=(bsz // nb, n_chunks),
        in_specs=[pl.BlockSpec((nb, CHUNK, N_OPERANDS * d_a), fwd),
                  pl.BlockSpec((nb, CHUNK, N_OPERANDS * d_a), rev),
                  pl.BlockSpec((nb, CHUNK, d_a), fwd),
                  pl.BlockSpec((nb, CHUNK, d_a), rev),
                  pl.BlockSpec((nb, 1, 8, d_a), lambda b, j: (b, j, 0, 0)),
                  pl.BlockSpec((nb, 1, 8, d_a), lambda b, j: (b, rev_chunk(j), 0, 0))],
        out_specs=(pl.BlockSpec((nb, CHUNK, d_a), fwd), pl.BlockSpec((nb, CHUNK, d_a), rev)),
        scratch_shapes=[pltpu.VMEM((nb, 2, heads // 2, HEAD_DIM, 2 * HEAD_DIM), F32)],
        compiler_params=pltpu.CompilerParams(dimension_semantics=("parallel", "arbitrary"),
                                             vmem_limit_bytes=VMEM_LIMIT),
        name="rwkv7_chunk_scan",
    )(op_f, op_r, v, v, pe_f, pe_r)


def _softmax_pv(heads_parts):
    ms = [functools.reduce(jnp.maximum, [jnp.max(s, axis=-1, keepdims=True) for s, _ in parts])
          for parts in heads_parts]
    ps = [[jnp.exp(s - m) for s, _ in parts] for parts, m in zip(heads_parts, ms)]
    dens = [sum(jnp.sum(p, axis=-1, keepdims=True) for p in pp) for pp in ps]
    nums = [sum(jnp.dot(p.astype(BF16), vals, preferred_element_type=F32) for p, (_, vals) in zip(pp, parts))
            for pp, parts in zip(ps, heads_parts)]
    return [num / den for num, den in zip(nums, dens)]


def _attn_kernel(q_ref, k_ref, v_ref, bias_ref, o_ref, *, n_ctx, rows, q_off, d_b):
    n = HEAD_DIM
    heads = d_b // n
    nq_ctx = n_ctx // GRID_W
    i = pl.program_id(1) + q_off
    nb = q_ref.shape[0]
    q = [q_ref[b] for b in range(nb)]
    kc = [k_ref[b, 0:n_ctx, :] for b in range(nb)]
    vc = [v_ref[b, 0:n_ctx, :] for b in range(nb)]
    units = [(b, h, slice(h * n, (h + 1) * n)) for b in range(nb) for h in range(heads)]

    def store(outs):
        for b in range(nb):
            o_ref[b] = jnp.concatenate(outs[b * heads:(b + 1) * heads], axis=1).astype(o_ref.dtype)

    def latent():
        li = i - nq_ctx
        r0 = jnp.clip(li - WIN_H // 2, 0, rows - WIN_H)
        start = pl.multiple_of(n_ctx + r0 * GRID_W, GRID_W)
        kl = [k_ref[b, pl.ds(start, WIN_H * GRID_W), :] for b in range(nb)]
        vl = [v_ref[b, pl.ds(start, WIN_H * GRID_W), :] for b in range(nb)]
        s_l = [_bdot_nt(q[b][:, sl], kl[b][:, sl]) + bias_ref[0, h] for b, h, sl in units]
        s_c = [_bdot_nt(q[b][:, sl], kc[b][:, sl]) for b, h, sl in units]
        store(_softmax_pv([[(sa, vl[b][:, sl]), (sb, vc[b][:, sl])]
                           for sa, sb, (b, h, sl) in zip(s_l, s_c, units)]))

    def context():
        store(_softmax_pv([[(_bdot_nt(q[b][:, sl], kc[b][:, sl]), vc[b][:, sl])] for b, h, sl in units]))

    if q_off == 0:
        pl.when(i >= nq_ctx)(latent)
        pl.when(i < nq_ctx)(context)
    else:
        latent()


def _bias_table(rpb):
    w = GRID_W
    heads = rpb.shape[0]
    j = np.arange(w)
    col_start = np.clip(j - WIN_W // 2, 0, w - WIN_W)
    in_win = (j[None, :] >= col_start[:, None]) & (j[None, :] < col_start[:, None] + WIN_W)
    rpb = rpb.astype(F32)
    ext = jnp.concatenate([jnp.repeat(rpb[..., :1], w - WIN_W, axis=-1), rpb,
                           jnp.repeat(rpb[..., -1:], w - WIN_W, axis=-1)], axis=-1)
    toe = jnp.stack([ext[..., w - 1 - qc:2 * w - 1 - qc] for qc in range(w)], axis=2)
    tz = jnp.where(in_win, toe, NEG_BIG)
    bt = jnp.stack([tz[:, v:v + WIN_H] for v in range(WIN_H)], axis=0)
    return bt.transpose(0, 1, 3, 2, 4).reshape(WIN_H, heads, w, WIN_H * w)


def _attn_call(qkv, bias_tab, *, n_ctx, with_ctx):
    bsz, tt, d3 = qkv.shape
    d_b = d3 // 3
    heads = d_b // HEAD_DIM
    rows = (tt - n_ctx) // GRID_W
    nq_ctx = n_ctx // GRID_W
    q_off = 0 if with_ctx else nq_ctx
    n_q = tt // GRID_W - q_off
    nb = next(m for m in (4, 2, 1) if bsz % m == 0)

    def variant(b, i):
        li = jnp.maximum(i + q_off - nq_ctx, 0)
        return (jnp.clip(li - WIN_H // 2, 0, rows - WIN_H) - li + WIN_H - 1, 0, 0, 0)

    return pl.pallas_call(
        functools.partial(_attn_kernel, n_ctx=n_ctx, rows=rows, q_off=q_off, d_b=d_b),
        out_shape=jax.ShapeDtypeStruct((bsz, n_q * GRID_W, d_b), BF16),
        grid=(bsz // nb, n_q),
        in_specs=[pl.BlockSpec((nb, GRID_W, d_b), lambda b, i: (b, i + q_off, 0)),
                  pl.BlockSpec((nb, tt, d_b), lambda b, i: (b, 0, 1)),
                  pl.BlockSpec((nb, tt, d_b), lambda b, i: (b, 0, 2)),
                  pl.BlockSpec((1, heads, GRID_W, WIN_H * GRID_W), variant)],
        out_specs=pl.BlockSpec((nb, GRID_W, d_b), lambda b, i: (b, i, 0)),
        compiler_params=pltpu.CompilerParams(dimension_semantics=("parallel", "arbitrary"),
                                             vmem_limit_bytes=VMEM_LIMIT),
        name="neighbourhood_attention",
    )(qkv, qkv, qkv, bias_tab)


def _merge_kernel(xc_ref, x_ref, mod_ref, y0_ref, y1_ref, rkv_ref, lg_ref, yb_ref, gates_ref,
                  rk_ref, lng_ref, lnb_ref, glb_ref, pa_ref, pb_ref, wo_ref, gs_ref, o_ref,
                  *, tm, n_ctx, t_off, d_a):
    t = pl.program_id(1) + t_off
    is_ctx = _ctx_rows(t, tm, n_ctx)
    gs = gs_ref[...]
    inv_n = 1.0 / HEAD_DIM
    y = y0_ref[0] + y1_ref[0]
    mu = _group_sum2(y, gs) * inv_n
    yc = y - mu
    var = _group_sum2(yc * yc, gs) * inv_n
    yn = yc * lax.rsqrt(var + GN_EPS) * lng_ref[...] + lnb_ref[...]
    rkv = rkv_ref[0]
    r, k, v = rkv[:, :d_a], rkv[:, d_a:2 * d_a], rkv[:, 2 * d_a:]
    bonus = _group_sum2(r * k * rk_ref[...], gs) * v
    g = _bdot(_sigmoid(lg_ref[0]), glb_ref[...])
    ya = (yn + bonus) * g
    gates = gates_ref[0].astype(F32)
    dm = gates.shape[1] // 2
    merged = gates[:, :dm] * _bdot(ya, pa_ref[...]) + gates[:, dm:] * _bdot(yb_ref[0], pb_ref[...])
    o_ref[0] = (_token_tile(xc_ref, x_ref, t, tm, n_ctx)
                + _mod_row(mod_ref, 2, is_ctx) * _bdot(merged, wo_ref[...]))


def _merge_call(xs, modsel, y_f, y_r, rkv, lg, yb, gates, r_k, ln_g, ln_b, glb, pa, pb, wo, gs,
                *, n_ctx, with_ctx, tm=256):
    bsz, tt, d_a3 = rkv.shape
    d_a = d_a3 // 3
    t_off = 0 if with_ctx else n_ctx // tm
    n_t = tt // tm - t_off
    tok_arrays, tok_specs = _token_sources(xs, tm, n_ctx, t_off)
    d = tok_arrays[1].shape[2]
    yb_off = 0 if with_ctx else -t_off
    row = lambda b, t: (b, t + t_off, 0)
    const = lambda b, t: (0, 0)
    return pl.pallas_call(
        functools.partial(_merge_kernel, tm=tm, n_ctx=n_ctx, t_off=t_off, d_a=d_a),
        out_shape=jax.ShapeDtypeStruct((bsz, n_t * tm, d), F32),
        grid=(bsz, n_t),
        in_specs=tok_specs + [
                  pl.BlockSpec((1, 2, 8, d), lambda b, t: (b, 0, 0, 0)),
                  pl.BlockSpec((1, tm, d_a), row),
                  pl.BlockSpec((1, tm, d_a), row),
                  pl.BlockSpec((1, tm, 3 * d_a), row),
                  pl.BlockSpec((1, tm, lg.shape[2]), row),
                  pl.BlockSpec((1, tm, yb.shape[2]), lambda b, t: (b, t + t_off + yb_off, 0)),
                  pl.BlockSpec((1, tm, gates.shape[2]), row),
                  pl.BlockSpec((1, d_a), const),
                  pl.BlockSpec((1, d_a), const),
                  pl.BlockSpec((1, d_a), const),
                  pl.BlockSpec(glb.shape, const),
                  pl.BlockSpec(pa.shape, const),
                  pl.BlockSpec(pb.shape, const),
                  pl.BlockSpec(wo.shape, const),
                  pl.BlockSpec((LANE, LANE), const)],
        out_specs=pl.BlockSpec((1, tm, d), lambda b, t: (b, t, 0)),
        compiler_params=pltpu.CompilerParams(dimension_semantics=("parallel", "parallel"),
                                             vmem_limit_bytes=VMEM_LIMIT),
        name="branch_merge",
    )(*tok_arrays, modsel, y_f, y_r, rkv, lg, yb, gates, r_k, ln_g, ln_b, glb, pa, pb, wo, gs)


def _route(sel, scores, n_experts):
    per = n_experts // N_GROUPS
    in_top = []
    for e in range(n_experts):
        g0 = (e // per) * per
        rank = 0.0
        for o in range(g0, g0 + per):
            if o == e:
                continue
            ahead = (sel[o] >= sel[e]) if o < e else (sel[o] > sel[e])
            rank = rank + jnp.where(ahead, 1.0, 0.0)
        in_top.append(rank < TOP_K)
    grp = [sum(jnp.where(in_top[e], sel[e], 0.0) for e in range(g * per, (g + 1) * per))
           for g in range(N_GROUPS)]
    best = jnp.zeros_like(grp[0], dtype=jnp.int32)
    best_s = grp[0]
    for g in range(1, N_GROUPS):
        better = grp[g] > best_s
        best = jnp.where(better, g, best)
        best_s = jnp.where(better, grp[g], best_s)
    chosen = [in_top[e] & (best == e // per) for e in range(n_experts)]
    den = sum(jnp.where(chosen[e], scores[e], 0.0) for e in range(n_experts))
    return [jnp.where(chosen[e], scores[e] / den, 0.0) for e in range(n_experts)], best


MOE_FULL_GROUP_MAX_TM = 768
MOE_BLK = 128
POS_COL = 16


def _moe_route_kernel(x_ref, mod_ref, g_ref, rw_ref, rb_ref, hs_ref, gs_ref, tab_ref, cnt_ref,
                      *, tm, ts, n_ctx, n_experts):
    t = pl.program_id(1)
    is_ctx = _ctx_rows(t, tm, n_ctx)
    h = _rms_mod(x_ref[0], g_ref[...], _mod_row(mod_ref, 3, is_ctx), _mod_row(mod_ref, 4, is_ctx))
    hh, hl = _split(h)
    wh, wl = _split(rw_ref[...])
    nt = lambda a, b: lax.dot_general(a, b, (((1,), (1,)), ((), ())), preferred_element_type=F32)
    logits = nt(wh, hh) + nt(wh, hl) + nt(wl, hh)
    scores = _sigmoid(logits)
    sel = scores + rb_ref[...]
    gates, best = _route([sel[i:i + 1, :] for i in range(n_experts)],
                         [scores[i:i + 1, :] for i in range(n_experts)], n_experts)

    in_grp = [jnp.where(best == g, 1.0, 0.0) for g in range(N_GROUPS)]
    grp8 = jnp.concatenate(in_grp + [jnp.zeros((8 - N_GROUPS, tm), F32)], axis=0).astype(BF16)
    row = lax.broadcasted_iota(jnp.int32, (tm, tm), 0)
    col = lax.broadcasted_iota(jnp.int32, (tm, tm), 1)
    upper = jnp.where(row <= col, 1.0, 0.0).astype(BF16)
    run = jnp.dot(grp8, upper, preferred_element_type=F32)
    pos = jnp.zeros((1, tm), F32)
    seg = jnp.zeros((1, 1), F32)
    cnts = []
    for g in range(N_GROUPS):
        cnt = run[g:g + 1, tm - 1:tm]
        cnts.append(cnt)
        pos = pos + in_grp[g] * (seg + run[g:g + 1, :] - 1.0)
        seg = seg + jnp.floor((cnt + (MOE_BLK - 1)) * (1.0 / MOE_BLK)) * MOE_BLK
    cnt_ref[0, 0] = jnp.concatenate([jnp.broadcast_to(c, (1, LANE)) for c in cnts]
                                    + [jnp.zeros((8 - N_GROUPS, LANE), F32)], axis=0)
    tab = jnp.concatenate(gates + [pos, jnp.zeros((LANE - n_experts - 1, tm), F32)], axis=0).T
    tab_ref[0] = tab
    sel = jnp.where(lax.broadcasted_iota(jnp.int32, (ts, tm), 0) == pos.astype(jnp.int32), 1.0, 0.0).astype(BF16)
    hs_ref[0, 0] = jnp.dot(sel, h.astype(BF16), preferred_element_type=F32).astype(BF16)
    t1 = tab.astype(BF16)
    r1 = tab - t1.astype(F32)
    t2 = r1.astype(BF16)
    t3 = (r1 - t2.astype(F32)).astype(BF16)
    gs_ref[0, 0] = (jnp.dot(sel, t1, preferred_element_type=F32) + jnp.dot(sel, t2, preferred_element_type=F32)
                    + jnp.dot(sel, t3, preferred_element_type=F32))


def _moe_expert_kernel(lo_ref, hi_ref, x_ref, mod_ref, hs_ref, gs_ref, tab_ref, w1_ref, w3_ref, w2_ref, o_ref,
                       acc_ref, *, tm, ts, n_ctx, n_experts):
    b = pl.program_id(0)
    t = pl.program_id(1)
    e = pl.program_id(2)
    n_t = pl.num_programs(1)

    @pl.when(e == 0)
    def _():
        acc_ref[...] = jnp.zeros_like(acc_ref)

    per_step = w1_ref.shape[1]
    e_first = e * per_step
    seg = (b * n_t + t) * N_GROUPS + e_first // (n_experts // N_GROUPS)
    lane = lax.broadcasted_iota(jnp.int32, (MOE_BLK, LANE), 1)

    def block(j, carry):
        rows = pl.ds(pl.multiple_of(j * MOE_BLK, MOE_BLK), MOE_BLK)
        hb = hs_ref[0, 0, rows, :]
        gs = gs_ref[0, 0, rows, :]
        hids = []
        for i in range(per_step):
            a = jnp.dot(hb, w1_ref[0, i], preferred_element_type=F32)
            bb = jnp.dot(hb, w3_ref[0, i], preferred_element_type=F32)
            gcol = jnp.sum(jnp.where(lane == e_first + i, gs, 0.0), axis=-1, keepdims=True)
            hids.append((a * _sigmoid(a) * bb * gcol).astype(BF16))
        w2 = w2_ref[0].reshape(per_step * w2_ref.shape[2], w2_ref.shape[3])
        acc_ref[rows, :] += jnp.dot(jnp.concatenate(hids, axis=1), w2, preferred_element_type=F32)
        return carry

    lax.fori_loop(lo_ref[seg], hi_ref[seg], block, 0)

    @pl.when(e == pl.num_programs(2) - 1)
    def _():
        is_ctx = _ctx_rows(t, tm, n_ctx)
        pos = tab_ref[0][:, POS_COL:POS_COL + 1].astype(jnp.int32)
        back = jnp.where(lax.broadcasted_iota(jnp.int32, (tm, ts), 1) == pos, 1.0, 0.0).astype(BF16)
        y = jnp.dot(back, acc_ref[...].astype(BF16), preferred_element_type=F32)
        o_ref[0] = x_ref[0] + _mod_row(mod_ref, 5, is_ctx) * y


def _moe_call(xs, modsel, norm_g, router_wt, router_b, w1, w3, w2, *, layer, n_ctx, tm):
    bsz, tt, d = xs.shape
    n_experts = router_wt.shape[0]
    n_t = tt // tm
    ts = tm + N_GROUPS * MOE_BLK
    row2 = lambda b, t: (b, t, 0)
    const2 = lambda b, t: (0, 0)
    tile2 = lambda b, t: (b, t, 0, 0)
    hs, gs, tab, cnt = pl.pallas_call(
        functools.partial(_moe_route_kernel, tm=tm, ts=ts, n_ctx=n_ctx, n_experts=n_experts),
        out_shape=(jax.ShapeDtypeStruct((bsz, n_t, ts, d), BF16),
                   jax.ShapeDtypeStruct((bsz, n_t, ts, LANE), F32),
                   jax.ShapeDtypeStruct((bsz, tt, LANE), F32),
                   jax.ShapeDtypeStruct((bsz, n_t, 8, LANE), F32)),
        grid=(bsz, n_t),
        in_specs=[pl.BlockSpec((1, tm, d), row2),
                  pl.BlockSpec((1, 2, 8, d), lambda b, t: (b, 0, 0, 0)),
                  pl.BlockSpec((1, d), const2),
                  pl.BlockSpec(router_wt.shape, const2),
                  pl.BlockSpec((n_experts, 1), const2)],
        out_specs=(pl.BlockSpec((1, 1, ts, d), tile2),
                   pl.BlockSpec((1, 1, ts, LANE), tile2),
                   pl.BlockSpec((1, tm, LANE), row2),
                   pl.BlockSpec((1, 1, 8, LANE), tile2)),
        compiler_params=pltpu.CompilerParams(dimension_semantics=("parallel", "parallel"),
                                             vmem_limit_bytes=VMEM_LIMIT),
        name="moe_route",
    )(xs, modsel, norm_g, router_wt, router_b)

    n_blk = (cnt[:, :, :N_GROUPS, 0].astype(jnp.int32) + (MOE_BLK - 1)) // MOE_BLK
    hi = jnp.cumsum(n_blk, axis=-1)
    lo = hi - n_blk

    row = lambda b, t, e, lo_r, hi_r: (b, t, 0)
    wexp = lambda b, t, e, lo_r, hi_r: (layer, e, 0, 0)
    tile4 = lambda b, t, e, lo_r, hi_r: (b, t, 0, 0)
    per_step = n_experts // N_GROUPS if tm <= MOE_FULL_GROUP_MAX_TM else n_experts // N_GROUPS // 2
    return pl.pallas_call(
        functools.partial(_moe_expert_kernel, tm=tm, ts=ts, n_ctx=n_ctx, n_experts=n_experts),
        out_shape=jax.ShapeDtypeStruct((bsz, tt, d), F32),
        grid_spec=pltpu.PrefetchScalarGridSpec(
            num_scalar_prefetch=2,
            grid=(bsz, n_t, n_experts // per_step),
            in_specs=[pl.BlockSpec((1, tm, d), row),
                      pl.BlockSpec((1, 2, 8, d), lambda b, t, e, lo_r, hi_r: (b, 0, 0, 0)),
                      pl.BlockSpec((1, 1, ts, d), tile4),
                      pl.BlockSpec((1, 1, ts, LANE), tile4),
                      pl.BlockSpec((1, tm, LANE), row),
                      pl.BlockSpec((1, per_step) + w1.shape[2:], wexp),
                      pl.BlockSpec((1, per_step) + w3.shape[2:], wexp),
                      pl.BlockSpec((1, per_step) + w2.shape[2:], wexp)],
            out_specs=pl.BlockSpec((1, tm, d), row),
            scratch_shapes=[pltpu.VMEM((ts, d), F32)]),
        compiler_params=pltpu.CompilerParams(
            dimension_semantics=("parallel", "parallel", "arbitrary"),
            vmem_limit_bytes=VMEM_LIMIT),
        name="moe_experts",
    )(lo.reshape(-1), hi.reshape(-1), xs, modsel, hs, gs, tab, w1, w3, w2)


def kernel(x, c, ctx, c_ctx, mod_w, mod_b, norm1_g, norm2_g, w_in, rw_w0, rw_w_lora_b, rw_a0, rw_a_lora_b,
           rw_g_lora_b, rw_k_k, rw_k_a, rw_r_k, rw_ln_g, rw_ln_b, na_q_g, na_k_g, na_rpb, proj_a, proj_b,
           w_out, router_w, router_bias, moe_w1, moe_w3, moe_w2):
    bsz, seq, d = x.shape
    n_ctx = ctx.shape[1]
    depth = mod_w.shape[0]
    d_a = rw_w0.shape[2]
    d_b = proj_b.shape[1]
    lora_g = rw_g_lora_b.shape[1]
    heads_b = d_b // HEAD_DIM
    assert seq % (GRID_W * WIN_H) == 0 and n_ctx % 256 == 0 and bsz + 1 <= 16

    cs = jnp.zeros((16, d), F32).at[:bsz].set(c).at[bsz].set(c_ctx)
    mod = _mod_call(cs, mod_w, mod_b).reshape(depth, 16, 6, d)

    lane = jnp.arange(LANE)
    gs = (lane[:, None] // HEAD_DIM == lane[None, :] // HEAD_DIM).astype(BF16)
    router_wt = router_w.T
    router_b = router_bias.reshape(-1, 1)

    w_in_b, w1_b, w3_b, w2_b = (_to_bf16(w) for w in (w_in, moe_w1, moe_w3, moe_w2))
    xs = (ctx, x)
    for l in range(depth):
        last = l == depth - 1
        m_c = jnp.broadcast_to(mod[l, bsz][None], (bsz, 6, d))
        modsel = jnp.pad(jnp.stack([m_c, mod[l, :bsz]], axis=1), ((0, 0), (0, 0), (0, 2), (0, 0)))

        rkv, lg, qkv, gates, op_f, op_r, v_a, pe_f, pe_r = _in_proj_call(
            xs, modsel, norm1_g[l][None], w_in_b, rw_k_k[l][None],
            jnp.tile(na_q_g[l], heads_b)[None], jnp.tile(na_k_g[l], heads_b)[None], gs,
            rw_w0[l], rw_w_lora_b[l], rw_a0[l], rw_a_lora_b[l], rw_k_a[l][None],
            layer=l, n_ctx=n_ctx, d_a=d_a, d_b=d_b, lora_g=lora_g)
        y_f, y_r = _rwkv_call(op_f, op_r, v_a, pe_f, pe_r, n_ctx=n_ctx)
        yb = _attn_call(qkv, _bias_table(na_rpb[l]), n_ctx=n_ctx, with_ctx=not last)
        xs = _merge_call(xs, modsel, y_f, y_r, rkv, lg, yb, gates, rw_r_k[l].reshape(1, d_a),
                         rw_ln_g[l][None], rw_ln_b[l][None], rw_g_lora_b[l].astype(BF16),
                         proj_a[l].astype(BF16), proj_b[l].astype(BF16), w_out[l].astype(BF16), gs,
                         n_ctx=n_ctx, with_ctx=not last)
        moe_ctx = 0 if last else n_ctx
        tm = next(m for m in (1024, 768, 512, 256) if xs.shape[1] % m == 0)
        xs = _moe_call(xs, modsel, norm2_g[l][None], router_wt, router_b, w1_b, w3_b, w2_b,
                       layer=l, n_ctx=moe_ctx, tm=tm)
    return xs
```

```python
import functools
import math

import jax
import jax.numpy as jnp
import numpy as np
from jax import lax
from jax.experimental import pallas as pl
from jax.experimental.pallas import tpu as pltpu

F32 = jnp.float32
BF16 = jnp.bfloat16

HEAD_DIM = 64
GRID_W = 64
WIN_H = 8
WIN_W = 16
N_GROUPS = 4
TOP_K = 2
RMS_EPS = 1e-6
GN_EPS = 64e-5
LANE = 128
CHUNK = 64
SUB = 16
NEG_BIG = -1e30
DECAY_SCALE = math.exp(-0.5)
VMEM_LIMIT = 58 * 1024 * 1024


def _bdot(a, b):
    return jnp.dot(a.astype(BF16), b.astype(BF16), preferred_element_type=F32)


def _bdot_nt(a, b):
    return lax.dot_general(a.astype(BF16), b.astype(BF16), (((1,), (1,)), ((), ())),
                           preferred_element_type=F32)


def _bdot_tn(a, b):
    return lax.dot_general(a.astype(BF16), b.astype(BF16), (((0,), (0,)), ((), ())),
                           preferred_element_type=F32)


def _split(x):
    hi = x.astype(BF16)
    lo = (x - hi.astype(F32)).astype(BF16)
    return hi, lo


def _dot3(a, b):
    ah, al = _split(a)
    bh, bl = _split(b)
    return (jnp.dot(ah, bh, preferred_element_type=F32)
            + jnp.dot(al, bh, preferred_element_type=F32)
            + jnp.dot(ah, bl, preferred_element_type=F32))


def _sigmoid(x):
    return 1.0 / (1.0 + jnp.exp(-x))


def _group_sum(x, g128):
    parts = [_bdot(x[:, j * LANE:(j + 1) * LANE], g128) for j in range(x.shape[1] // LANE)]
    return jnp.concatenate(parts, axis=1)


def _group_sum2(x, g128):
    hi, lo = _split(x)
    parts = []
    for j in range(x.shape[1] // LANE):
        sl = slice(j * LANE, (j + 1) * LANE)
        parts.append(jnp.dot(hi[:, sl], g128, preferred_element_type=F32)
                     + jnp.dot(lo[:, sl], g128, preferred_element_type=F32))
    return jnp.concatenate(parts, axis=1)


def _mod_row(mod_ref, idx, is_ctx):
    mx = mod_ref[0, 1, idx:idx + 1, :]
    if is_ctx is None:
        return mx
    return jnp.where(is_ctx, mod_ref[0, 0, idx:idx + 1, :], mx)


def _ctx_rows(tile, tm, n_ctx):
    if n_ctx == 0:
        return None
    rows = tile * tm + lax.broadcasted_iota(jnp.int32, (tm, 1), 0)
    return rows < n_ctx


def _token_sources(src, tm, n_ctx, t_off):
    assert n_ctx % tm == 0
    n_ct = n_ctx // tm
    if isinstance(src, tuple):
        ctx, x = src
        d = x.shape[2]
        lat = pl.BlockSpec((1, tm, d), lambda b, t: (b, jnp.maximum(t + t_off - n_ct, 0), 0))
    else:
        ctx = x = src
        d = x.shape[2]
        lat = pl.BlockSpec((1, tm, d), lambda b, t: (b, t + t_off, 0))
    con = pl.BlockSpec((1, tm, d), lambda b, t: (b, jnp.minimum(t + t_off, n_ct - 1), 0))
    return (ctx, x), [con, lat]


def _token_tile(ctx_ref, lat_ref, tile, tm, n_ctx):
    return jnp.where(tile * tm < n_ctx, ctx_ref[0], lat_ref[0])


def _rms_mod(x, gain, shift, scale):
    xn = x * lax.rsqrt(jnp.mean(x * x, axis=-1, keepdims=True) + RMS_EPS) * gain
    return xn * (1.0 + scale) + shift


CAST_BLOCK_BYTES = 8 * 1024 * 1024


def _cast_kernel(w_ref, o_ref):
    o_ref[...] = w_ref[...].astype(o_ref.dtype)


def _to_bf16(w):
    c = w.shape[-1]
    w2 = w.reshape(-1, c)
    rows = w2.shape[0]
    rb = max(16, min(rows, CAST_BLOCK_BYTES // (c * 4)) // 16 * 16)
    while rows % rb:
        rb -= 16
    out = pl.pallas_call(
        _cast_kernel,
        out_shape=jax.ShapeDtypeStruct(w2.shape, BF16),
        grid=(rows // rb,),
        in_specs=[pl.BlockSpec((rb, c), lambda i: (i, 0))],
        out_specs=pl.BlockSpec((rb, c), lambda i: (i, 0)),
        compiler_params=pltpu.CompilerParams(dimension_semantics=("parallel",), vmem_limit_bytes=VMEM_LIMIT),
        name="to_bf16",
    )(w2)
    return out.reshape(w.shape)


def _mod_kernel(c_ref, w_ref, b_ref, o_ref):
    c = c_ref[...]
    o_ref[0] = _dot3(c * _sigmoid(c), w_ref[0]) + b_ref[0]


def _mod_call(cs, mod_w, mod_b):
    depth, d, n = mod_w.shape
    tn = n // 4
    return pl.pallas_call(
        _mod_kernel,
        out_shape=jax.ShapeDtypeStruct((depth, cs.shape[0], n), F32),
        grid=(depth, n // tn),
        in_specs=[pl.BlockSpec(cs.shape, lambda l, j: (0, 0)),
                  pl.BlockSpec((1, d, tn), lambda l, j: (l, 0, j)),
                  pl.BlockSpec((1, 1, tn), lambda l, j: (l, 0, j))],
        out_specs=pl.BlockSpec((1, cs.shape[0], tn), lambda l, j: (l, 0, j)),
        compiler_params=pltpu.CompilerParams(dimension_semantics=("parallel", "parallel"),
                                             vmem_limit_bytes=VMEM_LIMIT),
        name="adaln_vectors",
    )(cs, mod_w, mod_b.reshape(depth, 1, n))


def _in_proj_kernel(xc_ref, x_ref, mod_ref, g_ref, w_ref, kkg_ref, qg_ref, kg_ref, gs_ref,
                    w0_ref, wlbh_ref, wlbl_ref, a0_ref, albh_ref, albl_ref, ka_ref,
                    rkv_ref, lg_ref, qkv_ref, gates_ref, opf_ref, opr_ref, v_ref, pef_ref, per_ref,
                    *, tm, n_ctx, d_a, d_b, lora_w, lora_a, lora_g):
    t = pl.program_id(1)
    is_ctx = _ctx_rows(t, tm, n_ctx)
    h = _rms_mod(_token_tile(xc_ref, x_ref, t, tm, n_ctx), g_ref[...],
                 _mod_row(mod_ref, 0, is_ctx), _mod_row(mod_ref, 1, is_ctx))
    h = h.astype(BF16)
    gs = gs_ref[...]
    n_lora = 2 * lora_w + 2 * lora_a + lora_g

    o = 0
    rkv = jnp.dot(h, w_ref[0, :, o:o + 3 * d_a], preferred_element_type=F32)
    rkv_ref[0] = rkv
    kkv = rkv[:, d_a:2 * d_a] * kkg_ref[...]
    ss = _group_sum2(kkv * kkv, gs)
    kk = kkv * lax.rsqrt(jnp.maximum(ss, 1e-12))
    o += 3 * d_a

    lora = jnp.dot(h, w_ref[0, :, o:o + n_lora], preferred_element_type=F32)
    lg_ref[0] = lora[:, n_lora - lora_g:]
    v_ref[0] = rkv[:, 2 * d_a:].astype(BF16)
    o += n_lora

    qkv = jnp.dot(h, w_ref[0, :, o:o + 3 * d_b], preferred_element_type=F32)
    q, k = qkv[:, :d_b], qkv[:, d_b:2 * d_b]
    inv_n = 1.0 / HEAD_DIM
    qn = q * lax.rsqrt(_group_sum(q * q, gs) * inv_n + RMS_EPS) * qg_ref[...] * (HEAD_DIM ** -0.5)
    kn = k * lax.rsqrt(_group_sum(k * k, gs) * inv_n + RMS_EPS) * kg_ref[...]
    qkv_ref[0] = jnp.concatenate([qn, kn, qkv[:, 2 * d_b:]], axis=1).astype(BF16)
    o += 3 * d_b

    gates_ref[0] = _sigmoid(jnp.dot(h, w_ref[0, :, o:], preferred_element_type=F32)).astype(BF16)
    _rwkv_operand_rows((0, 1), rkv, kk, lora, w0_ref, wlbh_ref, wlbl_ref, a0_ref, albh_ref, albl_ref, ka_ref,
                       opf_ref, opr_ref, pef_ref, per_ref, tm=tm, d_a=d_a, lora_w=lora_w, lora_a=lora_a)


def _in_proj_call(xs, modsel, norm_g, w_in, k_k, q_g, k_g, gs, w0, wlb, a0, alb, k_a,
                  *, layer, n_ctx, d_a, d_b, lora_g, tm=256):
    tok_arrays, tok_specs = _token_sources(xs, tm, n_ctx, 0)
    bsz, d = tok_arrays[1].shape[0], tok_arrays[1].shape[2]
    tt = sum(a.shape[1] for a in xs) if isinstance(xs, tuple) else xs.shape[1]
    lora_w, lora_a = wlb.shape[1], alb.shape[1]
    n_lora = 2 * lora_w + 2 * lora_a + lora_g
    n_gate = w_in.shape[2] - 3 * d_a - n_lora - 3 * d_b
    row = lambda b, t: (b, t, 0)
    const = lambda b, t: (0, 0)
    const3 = lambda b, t: (0, 0, 0)
    wlb_hi, wlb_lo = _split(wlb)
    alb_hi, alb_lo = _split(alb)
    op_shape = jax.ShapeDtypeStruct((bsz, tt, N_OPERANDS * d_a), BF16)
    op_spec = pl.BlockSpec((1, tm, N_OPERANDS * d_a), row)
    pe_shape = jax.ShapeDtypeStruct((bsz, tt // CHUNK, 8, d_a), F32)
    pe_spec = pl.BlockSpec((1, tm // CHUNK, 8, d_a), lambda b, t: (b, t, 0, 0))
    return pl.pallas_call(
        functools.partial(_in_proj_kernel, tm=tm, n_ctx=n_ctx, d_a=d_a, d_b=d_b,
                          lora_w=lora_w, lora_a=lora_a, lora_g=lora_g),
        out_shape=(jax.ShapeDtypeStruct((bsz, tt, 3 * d_a), F32),
                   jax.ShapeDtypeStruct((bsz, tt, lora_g), F32),
                   jax.ShapeDtypeStruct((bsz, tt, 3 * d_b), BF16),
                   jax.ShapeDtypeStruct((bsz, tt, n_gate), BF16),
                   op_shape, op_shape, jax.ShapeDtypeStruct((bsz, tt, d_a), BF16), pe_shape, pe_shape),
        grid=(bsz, tt // tm),
        in_specs=tok_specs + [
                  pl.BlockSpec((1, 2, 8, d), lambda b, t: (b, 0, 0, 0)),
                  pl.BlockSpec((1, d), const),
                  pl.BlockSpec((1,) + w_in.shape[1:], lambda b, t: (layer, 0, 0)),
                  pl.BlockSpec((1, d_a), const),
                  pl.BlockSpec((1, d_b), const),
                  pl.BlockSpec((1, d_b), const),
                  pl.BlockSpec((LANE, LANE), const),
                  pl.BlockSpec((2, 1, d_a), const3),
                  pl.BlockSpec((2, lora_w, d_a), const3),
                  pl.BlockSpec((2, lora_w, d_a), const3),
                  pl.BlockSpec((2, 1, d_a), const3),
                  pl.BlockSpec((2, lora_a, d_a), const3),
                  pl.BlockSpec((2, lora_a, d_a), const3),
                  pl.BlockSpec((1, d_a), const)],
        out_specs=(pl.BlockSpec((1, tm, 3 * d_a), row),
                   pl.BlockSpec((1, tm, lora_g), row),
                   pl.BlockSpec((1, tm, 3 * d_b), row),
                   pl.BlockSpec((1, tm, n_gate), row),
                   op_spec, op_spec, pl.BlockSpec((1, tm, d_a), row), pe_spec, pe_spec),
        compiler_params=pltpu.CompilerParams(dimension_semantics=("parallel", "parallel"),
                                             vmem_limit_bytes=VMEM_LIMIT),
        name="in_proj",
    )(*tok_arrays, modsel, norm_g, w_in, k_k, q_g, k_g, gs,
      w0.reshape(2, 1, d_a), wlb_hi, wlb_lo, a0.reshape(2, 1, d_a), alb_hi, alb_lo, k_a)


N_OPERANDS = 4


def _dot3_presplit(x, w_hi, w_lo):
    xh, xl = _split(x)
    return (jnp.dot(xh, w_hi, preferred_element_type=F32) + jnp.dot(xl, w_hi, preferred_element_type=F32)
            + jnp.dot(xh, w_lo, preferred_element_type=F32))


def _rwkv_operand_rows(dirs, rkv, kk, lora, w0_ref, wlbh_ref, wlbl_ref, a0_ref, albh_ref, albl_ref, ka_ref,
                       opf_ref, opr_ref, pef_ref, per_ref, *, tm, d_a, lora_w, lora_a):
    r, k = rkv[:, :d_a], rkv[:, d_a:2 * d_a]
    chunk_bits = int(math.log2(CHUNK))
    row = lax.broadcasted_iota(jnp.int32, (tm, tm), 0)
    col = lax.broadcasted_iota(jnp.int32, (tm, tm), 1)
    same = (row >> chunk_bits) == (col >> chunk_bits)
    for d in dirs:
        op_ref, pe_ref = ((opf_ref, pef_ref), (opr_ref, per_ref))[d]
        lw = lora[:, d * lora_w:(d + 1) * lora_w]
        la = lora[:, 2 * lora_w + d * lora_a:2 * lora_w + (d + 1) * lora_a]
        wl = w0_ref[d] + _dot3_presplit(jnp.tanh(lw), wlbh_ref[d], wlbl_ref[d])
        ld = -DECAY_SCALE * _sigmoid(wl)
        al = _sigmoid(a0_ref[d] + _dot3_presplit(la, albh_ref[d], albl_ref[d]))
        kd = k * (1.0 + (al - 1.0) * ka_ref[...])
        before = (row <= col) if d else (row >= col)
        tri = jnp.where(same & before, 1.0, 0.0).astype(BF16)
        ld_hi, ld_lo = _split(ld)
        cum = jnp.dot(tri, ld_hi, preferred_element_type=F32) + jnp.dot(tri, ld_lo, preferred_element_type=F32)
        p_inv = jnp.exp(-cum)
        op_ref[0] = jnp.concatenate([-kk * jnp.exp(cum - ld), r * jnp.exp(cum), kk * al * p_inv, kd * p_inv],
                                    axis=1).astype(BF16)
        for ci in range(tm // CHUNK):
            last = ci * CHUNK + (0 if d else CHUNK - 1)
            pe_ref[0, ci] = jnp.exp(jnp.broadcast_to(cum[last:last + 1, :], (8, d_a)))


def _rwkv_pair_kernel(opf_ref, opr_ref, vf_ref, vr_ref, pef_ref, per_ref, yf_ref, yr_ref, st_ref, *, d_a):
    n = HEAD_DIM
    pw = 2 * n
    pairs = d_a // pw
    c = CHUNK
    nb = opf_ref.shape[0]

    @pl.when(pl.program_id(1) == 0)
    def _():
        st_ref[...] = jnp.zeros_like(st_ref)

    chains = [(b, d, p) for b in range(nb) for d in range(2) for p in range(pairs)]
    op_refs = (opf_ref, opr_ref)
    part = lambda i: [op_refs[d][b, :, i * d_a + p * pw:i * d_a + (p + 1) * pw] for b, d, p in chains]
    v_refs = (vf_ref, vr_ref)
    pe_refs = (pef_ref, per_ref)
    v = [v_refs[d][b, :, p * pw:(p + 1) * pw] for b, d, p in chains]
    p_end = [pe_refs[d][b, 0, 0:1, p * pw:(p + 1) * pw] for b, d, p in chains]
    each = lambda f, *ls: [f(*xs) for xs in zip(*ls)]
    rows2 = lambda a, b: jnp.concatenate([a, b], axis=0)
    bf = lambda t: t.astype(BF16)

    def bd(x):
        x = bf(x)
        left = lax.broadcasted_iota(jnp.int32, x.shape, 1) < n
        zero = jnp.zeros_like(x)
        return rows2(jnp.where(left, x, zero), jnp.where(left, zero, x))

    pdot = lambda a, b: jnp.dot(bf(a), bd(b), preferred_element_type=F32)
    ar = each(rows2, part(0), part(1))
    bk = each(rows2, part(2), part(3))
    b_k = each(lambda w: rows2(bd(w[:c]), bd(w[c:])), bk)
    s = [st_ref[b, d, p] for b, d, p in chains]

    mm_all = each(lambda a, w, ss: _bdot_nt(a, rows2(w, bd(ss))), ar, b_k, s)
    row = lax.broadcasted_iota(jnp.int32, (c, 2 * pw), 0)
    col = lax.broadcasted_iota(jnp.int32, (c, 2 * pw), 1) & (c - 1)
    incl = [(row <= col) if d else (row >= col) for _, d, _ in chains]
    strict = [(row < col) if d else (row > col) for _, d, _ in chains]
    m_a = each(lambda mm, msk: jnp.where(msk, mm[:c, :2 * pw], 0.0), mm_all, strict)
    m_r = each(lambda mm, msk: jnp.where(msk, mm[c:, :2 * pw], 0.0), mm_all, incl)
    ms_a = each(lambda mm: mm[:c, 2 * pw:], mm_all)
    ms_r = each(lambda mm: mm[c:, 2 * pw:], mm_all)
    a_ab = each(lambda mm: mm[:, :pw], m_a)
    a_ak = each(lambda mm: mm[:, pw:], m_a)

    row = lax.broadcasted_iota(jnp.int32, (c, pw), 0)
    col = lax.broadcasted_iota(jnp.int32, (c, pw), 1) & (c - 1)
    eye = jnp.where(row == col, 1.0, 0.0)
    sub_bits = int(math.log2(SUB))
    blk = (row >> sub_bits) == (col >> sub_bits)
    ad = each(lambda a: jnp.where(blk, a, 0.0), a_ab)
    ao = each(lambda a, b: a - b, a_ab, ad)
    td = each(lambda a: eye + a, ad)
    x = each(pdot, ad, ad)
    for _ in range(sub_bits - 2):
        res = each(lambda xx, tt: pdot(rows2(xx, tt), xx), x, td)
        td = each(lambda tt, rr: tt + rr[c:], td, res)
        x = each(lambda rr: rr[:c], res)
    td = each(lambda tt, xx: tt + pdot(tt, xx), td, x)
    x = each(pdot, td, ao)
    w = td
    for _ in range(int(math.log2(c // SUB)) - 1):
        res = each(lambda xx, ww: jnp.dot(bf(xx), jnp.concatenate([bd(xx), bd(ww)], axis=1),
                                          preferred_element_type=F32), x, w)
        w = each(lambda ww, rr: ww + rr[:, pw:], w, res)
        x = each(lambda rr: rr[:, :pw], res)
    t_inv = each(lambda ww, xx: ww + pdot(xx, ww), w, x)

    akv = each(pdot, a_ak, v)
    u = each(lambda t, m0, m1: bf(pdot(t, m0 + m1)), t_inv, ms_a, akv)
    y = each(lambda m0, a, uu, vv: m0 + jnp.dot(bf(a), rows2(bd(uu), bd(vv)), preferred_element_type=F32),
             ms_r, m_r, u, v)
    left = lax.broadcasted_iota(jnp.int32, (n, pw), 1) < n
    upd = each(lambda uu, vv, w: _bdot_tn(rows2(uu, vv), w), u, v, bk)
    s_new = each(lambda ss, p, dd: (ss + jnp.where(left, dd[:n], dd[n:])) * p, s, p_end, upd)
    for (b, d, p), val in zip(chains, s_new):
        st_ref[b, d, p] = val
    for b in range(nb):
        yf_ref[b] = jnp.concatenate(y[2 * b * pairs:(2 * b + 1) * pairs], axis=1)
        yr_ref[b] = jnp.concatenate(y[(2 * b + 1) * pairs:(2 * b + 2) * pairs], axis=1)


def _rwkv_call(op_f, op_r, v, pe_f, pe_r, *, n_ctx):
    bsz, tt, d_a = v.shape
    n_chunks = tt // CHUNK
    nc_ctx = n_ctx // CHUNK
    heads = d_a // HEAD_DIM

    rev_chunk = lambda j: jnp.where(j < nc_ctx, nc_ctx - 1 - j, n_chunks - 1 - (j - nc_ctx))
    fwd = lambda b, j: (b, j, 0)
    rev = lambda b, j: (b, rev_chunk(j), 0)
    y_shape = jax.ShapeDtypeStruct((bsz, tt, d_a), F32)
    nb = next(m for m in (4, 2, 1) if bsz % m == 0)
    return pl.pallas_call(
        functools.partial(_rwkv_pair_kernel, d_a=d_a),
        out_shape=(y_shape, y_shape),
        grid=(bsz // nb, n_chunks),
        in_specs=[pl.BlockSpec((nb, CHUNK, N_OPERANDS * d_a), fwd),
                  pl.BlockSpec((nb, CHUNK, N_OPERANDS * d_a), rev),
                  pl.BlockSpec((nb, CHUNK, d_a), fwd),
                  pl.BlockSpec((nb, CHUNK, d_a), rev),
                  pl.BlockSpec((nb, 1, 8, d_a), lambda b, j: (b, j, 0, 0)),
                  pl.BlockSpec((nb, 1, 8, d_a), lambda b, j: (b, rev_chunk(j), 0, 0))],
        out_specs=(pl.BlockSpec((nb, CHUNK, d_a), fwd), pl.BlockSpec((nb, CHUNK, d_a), rev)),
        scratch_shapes=[pltpu.VMEM((nb, 2, heads // 2, HEAD_DIM, 2 * HEAD_DIM), F32)],
        compiler_params=pltpu.CompilerParams(dimension_semantics=("parallel", "arbitrary"),
                                             vmem_limit_bytes=VMEM_LIMIT),
        name="rwkv7_chunk_scan",
    )(op_f, op_r, v, v, pe_f, pe_r)


def _softmax_pv(heads_parts):
    ms = [functools.reduce(jnp.maximum, [jnp.max(s, axis=-1, keepdims=True) for s, _ in parts])
          for parts in heads_parts]
    ps = [[jnp.exp(s - m) for s, _ in parts] for parts, m in zip(heads_parts, ms)]
    dens = [sum(jnp.sum(p, axis=-1, keepdims=True) for p in pp) for pp in ps]
    nums = [sum(jnp.dot(p.astype(BF16), vals, preferred_element_type=F32) for p, (_, vals) in zip(pp, parts))
            for pp, parts in zip(ps, heads_parts)]
    return [num / den for num, den in zip(nums, dens)]


def _attn_kernel(q_ref, k_ref, v_ref, bias_ref, o_ref, *, n_ctx, rows, q_off, d_b):
    n = HEAD_DIM
    heads = d_b // n
    nq_ctx = n_ctx // GRID_W
    i = pl.program_id(1) + q_off
    nb = q_ref.shape[0]
    q = [q_ref[b] for b in range(nb)]
    kc = [k_ref[b, 0:n_ctx, :] for b in range(nb)]
    vc = [v_ref[b, 0:n_ctx, :] for b in range(nb)]
    units = [(b, h, slice(h * n, (h + 1) * n)) for b in range(nb) for h in range(heads)]

    def store(outs):
        for b in range(nb):
            o_ref[b] = jnp.concatenate(outs[b * heads:(b + 1) * heads], axis=1).astype(o_ref.dtype)

    def latent():
        li = i - nq_ctx
        r0 = jnp.clip(li - WIN_H // 2, 0, rows - WIN_H)
        start = pl.multiple_of(n_ctx + r0 * GRID_W, GRID_W)
        kl = [k_ref[b, pl.ds(start, WIN_H * GRID_W), :] for b in range(nb)]
        vl = [v_ref[b, pl.ds(start, WIN_H * GRID_W), :] for b in range(nb)]
        s_l = [_bdot_nt(q[b][:, sl], kl[b][:, sl]) + bias_ref[0, h] for b, h, sl in units]
        s_c = [_bdot_nt(q[b][:, sl], kc[b][:, sl]) for b, h, sl in units]
        store(_softmax_pv([[(sa, vl[b][:, sl]), (sb, vc[b][:, sl])]
                           for sa, sb, (b, h, sl) in zip(s_l, s_c, units)]))

    def context():
        store(_softmax_pv([[(_bdot_nt(q[b][:, sl], kc[b][:, sl]), vc[b][:, sl])] for b, h, sl in units]))

    if q_off == 0:
        pl.when(i >= nq_ctx)(latent)
        pl.when(i < nq_ctx)(context)
    else:
        latent()


def _bias_table(rpb):
    w = GRID_W
    heads = rpb.shape[0]
    j = np.arange(w)
    col_start = np.clip(j - WIN_W // 2, 0, w - WIN_W)
    in_win = (j[None, :] >= col_start[:, None]) & (j[None, :] < col_start[:, None] + WIN_W)
    rpb = rpb.astype(F32)
    ext = jnp.concatenate([jnp.repeat(rpb[..., :1], w - WIN_W, axis=-1), rpb,
                           jnp.repeat(rpb[..., -1:], w - WIN_W, axis=-1)], axis=-1)
    toe = jnp.stack([ext[..., w - 1 - qc:2 * w - 1 - qc] for qc in range(w)], axis=2)
    tz = jnp.where(in_win, toe, NEG_BIG)
    bt = jnp.stack([tz[:, v:v + WIN_H] for v in range(WIN_H)], axis=0)
    return bt.transpose(0, 1, 3, 2, 4).reshape(WIN_H, heads, w, WIN_H * w)


def _attn_call(qkv, bias_tab, *, n_ctx, with_ctx):
    bsz, tt, d3 = qkv.shape
    d_b = d3 // 3
    heads = d_b // HEAD_DIM
    rows = (tt - n_ctx) // GRID_W
    nq_ctx = n_ctx // GRID_W
    q_off = 0 if with_ctx else nq_ctx
    n_q = tt // GRID_W - q_off
    nb = next(m for m in (4, 2, 1) if bsz % m == 0)

    def variant(b, i):
        li = jnp.maximum(i + q_off - nq_ctx, 0)
        return (jnp.clip(li - WIN_H // 2, 0, rows - WIN_H) - li + WIN_H - 1, 0, 0, 0)

    return pl.pallas_call(
        functools.partial(_attn_kernel, n_ctx=n_ctx, rows=rows, q_off=q_off, d_b=d_b),
        out_shape=jax.ShapeDtypeStruct((bsz, n_q * GRID_W, d_b), BF16),
        grid=(bsz // nb, n_q),
        in_specs=[pl.BlockSpec((nb, GRID_W, d_b), lambda b, i: (b, i + q_off, 0)),
                  pl.BlockSpec((nb, tt, d_b), lambda b, i: (b, 0, 1)),
                  pl.BlockSpec((nb, tt, d_b), lambda b, i: (b, 0, 2)),
                  pl.BlockSpec((1, heads, GRID_W, WIN_H * GRID_W), variant)],
        out_specs=pl.BlockSpec((nb, GRID_W, d_b), lambda b, i: (b, i, 0)),
        compiler_params=pltpu.CompilerParams(dimension_semantics=("parallel", "arbitrary"),
                                             vmem_limit_bytes=VMEM_LIMIT),
        name="neighbourhood_attention",
    )(qkv, qkv, qkv, bias_tab)


def _merge_kernel(xc_ref, x_ref, mod_ref, y0_ref, y1_ref, rkv_ref, lg_ref, yb_ref, gates_ref,
                  rk_ref, lng_ref, lnb_ref, glb_ref, pa_ref, pb_ref, wo_ref, gs_ref, o_ref,
                  *, tm, n_ctx, t_off, d_a):
    t = pl.program_id(1) + t_off
    is_ctx = _ctx_rows(t, tm, n_ctx)
    gs = gs_ref[...]
    inv_n = 1.0 / HEAD_DIM
    y = y0_ref[0] + y1_ref[0]
    mu = _group_sum2(y, gs) * inv_n
    yc = y - mu
    var = _group_sum2(yc * yc, gs) * inv_n
    yn = yc * lax.rsqrt(var + GN_EPS) * lng_ref[...] + lnb_ref[...]
    rkv = rkv_ref[0]
    r, k, v = rkv[:, :d_a], rkv[:, d_a:2 * d_a], rkv[:, 2 * d_a:]
    bonus = _group_sum2(r * k * rk_ref[...], gs) * v
    g = _bdot(_sigmoid(lg_ref[0]), glb_ref[...])
    ya = (yn + bonus) * g
    gates = gates_ref[0].astype(F32)
    dm = gates.shape[1] // 2
    merged = gates[:, :dm] * _bdot(ya, pa_ref[...]) + gates[:, dm:] * _bdot(yb_ref[0], pb_ref[...])
    o_ref[0] = (_token_tile(xc_ref, x_ref, t, tm, n_ctx)
                + _mod_row(mod_ref, 2, is_ctx) * _bdot(merged, wo_ref[...]))


def _merge_call(xs, modsel, y_f, y_r, rkv, lg, yb, gates, r_k, ln_g, ln_b, glb, pa, pb, wo, gs,
                *, n_ctx, with_ctx, tm=256):
    bsz, tt, d_a3 = rkv.shape
    d_a = d_a3 // 3
    t_off = 0 if with_ctx else n_ctx // tm
    n_t = tt // tm - t_off
    tok_arrays, tok_specs = _token_sources(xs, tm, n_ctx, t_off)
    d = tok_arrays[1].shape[2]
    yb_off = 0 if with_ctx else -t_off
    row = lambda b, t: (b, t + t_off, 0)
    const = lambda b, t: (0, 0)
    return pl.pallas_call(
        functools.partial(_merge_kernel, tm=tm, n_ctx=n_ctx, t_off=t_off, d_a=d_a),
        out_shape=jax.ShapeDtypeStruct((bsz, n_t * tm, d), F32),
        grid=(bsz, n_t),
        in_specs=tok_specs + [
                  pl.BlockSpec((1, 2, 8, d), lambda b, t: (b, 0, 0, 0)),
                  pl.BlockSpec((1, tm, d_a), row),
                  pl.BlockSpec((1, tm, d_a), row),
                  pl.BlockSpec((1, tm, 3 * d_a), row),
                  pl.BlockSpec((1, tm, lg.shape[2]), row),
                  pl.BlockSpec((1, tm, yb.shape[2]), lambda b, t: (b, t + t_off + yb_off, 0)),
                  pl.BlockSpec((1, tm, gates.shape[2]), row),
                  pl.BlockSpec((1, d_a), const),
                  pl.BlockSpec((1, d_a), const),
                  pl.BlockSpec((1, d_a), const),
                  pl.BlockSpec(glb.shape, const),
                  pl.BlockSpec(pa.shape, const),
                  pl.BlockSpec(pb.shape, const),
                  pl.BlockSpec(wo.shape, const),
                  pl.BlockSpec((LANE, LANE), const)],
        out_specs=pl.BlockSpec((1, tm, d), lambda b, t: (b, t, 0)),
        compiler_params=pltpu.CompilerParams(dimension_semantics=("parallel", "parallel"),
                                             vmem_limit_bytes=VMEM_LIMIT),
        name="branch_merge",
    )(*tok_arrays, modsel, y_f, y_r, rkv, lg, yb, gates, r_k, ln_g, ln_b, glb, pa, pb, wo, gs)


def _route(sel, scores, n_experts):
    per = n_experts // N_GROUPS
    in_top = []
    for e in range(n_experts):
        g0 = (e // per) * per
        rank = 0.0
        for o in range(g0, g0 + per):
            if o == e:
                continue
            ahead = (sel[o] >= sel[e]) if o < e else (sel[o] > sel[e])
            rank = rank + jnp.where(ahead, 1.0, 0.0)
        in_top.append(rank < TOP_K)
    grp = [sum(jnp.where(in_top[e], sel[e], 0.0) for e in range(g * per, (g + 1) * per))
           for g in range(N_GROUPS)]
    best = jnp.zeros_like(grp[0], dtype=jnp.int32)
    best_s = grp[0]
    for g in range(1, N_GROUPS):
        better = grp[g] > best_s
        best = jnp.where(better, g, best)
        best_s = jnp.where(better, grp[g], best_s)
    chosen = [in_top[e] & (best == e // per) for e in range(n_experts)]
    den = sum(jnp.where(chosen[e], scores[e], 0.0) for e in range(n_experts))
    return [jnp.where(chosen[e], scores[e] / den, 0.0) for e in range(n_experts)], best


MOE_FULL_GROUP_MAX_TM = 1024
MOE_BLK = 128
POS_COL = 16


def _moe_route_kernel(x_ref, mod_ref, g_ref, rw_ref, rb_ref, hs_ref, gs_ref, tab_ref, cnt_ref,
                      *, tm, ts, n_ctx, n_experts):
    t = pl.program_id(1)
    is_ctx = _ctx_rows(t, tm, n_ctx)
    h = _rms_mod(x_ref[0], g_ref[...], _mod_row(mod_ref, 3, is_ctx), _mod_row(mod_ref, 4, is_ctx))
    hh, hl = _split(h)
    wh, wl = _split(rw_ref[...])
    nt = lambda a, b: lax.dot_general(a, b, (((1,), (1,)), ((), ())), preferred_element_type=F32)
    logits = nt(wh, hh) + nt(wh, hl) + nt(wl, hh)
    scores = _sigmoid(logits)
    sel = scores + rb_ref[...]
    gates, best = _route([sel[i:i + 1, :] for i in range(n_experts)],
                         [scores[i:i + 1, :] for i in range(n_experts)], n_experts)

    in_grp = [jnp.where(best == g, 1.0, 0.0) for g in range(N_GROUPS)]
    grp8 = jnp.concatenate(in_grp + [jnp.zeros((8 - N_GROUPS, tm), F32)], axis=0).astype(BF16)
    row = lax.broadcasted_iota(jnp.int32, (tm, tm), 0)
    col = lax.broadcasted_iota(jnp.int32, (tm, tm), 1)
    upper = jnp.where(row <= col, 1.0, 0.0).astype(BF16)
    run = jnp.dot(grp8, upper, preferred_element_type=F32)
    pos = jnp.zeros((1, tm), F32)
    seg = jnp.zeros((1, 1), F32)
    cnts = []
    for g in range(N_GROUPS):
        cnt = run[g:g + 1, tm - 1:tm]
        cnts.append(cnt)
        pos = pos + in_grp[g] * (seg + run[g:g + 1, :] - 1.0)
        seg = seg + jnp.floor((cnt + (MOE_BLK - 1)) * (1.0 / MOE_BLK)) * MOE_BLK
    cnt_ref[0, 0] = jnp.concatenate([jnp.broadcast_to(c, (1, LANE)) for c in cnts]
                                    + [jnp.zeros((8 - N_GROUPS, LANE), F32)], axis=0)
    tab = jnp.concatenate(gates + [pos, jnp.zeros((LANE - n_experts - 1, tm), F32)], axis=0).T
    tab_ref[0] = tab
    sel = jnp.where(lax.broadcasted_iota(jnp.int32, (ts, tm), 0) == pos.astype(jnp.int32), 1.0, 0.0).astype(BF16)
    hs_ref[0, 0] = jnp.dot(sel, h.astype(BF16), preferred_element_type=F32).astype(BF16)
    t1 = tab.astype(BF16)
    r1 = tab - t1.astype(F32)
    t2 = r1.astype(BF16)
    t3 = (r1 - t2.astype(F32)).astype(BF16)
    gs_ref[0, 0] = (jnp.dot(sel, t1, preferred_element_type=F32) + jnp.dot(sel, t2, preferred_element_type=F32)
                    + jnp.dot(sel, t3, preferred_element_type=F32))


def _moe_expert_kernel(lo_ref, hi_ref, x_ref, mod_ref, hs_ref, gs_ref, tab_ref, w1_ref, w3_ref, w2_ref, o_ref,
                       acc_ref, *, tm, ts, n_ctx, n_experts):
    b = pl.program_id(0)
    t = pl.program_id(1)
    e = pl.program_id(2)
    n_t = pl.num_programs(1)

    @pl.when(e == 0)
    def _():
        acc_ref[...] = jnp.zeros_like(acc_ref)

    per_step = w1_ref.shape[1]
    e_first = e * per_step
    seg = (b * n_t + t) * N_GROUPS + e_first // (n_experts // N_GROUPS)
    lane = lax.broadcasted_iota(jnp.int32, (MOE_BLK, LANE), 1)

    def block(j, carry):
        rows = pl.ds(pl.multiple_of(j * MOE_BLK, MOE_BLK), MOE_BLK)
        hb = hs_ref[0, 0, rows, :]
        gs = gs_ref[0, 0, rows, :]
        hids = []
        for i in range(per_step):
            a = jnp.dot(hb, w1_ref[0, i], preferred_element_type=F32)
            bb = jnp.dot(hb, w3_ref[0, i], preferred_element_type=F32)
            gcol = jnp.sum(jnp.where(lane == e_first + i, gs, 0.0), axis=-1, keepdims=True)
            hids.append((a * _sigmoid(a) * bb * gcol).astype(BF16))
        w2 = w2_ref[0].reshape(per_step * w2_ref.shape[2], w2_ref.shape[3])
        acc_ref[rows, :] += jnp.dot(jnp.concatenate(hids, axis=1), w2, preferred_element_type=F32)
        return carry

    lax.fori_loop(lo_ref[seg], hi_ref[seg], block, 0)

    @pl.when(e == pl.num_programs(2) - 1)
    def _():
        is_ctx = _ctx_rows(t, tm, n_ctx)
        pos = tab_ref[0][:, POS_COL:POS_COL + 1].astype(jnp.int32)
        back = jnp.where(lax.broadcasted_iota(jnp.int32, (tm, ts), 1) == pos, 1.0, 0.0).astype(BF16)
        y = jnp.dot(back, acc_ref[...].astype(BF16), preferred_element_type=F32)
        o_ref[0] = x_ref[0] + _mod_row(mod_ref, 5, is_ctx) * y


def _moe_call(xs, modsel, norm_g, router_wt, router_b, w1, w3, w2, *, layer, n_ctx, tm):
    bsz, tt, d = xs.shape
    n_experts = router_wt.shape[0]
    n_t = tt // tm
    ts = tm + N_GROUPS * MOE_BLK
    row2 = lambda b, t: (b, t, 0)
    const2 = lambda b, t: (0, 0)
    tile2 = lambda b, t: (b, t, 0, 0)
    hs, gs, tab, cnt = pl.pallas_call(
        functools.partial(_moe_route_kernel, tm=tm, ts=ts, n_ctx=n_ctx, n_experts=n_experts),
        out_shape=(jax.ShapeDtypeStruct((bsz, n_t, ts, d), BF16),
                   jax.ShapeDtypeStruct((bsz, n_t, ts, LANE), F32),
                   jax.ShapeDtypeStruct((bsz, tt, LANE), F32),
                   jax.ShapeDtypeStruct((bsz, n_t, 8, LANE), F32)),
        grid=(bsz, n_t),
        in_specs=[pl.BlockSpec((1, tm, d), row2),
                  pl.BlockSpec((1, 2, 8, d), lambda b, t: (b, 0, 0, 0)),
                  pl.BlockSpec((1, d), const2),
                  pl.BlockSpec(router_wt.shape, const2),
                  pl.BlockSpec((n_experts, 1), const2)],
        out_specs=(pl.BlockSpec((1, 1, ts, d), tile2),
                   pl.BlockSpec((1, 1, ts, LANE), tile2),
                   pl.BlockSpec((1, tm, LANE), row2),
                   pl.BlockSpec((1, 1, 8, LANE), tile2)),
        compiler_params=pltpu.CompilerParams(dimension_semantics=("parallel", "parallel"),
                                             vmem_limit_bytes=VMEM_LIMIT),
        name="moe_route",
    )(xs, modsel, norm_g, router_wt, router_b)

    n_blk = (cnt[:, :, :N_GROUPS, 0].astype(jnp.int32) + (MOE_BLK - 1)) // MOE_BLK
    hi = jnp.cumsum(n_blk, axis=-1)
    lo = hi - n_blk

    row = lambda b, t, e, lo_r, hi_r: (b, t, 0)
    wexp = lambda b, t, e, lo_r, hi_r: (layer, e, 0, 0)
    tile4 = lambda b, t, e, lo_r, hi_r: (b, t, 0, 0)
    per_step = n_experts // N_GROUPS if tm <= MOE_FULL_GROUP_MAX_TM else n_experts // N_GROUPS // 2
    return pl.pallas_call(
        functools.partial(_moe_expert_kernel, tm=tm, ts=ts, n_ctx=n_ctx, n_experts=n_experts),
        out_shape=jax.ShapeDtypeStruct((bsz, tt, d), F32),
        grid_spec=pltpu.PrefetchScalarGridSpec(
            num_scalar_prefetch=2,
            grid=(bsz, n_t, n_experts // per_step),
            in_specs=[pl.BlockSpec((1, tm, d), row),
                      pl.BlockSpec((1, 2, 8, d), lambda b, t, e, lo_r, hi_r: (b, 0, 0, 0)),
                      pl.BlockSpec((1, 1, ts, d), tile4),
                      pl.BlockSpec((1, 1, ts, LANE), tile4),
                      pl.BlockSpec((1, tm, LANE), row),
                      pl.BlockSpec((1, per_step) + w1.shape[2:], wexp),
                      pl.BlockSpec((1, per_step) + w3.shape[2:], wexp),
                      pl.BlockSpec((1, per_step) + w2.shape[2:], wexp)],
            out_specs=pl.BlockSpec((1, tm, d), row),
            scratch_shapes=[pltpu.VMEM((ts, d), F32)]),
        compiler_params=pltpu.CompilerParams(
            dimension_semantics=("parallel", "parallel", "arbitrary"),
            vmem_limit_bytes=VMEM_LIMIT),
        name="moe_experts",
    )(lo.reshape(-1), hi.reshape(-1), xs, modsel, hs, gs, tab, w1, w3, w2)


def kernel(x, c, ctx, c_ctx, mod_w, mod_b, norm1_g, norm2_g, w_in, rw_w0, rw_w_lora_b, rw_a0, rw_a_lora_b,
           rw_g_lora_b, rw_k_k, rw_k_a, rw_r_k, rw_ln_g, rw_ln_b, na_q_g, na_k_g, na_rpb, proj_a, proj_b,
           w_out, router_w, router_bias, moe_w1, moe_w3, moe_w2):
    bsz, seq, d = x.shape
    n_ctx = ctx.shape[1]
    depth = mod_w.shape[0]
    d_a = rw_w0.shape[2]
    d_b = proj_b.shape[1]
    lora_g = rw_g_lora_b.shape[1]
    heads_b = d_b // HEAD_DIM
    assert seq % (GRID_W * WIN_H) == 0 and n_ctx % 256 == 0 and bsz + 1 <= 16

    cs = jnp.zeros((16, d), F32).at[:bsz].set(c).at[bsz].set(c_ctx)
    mod = _mod_call(cs, mod_w, mod_b).reshape(depth, 16, 6, d)

    lane = jnp.arange(LANE)
    gs = (lane[:, None] // HEAD_DIM == lane[None, :] // HEAD_DIM).astype(BF16)
    router_wt = router_w.T
    router_b = router_bias.reshape(-1, 1)

    w_in_b, w1_b, w3_b, w2_b = (_to_bf16(w) for w in (w_in, moe_w1, moe_w3, moe_w2))
    xs = (ctx, x)
    for l in range(depth):
        last = l == depth - 1
        m_c = jnp.broadcast_to(mod[l, bsz][None], (bsz, 6, d))
        modsel = jnp.pad(jnp.stack([m_c, mod[l, :bsz]], axis=1), ((0, 0), (0, 0), (0, 2), (0, 0)))

        rkv, lg, qkv, gates, op_f, op_r, v_a, pe_f, pe_r = _in_proj_call(
            xs, modsel, norm1_g[l][None], w_in_b, rw_k_k[l][None],
            jnp.tile(na_q_g[l], heads_b)[None], jnp.tile(na_k_g[l], heads_b)[None], gs,
            rw_w0[l], rw_w_lora_b[l], rw_a0[l], rw_a_lora_b[l], rw_k_a[l][None],
            layer=l, n_ctx=n_ctx, d_a=d_a, d_b=d_b, lora_g=lora_g)
        y_f, y_r = _rwkv_call(op_f, op_r, v_a, pe_f, pe_r, n_ctx=n_ctx)
        yb = _attn_call(qkv, _bias_table(na_rpb[l]), n_ctx=n_ctx, with_ctx=not last)
        xs = _merge_call(xs, modsel, y_f, y_r, rkv, lg, yb, gates, rw_r_k[l].reshape(1, d_a),
                         rw_ln_g[l][None], rw_ln_b[l][None], rw_g_lora_b[l].astype(BF16),
                         proj_a[l].astype(BF16), proj_b[l].astype(BF16), w_out[l].astype(BF16), gs,
                         n_ctx=n_ctx, with_ctx=not last)
        moe_ctx = 0 if last else n_ctx
        tm = next(m for m in (1024, 768, 512, 256) if xs.shape[1] % m == 0)
        xs = _moe_call(xs, modsel, norm2_g[l][None], router_wt, router_b, w1_b, w3_b, w2_b,
                       layer=l, n_ctx=moe_ctx, tm=tm)
    return xs
```

```python
import functools
import math

import jax
import jax.numpy as jnp
import numpy as np
from jax import lax
from jax.experimental import pallas as pl
from jax.experimental.pallas import tpu as pltpu

F32 = jnp.float32
BF16 = jnp.bfloat16

HEAD_DIM = 64
GRID_W = 64
WIN_H = 8
WIN_W = 16
N_GROUPS = 4
TOP_K = 2
RMS_EPS = 1e-6
GN_EPS = 64e-5
LANE = 128
CHUNK = 64
SUB = 16
NEG_BIG = -1e30
DECAY_SCALE = math.exp(-0.5)
VMEM_LIMIT = 58 * 1024 * 1024


def _bdot(a, b):
    return jnp.dot(a.astype(BF16), b.astype(BF16), preferred_element_type=F32)


def _bdot_nt(a, b):
    return lax.dot_general(a.astype(BF16), b.astype(BF16), (((1,), (1,)), ((), ())),
                           preferred_element_type=F32)


def _bdot_tn(a, b):
    return lax.dot_general(a.astype(BF16), b.astype(BF16), (((0,), (0,)), ((), ())),
                           preferred_element_type=F32)


def _split(x):
    hi = x.astype(BF16)
    lo = (x - hi.astype(F32)).astype(BF16)
    return hi, lo


def _dot3(a, b):
    ah, al = _split(a)
    bh, bl = _split(b)
    return (jnp.dot(ah, bh, preferred_element_type=F32)
            + jnp.dot(al, bh, preferred_element_type=F32)
            + jnp.dot(ah, bl, preferred_element_type=F32))


def _sigmoid(x):
    return 1.0 / (1.0 + jnp.exp(-x))


def _group_sum(x, g128):
    parts = [_bdot(x[:, j * LANE:(j + 1) * LANE], g128) for j in range(x.shape[1] // LANE)]
    return jnp.concatenate(parts, axis=1)


def _group_sum2(x, g128):
    hi, lo = _split(x)
    parts = []
    for j in range(x.shape[1] // LANE):
        sl = slice(j * LANE, (j + 1) * LANE)
        parts.append(jnp.dot(hi[:, sl], g128, preferred_element_type=F32)
                     + jnp.dot(lo[:, sl], g128, preferred_element_type=F32))
    return jnp.concatenate(parts, axis=1)


def _mod_row(mod_ref, idx, is_ctx):
    mx = mod_ref[0, 1, idx:idx + 1, :]
    if is_ctx is None:
        return mx
    return jnp.where(is_ctx, mod_ref[0, 0, idx:idx + 1, :], mx)


def _ctx_rows(tile, tm, n_ctx):
    if n_ctx == 0:
        return None
    rows = tile * tm + lax.broadcasted_iota(jnp.int32, (tm, 1), 0)
    return rows < n_ctx


def _token_sources(src, tm, n_ctx, t_off):
    assert n_ctx % tm == 0
    n_ct = n_ctx // tm
    if isinstance(src, tuple):
        ctx, x = src
        d = x.shape[2]
        lat = pl.BlockSpec((1, tm, d), lambda b, t: (b, jnp.maximum(t + t_off - n_ct, 0), 0))
    else:
        ctx = x = src
        d = x.shape[2]
        lat = pl.BlockSpec((1, tm, d), lambda b, t: (b, t + t_off, 0))
    con = pl.BlockSpec((1, tm, d), lambda b, t: (b, jnp.minimum(t + t_off, n_ct - 1), 0))
    return (ctx, x), [con, lat]


def _token_tile(ctx_ref, lat_ref, tile, tm, n_ctx):
    return jnp.where(tile * tm < n_ctx, ctx_ref[0], lat_ref[0])


def _rms_mod(x, gain, shift, scale):
    xn = x * lax.rsqrt(jnp.mean(x * x, axis=-1, keepdims=True) + RMS_EPS) * gain
    return xn * (1.0 + scale) + shift


CAST_BLOCK_BYTES = 8 * 1024 * 1024


def _cast_kernel(w_ref, o_ref):
    o_ref[...] = w_ref[...].astype(o_ref.dtype)


def _to_bf16(w):
    c = w.shape[-1]
    w2 = w.reshape(-1, c)
    rows = w2.shape[0]
    rb = max(16, min(rows, CAST_BLOCK_BYTES // (c * 4)) // 16 * 16)
    while rows % rb:
        rb -= 16
    out = pl.pallas_call(
        _cast_kernel,
        out_shape=jax.ShapeDtypeStruct(w2.shape, BF16),
        grid=(rows // rb,),
        in_specs=[pl.BlockSpec((rb, c), lambda i: (i, 0))],
        out_specs=pl.BlockSpec((rb, c), lambda i: (i, 0)),
        compiler_params=pltpu.CompilerParams(dimension_semantics=("parallel",), vmem_limit_bytes=VMEM_LIMIT),
        name="to_bf16",
    )(w2)
    return out.reshape(w.shape)


def _mod_kernel(c_ref, w_ref, b_ref, o_ref):
    c = c_ref[...]
    o_ref[0] = _dot3(c * _sigmoid(c), w_ref[0]) + b_ref[0]


def _mod_call(cs, mod_w, mod_b):
    depth, d, n = mod_w.shape
    tn = n // 4
    return pl.pallas_call(
        _mod_kernel,
        out_shape=jax.ShapeDtypeStruct((depth, cs.shape[0], n), F32),
        grid=(depth, n // tn),
        in_specs=[pl.BlockSpec(cs.shape, lambda l, j: (0, 0)),
                  pl.BlockSpec((1, d, tn), lambda l, j: (l, 0, j)),
                  pl.BlockSpec((1, 1, tn), lambda l, j: (l, 0, j))],
        out_specs=pl.BlockSpec((1, cs.shape[0], tn), lambda l, j: (l, 0, j)),
        compiler_params=pltpu.CompilerParams(dimension_semantics=("parallel", "parallel"),
                                             vmem_limit_bytes=VMEM_LIMIT),
        name="adaln_vectors",
    )(cs, mod_w, mod_b.reshape(depth, 1, n))


def _in_proj_kernel(xc_ref, x_ref, mod_ref, g_ref, w_ref, kkg_ref, qg_ref, kg_ref, gs_ref,
                    w0_ref, wlbh_ref, wlbl_ref, a0_ref, albh_ref, albl_ref, ka_ref,
                    rkv_ref, lg_ref, qkv_ref, gates_ref, opf_ref, opr_ref, v_ref, pef_ref, per_ref,
                    *, tm, n_ctx, d_a, d_b, lora_w, lora_a, lora_g):
    t = pl.program_id(1)
    is_ctx = _ctx_rows(t, tm, n_ctx)
    h = _rms_mod(_token_tile(xc_ref, x_ref, t, tm, n_ctx), g_ref[...],
                 _mod_row(mod_ref, 0, is_ctx), _mod_row(mod_ref, 1, is_ctx))
    h = h.astype(BF16)
    gs = gs_ref[...]
    n_lora = 2 * lora_w + 2 * lora_a + lora_g

    o = 0
    rkv = jnp.dot(h, w_ref[0, :, o:o + 3 * d_a], preferred_element_type=F32)
    rkv_ref[0] = rkv
    kkv = rkv[:, d_a:2 * d_a] * kkg_ref[...]
    ss = _group_sum2(kkv * kkv, gs)
    kk = kkv * lax.rsqrt(jnp.maximum(ss, 1e-12))
    o += 3 * d_a

    lora = jnp.dot(h, w_ref[0, :, o:o + n_lora], preferred_element_type=F32)
    lg_ref[0] = lora[:, n_lora - lora_g:]
    v_ref[0] = rkv[:, 2 * d_a:].astype(BF16)
    o += n_lora

    qkv = jnp.dot(h, w_ref[0, :, o:o + 3 * d_b], preferred_element_type=F32)
    q, k = qkv[:, :d_b], qkv[:, d_b:2 * d_b]
    inv_n = 1.0 / HEAD_DIM
    qn = q * lax.rsqrt(_group_sum(q * q, gs) * inv_n + RMS_EPS) * qg_ref[...] * (HEAD_DIM ** -0.5)
    kn = k * lax.rsqrt(_group_sum(k * k, gs) * inv_n + RMS_EPS) * kg_ref[...]
    qkv_ref[0] = jnp.concatenate([qn, kn, qkv[:, 2 * d_b:]], axis=1).astype(BF16)
    o += 3 * d_b

    gates_ref[0] = _sigmoid(jnp.dot(h, w_ref[0, :, o:], preferred_element_type=F32)).astype(BF16)
    _rwkv_operand_rows((0, 1), rkv, kk, lora, w0_ref, wlbh_ref, wlbl_ref, a0_ref, albh_ref, albl_ref, ka_ref,
                       opf_ref, opr_ref, pef_ref, per_ref, tm=tm, d_a=d_a, lora_w=lora_w, lora_a=lora_a)


def _in_proj_call(xs, modsel, norm_g, w_in, k_k, q_g, k_g, gs, w0, wlb, a0, alb, k_a,
                  *, layer, n_ctx, d_a, d_b, lora_g, tm=256):
    tok_arrays, tok_specs = _token_sources(xs, tm, n_ctx, 0)
    bsz, d = tok_arrays[1].shape[0], tok_arrays[1].shape[2]
    tt = sum(a.shape[1] for a in xs) if isinstance(xs, tuple) else xs.shape[1]
    lora_w, lora_a = wlb.shape[1], alb.shape[1]
    n_lora = 2 * lora_w + 2 * lora_a + lora_g
    n_gate = w_in.shape[2] - 3 * d_a - n_lora - 3 * d_b
    row = lambda b, t: (b, t, 0)
    const = lambda b, t: (0, 0)
    const3 = lambda b, t: (0, 0, 0)
    wlb_hi, wlb_lo = _split(wlb)
    alb_hi, alb_lo = _split(alb)
    op_shape = jax.ShapeDtypeStruct((bsz, tt, N_OPERANDS * d_a), BF16)
    op_spec = pl.BlockSpec((1, tm, N_OPERANDS * d_a), row)
    pe_shape = jax.ShapeDtypeStruct((bsz, tt // CHUNK, 8, d_a), F32)
    pe_spec = pl.BlockSpec((1, tm // CHUNK, 8, d_a), lambda b, t: (b, t, 0, 0))
    return pl.pallas_call(
        functools.partial(_in_proj_kernel, tm=tm, n_ctx=n_ctx, d_a=d_a, d_b=d_b,
                          lora_w=lora_w, lora_a=lora_a, lora_g=lora_g),
        out_shape=(jax.ShapeDtypeStruct((bsz, tt, 3 * d_a), F32),
                   jax.ShapeDtypeStruct((bsz, tt, lora_g), F32),
                   jax.ShapeDtypeStruct((bsz, tt, 3 * d_b), BF16),
                   jax.ShapeDtypeStruct((bsz, tt, n_gate), BF16),
                   op_shape, op_shape, jax.ShapeDtypeStruct((bsz, tt, d_a), BF16), pe_shape, pe_shape),
        grid=(bsz, tt // tm),
        in_specs=tok_specs + [
                  pl.BlockSpec((1, 2, 8, d), lambda b, t: (b, 0, 0, 0)),
                  pl.BlockSpec((1, d), const),
                  pl.BlockSpec((1,) + w_in.shape[1:], lambda b, t: (layer, 0, 0)),
                  pl.BlockSpec((1, d_a), const),
                  pl.BlockSpec((1, d_b), const),
                  pl.BlockSpec((1, d_b), const),
                  pl.BlockSpec((LANE, LANE), const),
                  pl.BlockSpec((2, 1, d_a), const3),
                  pl.BlockSpec((2, lora_w, d_a), const3),
                  pl.BlockSpec((2, lora_w, d_a), const3),
                  pl.BlockSpec((2, 1, d_a), const3),
                  pl.BlockSpec((2, lora_a, d_a), const3),
                  pl.BlockSpec((2, lora_a, d_a), const3),
                  pl.BlockSpec((1, d_a), const)],
        out_specs=(pl.BlockSpec((1, tm, 3 * d_a), row),
                   pl.BlockSpec((1, tm, lora_g), row),
                   pl.BlockSpec((1, tm, 3 * d_b), row),
                   pl.BlockSpec((1, tm, n_gate), row),
                   op_spec, op_spec, pl.BlockSpec((1, tm, d_a), row), pe_spec, pe_spec),
        compiler_params=pltpu.CompilerParams(dimension_semantics=("parallel", "parallel"),
                                             vmem_limit_bytes=VMEM_LIMIT),
        name="in_proj",
    )(*tok_arrays, modsel, norm_g, w_in, k_k, q_g, k_g, gs,
      w0.reshape(2, 1, d_a), wlb_hi, wlb_lo, a0.reshape(2, 1, d_a), alb_hi, alb_lo, k_a)


N_OPERANDS = 4


def _dot3_presplit(x, w_hi, w_lo):
    xh, xl = _split(x)
    return (jnp.dot(xh, w_hi, preferred_element_type=F32) + jnp.dot(xl, w_hi, preferred_element_type=F32)
            + jnp.dot(xh, w_lo, preferred_element_type=F32))


def _rwkv_operand_rows(dirs, rkv, kk, lora, w0_ref, wlbh_ref, wlbl_ref, a0_ref, albh_ref, albl_ref, ka_ref,
                       opf_ref, opr_ref, pef_ref, per_ref, *, tm, d_a, lora_w, lora_a):
    r, k = rkv[:, :d_a], rkv[:, d_a:2 * d_a]
    chunk_bits = int(math.log2(CHUNK))
    row = lax.broadcasted_iota(jnp.int32, (tm, tm), 0)
    col = lax.broadcasted_iota(jnp.int32, (tm, tm), 1)
    same = (row >> chunk_bits) == (col >> chunk_bits)
    for d in dirs:
        op_ref, pe_ref = ((opf_ref, pef_ref), (opr_ref, per_ref))[d]
        lw = lora[:, d * lora_w:(d + 1) * lora_w]
        la = lora[:, 2 * lora_w + d * lora_a:2 * lora_w + (d + 1) * lora_a]
        wl = w0_ref[d] + _dot3_presplit(jnp.tanh(lw), wlbh_ref[d], wlbl_ref[d])
        ld = -DECAY_SCALE * _sigmoid(wl)
        al = _sigmoid(a0_ref[d] + _dot3_presplit(la, albh_ref[d], albl_ref[d]))
        kd = k * (1.0 + (al - 1.0) * ka_ref[...])
        before = (row <= col) if d else (row >= col)
        tri = jnp.where(same & before, 1.0, 0.0).astype(BF16)
        ld_hi, ld_lo = _split(ld)
        cum = jnp.dot(tri, ld_hi, preferred_element_type=F32) + jnp.dot(tri, ld_lo, preferred_element_type=F32)
        p_inv = jnp.exp(-cum)
        op_ref[0] = jnp.concatenate([-kk * jnp.exp(cum - ld), r * jnp.exp(cum), kk * al * p_inv, kd * p_inv],
                                    axis=1).astype(BF16)
        for ci in range(tm // CHUNK):
            last = ci * CHUNK + (0 if d else CHUNK - 1)
            pe_ref[0, ci] = jnp.exp(jnp.broadcast_to(cum[last:last + 1, :], (8, d_a)))


def _rwkv_pair_kernel(opf_ref, opr_ref, vf_ref, vr_ref, pef_ref, per_ref, yf_ref, yr_ref, st_ref, *, d_a):
    n = HEAD_DIM
    pw = 2 * n
    pairs = d_a // pw
    c = CHUNK
    nb = opf_ref.shape[0]

    @pl.when(pl.program_id(1) == 0)
    def _():
        st_ref[...] = jnp.zeros_like(st_ref)

    chains = [(b, d, p) for b in range(nb) for d in range(2) for p in range(pairs)]
    op_refs = (opf_ref, opr_ref)
    part = lambda i: [op_refs[d][b, :, i * d_a + p * pw:i * d_a + (p + 1) * pw] for b, d, p in chains]
    v_refs = (vf_ref, vr_ref)
    pe_refs = (pef_ref, per_ref)
    v = [v_refs[d][b, :, p * pw:(p + 1) * pw] for b, d, p in chains]
    p_end = [pe_refs[d][b, 0, 0:1, p * pw:(p + 1) * pw] for b, d, p in chains]
    each = lambda f, *ls: [f(*xs) for xs in zip(*ls)]
    rows2 = lambda a, b: jnp.concatenate([a, b], axis=0)
    bf = lambda t: t.astype(BF16)

    def bd(x):
        x = bf(x)
        left = lax.broadcasted_iota(jnp.int32, x.shape, 1) < n
        zero = jnp.zeros_like(x)
        return rows2(jnp.where(left, x, zero), jnp.where(left, zero, x))

    pdot = lambda a, b: jnp.dot(bf(a), bd(b), preferred_element_type=F32)
    ar = each(rows2, part(0), part(1))
    bk = each(rows2, part(2), part(3))
    b_k = each(lambda w: rows2(bd(w[:c]), bd(w[c:])), bk)
    s = [st_ref[b, d, p] for b, d, p in chains]

    mm_all = each(lambda a, w, ss: _bdot_nt(a, rows2(w, bd(ss))), ar, b_k, s)
    row = lax.broadcasted_iota(jnp.int32, (c, 2 * pw), 0)
    col = lax.broadcasted_iota(jnp.int32, (c, 2 * pw), 1) & (c - 1)
    incl = [(row <= col) if d else (row >= col) for _, d, _ in chains]
    strict = [(row < col) if d else (row > col) for _, d, _ in chains]
    m_a = each(lambda mm, msk: jnp.where(msk, mm[:c, :2 * pw], 0.0), mm_all, strict)
    m_r = each(lambda mm, msk: jnp.where(msk, mm[c:, :2 * pw], 0.0), mm_all, incl)
    ms_a = each(lambda mm: mm[:c, 2 * pw:], mm_all)
    ms_r = each(lambda mm: mm[c:, 2 * pw:], mm_all)
    a_ab = each(lambda mm: mm[:, :pw], m_a)
    a_ak = each(lambda mm: mm[:, pw:], m_a)

    row = lax.broadcasted_iota(jnp.int32, (c, pw), 0)
    col = lax.broadcasted_iota(jnp.int32, (c, pw), 1) & (c - 1)
    eye = jnp.where(row == col, 1.0, 0.0)
    sub_bits = int(math.log2(SUB))
    blk = (row >> sub_bits) == (col >> sub_bits)
    ad = each(lambda a: jnp.where(blk, a, 0.0), a_ab)
    ao = each(lambda a, b: a - b, a_ab, ad)
    td = each(lambda a: eye + a, ad)
    x = each(pdot, ad, ad)
    for _ in range(sub_bits - 2):
        res = each(lambda xx, tt: pdot(rows2(xx, tt), xx), x, td)
        td = each(lambda tt, rr: tt + rr[c:], td, res)
        x = each(lambda rr: rr[:c], res)
    td = each(lambda tt, xx: tt + pdot(tt, xx), td, x)
    x = each(pdot, td, ao)
    w = td
    for _ in range(int(math.log2(c // SUB)) - 1):
        res = each(lambda xx, ww: jnp.dot(bf(xx), jnp.concatenate([bd(xx), bd(ww)], axis=1),
                                          preferred_element_type=F32), x, w)
        w = each(lambda ww, rr: ww + rr[:, pw:], w, res)
        x = each(lambda rr: rr[:, :pw], res)
    t_inv = each(lambda ww, xx: ww + pdot(xx, ww), w, x)

    akv = each(pdot, a_ak, v)
    u = each(lambda t, m0, m1: bf(pdot(t, m0 + m1)), t_inv, ms_a, akv)
    y = each(lambda m0, a, uu, vv: m0 + jnp.dot(bf(a), rows2(bd(uu), bd(vv)), preferred_element_type=F32),
             ms_r, m_r, u, v)
    left = lax.broadcasted_iota(jnp.int32, (n, pw), 1) < n
    upd = each(lambda uu, vv, w: _bdot_tn(rows2(uu, vv), w), u, v, bk)
    s_new = each(lambda ss, p, dd: (ss + jnp.where(left, dd[:n], dd[n:])) * p, s, p_end, upd)
    for (b, d, p), val in zip(chains, s_new):
        st_ref[b, d, p] = val
    for b in range(nb):
        yf_ref[b] = jnp.concatenate(y[2 * b * pairs:(2 * b + 1) * pairs], axis=1)
        yr_ref[b] = jnp.concatenate(y[(2 * b + 1) * pairs:(2 * b + 2) * pairs], axis=1)


def _rwkv_call(op_f, op_r, v, pe_f, pe_r, *, n_ctx):
    bsz, tt, d_a = v.shape
    n_chunks = tt // CHUNK
    nc_ctx = n_ctx // CHUNK
    heads = d_a // HEAD_DIM

    rev_chunk = lambda j: jnp.where(j < nc_ctx, nc_ctx - 1 - j, n_chunks - 1 - (j - nc_ctx))
    fwd = lambda b, j: (b, j, 0)
    rev = lambda b, j: (b, rev_chunk(j), 0)
    y_shape = jax.ShapeDtypeStruct((bsz, tt, d_a), F32)
    nb = next(m for m in (8, 4, 2, 1) if bsz % m == 0)
    return pl.pallas_call(
        functools.partial(_rwkv_pair_kernel, d_a=d_a),
        out_shape=(y_shape, y_shape),
        grid=(bsz // nb, n_chunks),
        in_specs=[pl.BlockSpec((nb, CHUNK, N_OPERANDS * d_a), fwd),
                  pl.BlockSpec((nb, CHUNK, N_OPERANDS * d_a), rev),
                  pl.BlockSpec((nb, CHUNK, d_a), fwd),
                  pl.BlockSpec((nb, CHUNK, d_a), rev),
                  pl.BlockSpec((nb, 1, 8, d_a), lambda b, j: (b, j, 0, 0)),
                  pl.BlockSpec((nb, 1, 8, d_a), lambda b, j: (b, rev_chunk(j), 0, 0))],
        out_specs=(pl.BlockSpec((nb, CHUNK, d_a), fwd), pl.BlockSpec((nb, CHUNK, d_a), rev)),
        scratch_shapes=[pltpu.VMEM((nb, 2, heads // 2, HEAD_DIM, 2 * HEAD_DIM), F32)],
        compiler_params=pltpu.CompilerParams(dimension_semantics=("parallel", "arbitrary"),
                                             vmem_limit_bytes=VMEM_LIMIT),
        name="rwkv7_chunk_scan",
    )(op_f, op_r, v, v, pe_f, pe_r)


def _softmax_pv(heads_parts):
    ms = [functools.reduce(jnp.maximum, [jnp.max(s, axis=-1, keepdims=True) for s, _ in parts])
          for parts in heads_parts]
    ps = [[jnp.exp(s - m) for s, _ in parts] for parts, m in zip(heads_parts, ms)]
    dens = [sum(jnp.sum(p, axis=-1, keepdims=True) for p in pp) for pp in ps]
    nums = [sum(jnp.dot(p.astype(BF16), vals, preferred_element_type=F32) for p, (_, vals) in zip(pp, parts))
            for pp, parts in zip(ps, heads_parts)]
    return [num / den for num, den in zip(nums, dens)]


def _attn_kernel(q_ref, k_ref, v_ref, bias_ref, o_ref, *, n_ctx, rows, q_off, d_b):
    n = HEAD_DIM
    heads = d_b // n
    nq_ctx = n_ctx // GRID_W
    i = pl.program_id(1) + q_off
    nb = q_ref.shape[0]
    q = [q_ref[b] for b in range(nb)]
    kc = [k_ref[b, 0:n_ctx, :] for b in range(nb)]
    vc = [v_ref[b, 0:n_ctx, :] for b in range(nb)]
    units = [(b, h, slice(h * n, (h + 1) * n)) for b in range(nb) for h in range(heads)]

    def store(outs):
        for b in range(nb):
            o_ref[b] = jnp.concatenate(outs[b * heads:(b + 1) * heads], axis=1).astype(o_ref.dtype)

    def latent():
        li = i - nq_ctx
        r0 = jnp.clip(li - WIN_H // 2, 0, rows - WIN_H)
        start = pl.multiple_of(n_ctx + r0 * GRID_W, GRID_W)
        kl = [k_ref[b, pl.ds(start, WIN_H * GRID_W), :] for b in range(nb)]
        vl = [v_ref[b, pl.ds(start, WIN_H * GRID_W), :] for b in range(nb)]
        s_l = [_bdot_nt(q[b][:, sl], kl[b][:, sl]) + bias_ref[0, h] for b, h, sl in units]
        s_c = [_bdot_nt(q[b][:, sl], kc[b][:, sl]) for b, h, sl in units]
        store(_softmax_pv([[(sa, vl[b][:, sl]), (sb, vc[b][:, sl])]
                           for sa, sb, (b, h, sl) in zip(s_l, s_c, units)]))

    def context():
        store(_softmax_pv([[(_bdot_nt(q[b][:, sl], kc[b][:, sl]), vc[b][:, sl])] for b, h, sl in units]))

    if q_off == 0:
        pl.when(i >= nq_ctx)(latent)
        pl.when(i < nq_ctx)(context)
    else:
        latent()


def _bias_table(rpb):
    w = GRID_W
    heads = rpb.shape[0]
    j = np.arange(w)
    col_start = np.clip(j - WIN_W // 2, 0, w - WIN_W)
    in_win = (j[None, :] >= col_start[:, None]) & (j[None, :] < col_start[:, None] + WIN_W)
    rpb = rpb.astype(F32)
    ext = jnp.concatenate([jnp.repeat(rpb[..., :1], w - WIN_W, axis=-1), rpb,
                           jnp.repeat(rpb[..., -1:], w - WIN_W, axis=-1)], axis=-1)
    toe = jnp.stack([ext[..., w - 1 - qc:2 * w - 1 - qc] for qc in range(w)], axis=2)
    tz = jnp.where(in_win, toe, NEG_BIG)
    bt = jnp.stack([tz[:, v:v + WIN_H] for v in range(WIN_H)], axis=0)
    return bt.transpose(0, 1, 3, 2, 4).reshape(WIN_H, heads, w, WIN_H * w)


def _attn_call(qkv, bias_tab, *, n_ctx, with_ctx):
    bsz, tt, d3 = qkv.shape
    d_b = d3 // 3
    heads = d_b // HEAD_DIM
    rows = (tt - n_ctx) // GRID_W
    nq_ctx = n_ctx // GRID_W
    q_off = 0 if with_ctx else nq_ctx
    n_q = tt // GRID_W - q_off
    nb = next(m for m in (4, 2, 1) if bsz % m == 0)

    def variant(b, i):
        li = jnp.maximum(i + q_off - nq_ctx, 0)
        return (jnp.clip(li - WIN_H // 2, 0, rows - WIN_H) - li + WIN_H - 1, 0, 0, 0)

    return pl.pallas_call(
        functools.partial(_attn_kernel, n_ctx=n_ctx, rows=rows, q_off=q_off, d_b=d_b),
        out_shape=jax.ShapeDtypeStruct((bsz, n_q * GRID_W, d_b), BF16),
        grid=(bsz // nb, n_q),
        in_specs=[pl.BlockSpec((nb, GRID_W, d_b), lambda b, i: (b, i + q_off, 0)),
                  pl.BlockSpec((nb, tt, d_b), lambda b, i: (b, 0, 1)),
                  pl.BlockSpec((nb, tt, d_b), lambda b, i: (b, 0, 2)),
                  pl.BlockSpec((1, heads, GRID_W, WIN_H * GRID_W), variant)],
        out_specs=pl.BlockSpec((nb, GRID_W, d_b), lambda b, i: (b, i, 0)),
        compiler_params=pltpu.CompilerParams(dimension_semantics=("parallel", "arbitrary"),
                                             vmem_limit_bytes=VMEM_LIMIT),
        name="neighbourhood_attention",
    )(qkv, qkv, qkv, bias_tab)


def _merge_kernel(xc_ref, x_ref, mod_ref, y0_ref, y1_ref, rkv_ref, lg_ref, yb_ref, gates_ref,
                  rk_ref, lng_ref, lnb_ref, glb_ref, pa_ref, pb_ref, wo_ref, gs_ref, o_ref,
                  *, tm, n_ctx, t_off, d_a):
    t = pl.program_id(1) + t_off
    is_ctx = _ctx_rows(t, tm, n_ctx)
    gs = gs_ref[...]
    inv_n = 1.0 / HEAD_DIM
    y = y0_ref[0] + y1_ref[0]
    mu = _group_sum2(y, gs) * inv_n
    yc = y - mu
    var = _group_sum2(yc * yc, gs) * inv_n
    yn = yc * lax.rsqrt(var + GN_EPS) * lng_ref[...] + lnb_ref[...]
    rkv = rkv_ref[0]
    r, k, v = rkv[:, :d_a], rkv[:, d_a:2 * d_a], rkv[:, 2 * d_a:]
    bonus = _group_sum2(r * k * rk_ref[...], gs) * v
    g = _bdot(_sigmoid(lg_ref[0]), glb_ref[...])
    ya = (yn + bonus) * g
    gates = gates_ref[0].astype(F32)
    dm = gates.shape[1] // 2
    merged = gates[:, :dm] * _bdot(ya, pa_ref[...]) + gates[:, dm:] * _bdot(yb_ref[0], pb_ref[...])
    o_ref[0] = (_token_tile(xc_ref, x_ref, t, tm, n_ctx)
                + _mod_row(mod_ref, 2, is_ctx) * _bdot(merged, wo_ref[...]))


def _merge_call(xs, modsel, y_f, y_r, rkv, lg, yb, gates, r_k, ln_g, ln_b, glb, pa, pb, wo, gs,
                *, n_ctx, with_ctx, tm=256):
    bsz, tt, d_a3 = rkv.shape
    d_a = d_a3 // 3
    t_off = 0 if with_ctx else n_ctx // tm
    n_t = tt // tm - t_off
    tok_arrays, tok_specs = _token_sources(xs, tm, n_ctx, t_off)
    d = tok_arrays[1].shape[2]
    yb_off = 0 if with_ctx else -t_off
    row = lambda b, t: (b, t + t_off, 0)
    const = lambda b, t: (0, 0)
    return pl.pallas_call(
        functools.partial(_merge_kernel, tm=tm, n_ctx=n_ctx, t_off=t_off, d_a=d_a),
        out_shape=jax.ShapeDtypeStruct((bsz, n_t * tm, d), F32),
        grid=(bsz, n_t),
        in_specs=tok_specs + [
                  pl.BlockSpec((1, 2, 8, d), lambda b, t: (b, 0, 0, 0)),
                  pl.BlockSpec((1, tm, d_a), row),
                  pl.BlockSpec((1, tm, d_a), row),
                  pl.BlockSpec((1, tm, 3 * d_a), row),
                  pl.BlockSpec((1, tm, lg.shape[2]), row),
                  pl.BlockSpec((1, tm, yb.shape[2]), lambda b, t: (b, t + t_off + yb_off, 0)),
                  pl.BlockSpec((1, tm, gates.shape[2]), row),
                  pl.BlockSpec((1, d_a), const),
                  pl.BlockSpec((1, d_a), const),
                  pl.BlockSpec((1, d_a), const),
                  pl.BlockSpec(glb.shape, const),
                  pl.BlockSpec(pa.shape, const),
                  pl.BlockSpec(pb.shape, const),
                  pl.BlockSpec(wo.shape, const),
                  pl.BlockSpec((LANE, LANE), const)],
        out_specs=pl.BlockSpec((1, tm, d), lambda b, t: (b, t, 0)),
        compiler_params=pltpu.CompilerParams(dimension_semantics=("parallel", "parallel"),
                                             vmem_limit_bytes=VMEM_LIMIT),
        name="branch_merge",
    )(*tok_arrays, modsel, y_f, y_r, rkv, lg, yb, gates, r_k, ln_g, ln_b, glb, pa, pb, wo, gs)


def _route(sel, scores, n_experts):
    per = n_experts // N_GROUPS
    in_top = []
    for e in range(n_experts):
        g0 = (e // per) * per
        rank = 0.0
        for o in range(g0, g0 + per):
            if o == e:
                continue
            ahead = (sel[o] >= sel[e]) if o < e else (sel[o] > sel[e])
            rank = rank + jnp.where(ahead, 1.0, 0.0)
        in_top.append(rank < TOP_K)
    grp = [sum(jnp.where(in_top[e], sel[e], 0.0) for e in range(g * per, (g + 1) * per))
           for g in range(N_GROUPS)]
    best = jnp.zeros_like(grp[0], dtype=jnp.int32)
    best_s = grp[0]
    for g in range(1, N_GROUPS):
        better = grp[g] > best_s
        best = jnp.where(better, g, best)
        best_s = jnp.where(better, grp[g], best_s)
    chosen = [in_top[e] & (best == e // per) for e in range(n_experts)]
    den = sum(jnp.where(chosen[e], scores[e], 0.0) for e in range(n_experts))
    return [jnp.where(chosen[e], scores[e] / den, 0.0) for e in range(n_experts)], best


MOE_FULL_GROUP_MAX_TM = 1024
MOE_BLK = 128
POS_COL = 16


def _moe_route_kernel(x_ref, mod_ref, g_ref, rw_ref, rb_ref, hs_ref, gs_ref, tab_ref, cnt_ref,
                      *, tm, ts, n_ctx, n_experts):
    t = pl.program_id(1)
    is_ctx = _ctx_rows(t, tm, n_ctx)
    h = _rms_mod(x_ref[0], g_ref[...], _mod_row(mod_ref, 3, is_ctx), _mod_row(mod_ref, 4, is_ctx))
    hh, hl = _split(h)
    wh, wl = _split(rw_ref[...])
    nt = lambda a, b: lax.dot_general(a, b, (((1,), (1,)), ((), ())), preferred_element_type=F32)
    logits = nt(wh, hh) + nt(wh, hl) + nt(wl, hh)
    scores = _sigmoid(logits)
    sel = scores + rb_ref[...]
    gates, best = _route([sel[i:i + 1, :] for i in range(n_experts)],
                         [scores[i:i + 1, :] for i in range(n_experts)], n_experts)

    in_grp = [jnp.where(best == g, 1.0, 0.0) for g in range(N_GROUPS)]
    grp8 = jnp.concatenate(in_grp + [jnp.zeros((8 - N_GROUPS, tm), F32)], axis=0).astype(BF16)
    row = lax.broadcasted_iota(jnp.int32, (tm, tm), 0)
    col = lax.broadcasted_iota(jnp.int32, (tm, tm), 1)
    upper = jnp.where(row <= col, 1.0, 0.0).astype(BF16)
    run = jnp.dot(grp8, upper, preferred_element_type=F32)
    pos = jnp.zeros((1, tm), F32)
    seg = jnp.zeros((1, 1), F32)
    cnts = []
    for g in range(N_GROUPS):
        cnt = run[g:g + 1, tm - 1:tm]
        cnts.append(cnt)
        pos = pos + in_grp[g] * (seg + run[g:g + 1, :] - 1.0)
        seg = seg + jnp.floor((cnt + (MOE_BLK - 1)) * (1.0 / MOE_BLK)) * MOE_BLK
    cnt_ref[0, 0] = jnp.concatenate([jnp.broadcast_to(c, (1, LANE)) for c in cnts]
                                    + [jnp.zeros((8 - N_GROUPS, LANE), F32)], axis=0)
    tab = jnp.concatenate(gates + [pos, jnp.zeros((LANE - n_experts - 1, tm), F32)], axis=0).T
    tab_ref[0] = tab
    sel = jnp.where(lax.broadcasted_iota(jnp.int32, (ts, tm), 0) == pos.astype(jnp.int32), 1.0, 0.0).astype(BF16)
    hs_ref[0, 0] = jnp.dot(sel, h.astype(BF16), preferred_element_type=F32).astype(BF16)
    t1 = tab.astype(BF16)
    r1 = tab - t1.astype(F32)
    t2 = r1.astype(BF16)
    t3 = (r1 - t2.astype(F32)).astype(BF16)
    gs_ref[0, 0] = (jnp.dot(sel, t1, preferred_element_type=F32) + jnp.dot(sel, t2, preferred_element_type=F32)
                    + jnp.dot(sel, t3, preferred_element_type=F32))


def _moe_expert_kernel(lo_ref, hi_ref, x_ref, mod_ref, hs_ref, gs_ref, tab_ref, w1_ref, w3_ref, w2_ref, o_ref,
                       acc_ref, *, tm, ts, n_ctx, n_experts):
    b = pl.program_id(0)
    t = pl.program_id(1)
    e = pl.program_id(2)
    n_t = pl.num_programs(1)

    @pl.when(e == 0)
    def _():
        acc_ref[...] = jnp.zeros_like(acc_ref)

    per_step = w1_ref.shape[1]
    e_first = e * per_step
    seg = (b * n_t + t) * N_GROUPS + e_first // (n_experts // N_GROUPS)
    lane = lax.broadcasted_iota(jnp.int32, (MOE_BLK, LANE), 1)

    def block(j, carry):
        rows = pl.ds(pl.multiple_of(j * MOE_BLK, MOE_BLK), MOE_BLK)
        hb = hs_ref[0, 0, rows, :]
        gs = gs_ref[0, 0, rows, :]
        hids = []
        for i in range(per_step):
            a = jnp.dot(hb, w1_ref[0, i], preferred_element_type=F32)
            bb = jnp.dot(hb, w3_ref[0, i], preferred_element_type=F32)
            gcol = jnp.sum(jnp.where(lane == e_first + i, gs, 0.0), axis=-1, keepdims=True)
            hids.append((a * _sigmoid(a) * bb * gcol).astype(BF16))
        w2 = w2_ref[0].reshape(per_step * w2_ref.shape[2], w2_ref.shape[3])
        acc_ref[rows, :] += jnp.dot(jnp.concatenate(hids, axis=1), w2, preferred_element_type=F32)
        return carry

    lax.fori_loop(lo_ref[seg], hi_ref[seg], block, 0)

    @pl.when(e == pl.num_programs(2) - 1)
    def _():
        is_ctx = _ctx_rows(t, tm, n_ctx)
        pos = tab_ref[0][:, POS_COL:POS_COL + 1].astype(jnp.int32)
        back = jnp.where(lax.broadcasted_iota(jnp.int32, (tm, ts), 1) == pos, 1.0, 0.0).astype(BF16)
        y = jnp.dot(back, acc_ref[...].astype(BF16), preferred_element_type=F32)
        o_ref[0] = x_ref[0] + _mod_row(mod_ref, 5, is_ctx) * y


def _moe_call(xs, modsel, norm_g, router_wt, router_b, w1, w3, w2, *, layer, n_ctx, tm):
    bsz, tt, d = xs.shape
    n_experts = router_wt.shape[0]
    n_t = tt // tm
    ts = tm + N_GROUPS * MOE_BLK
    row2 = lambda b, t: (b, t, 0)
    const2 = lambda b, t: (0, 0)
    tile2 = lambda b, t: (b, t, 0, 0)
    hs, gs, tab, cnt = pl.pallas_call(
        functools.partial(_moe_route_kernel, tm=tm, ts=ts, n_ctx=n_ctx, n_experts=n_experts),
        out_shape=(jax.ShapeDtypeStruct((bsz, n_t, ts, d), BF16),
                   jax.ShapeDtypeStruct((bsz, n_t, ts, LANE), F32),
                   jax.ShapeDtypeStruct((bsz, tt, LANE), F32),
                   jax.ShapeDtypeStruct((bsz, n_t, 8, LANE), F32)),
        grid=(bsz, n_t),
        in_specs=[pl.BlockSpec((1, tm, d), row2),
                  pl.BlockSpec((1, 2, 8, d), lambda b, t: (b, 0, 0, 0)),
                  pl.BlockSpec((1, d), const2),
                  pl.BlockSpec(router_wt.shape, const2),
                  pl.BlockSpec((n_experts, 1), const2)],
        out_specs=(pl.BlockSpec((1, 1, ts, d), tile2),
                   pl.BlockSpec((1, 1, ts, LANE), tile2),
                   pl.BlockSpec((1, tm, LANE), row2),
                   pl.BlockSpec((1, 1, 8, LANE), tile2)),
        compiler_params=pltpu.CompilerParams(dimension_semantics=("parallel", "parallel"),
                                             vmem_limit_bytes=VMEM_LIMIT),
        name="moe_route",
    )(xs, modsel, norm_g, router_wt, router_b)

    n_blk = (cnt[:, :, :N_GROUPS, 0].astype(jnp.int32) + (MOE_BLK - 1)) // MOE_BLK
    hi = jnp.cumsum(n_blk, axis=-1)
    lo = hi - n_blk

    row = lambda b, t, e, lo_r, hi_r: (b, t, 0)
    wexp = lambda b, t, e, lo_r, hi_r: (layer, e, 0, 0)
    tile4 = lambda b, t, e, lo_r, hi_r: (b, t, 0, 0)
    per_step = n_experts // N_GROUPS if tm <= MOE_FULL_GROUP_MAX_TM else n_experts // N_GROUPS // 2
    return pl.pallas_call(
        functools.partial(_moe_expert_kernel, tm=tm, ts=ts, n_ctx=n_ctx, n_experts=n_experts),
        out_shape=jax.ShapeDtypeStruct((bsz, tt, d), F32),
        grid_spec=pltpu.PrefetchScalarGridSpec(
            num_scalar_prefetch=2,
            grid=(bsz, n_t, n_experts // per_step),
            in_specs=[pl.BlockSpec((1, tm, d), row),
                      pl.BlockSpec((1, 2, 8, d), lambda b, t, e, lo_r, hi_r: (b, 0, 0, 0)),
                      pl.BlockSpec((1, 1, ts, d), tile4),
                      pl.BlockSpec((1, 1, ts, LANE), tile4),
                      pl.BlockSpec((1, tm, LANE), row),
                      pl.BlockSpec((1, per_step) + w1.shape[2:], wexp),
                      pl.BlockSpec((1, per_step) + w3.shape[2:], wexp),
                      pl.BlockSpec((1, per_step) + w2.shape[2:], wexp)],
            out_specs=pl.BlockSpec((1, tm, d), row),
            scratch_shapes=[pltpu.VMEM((ts, d), F32)]),
        compiler_params=pltpu.CompilerParams(
            dimension_semantics=("parallel", "parallel", "arbitrary"),
            vmem_limit_bytes=VMEM_LIMIT),
        name="moe_experts",
    )(lo.reshape(-1), hi.reshape(-1), xs, modsel, hs, gs, tab, w1, w3, w2)


def kernel(x, c, ctx, c_ctx, mod_w, mod_b, norm1_g, norm2_g, w_in, rw_w0, rw_w_lora_b, rw_a0, rw_a_lora_b,
           rw_g_lora_b, rw_k_k, rw_k_a, rw_r_k, rw_ln_g, rw_ln_b, na_q_g, na_k_g, na_rpb, proj_a, proj_b,
           w_out, router_w, router_bias, moe_w1, moe_w3, moe_w2):
    bsz, seq, d = x.shape
    n_ctx = ctx.shape[1]
    depth = mod_w.shape[0]
    d_a = rw_w0.shape[2]
    d_b = proj_b.shape[1]
    lora_g = rw_g_lora_b.shape[1]
    heads_b = d_b // HEAD_DIM
    assert seq % (GRID_W * WIN_H) == 0 and n_ctx % 256 == 0 and bsz + 1 <= 16

    cs = jnp.zeros((16, d), F32).at[:bsz].set(c).at[bsz].set(c_ctx)
    mod = _mod_call(cs, mod_w, mod_b).reshape(depth, 16, 6, d)

    lane = jnp.arange(LANE)
    gs = (lane[:, None] // HEAD_DIM == lane[None, :] // HEAD_DIM).astype(BF16)
    router_wt = router_w.T
    router_b = router_bias.reshape(-1, 1)

    w_in_b, w1_b, w3_b, w2_b = (_to_bf16(w) for w in (w_in, moe_w1, moe_w3, moe_w2))
    xs = (ctx, x)
    for l in range(depth):
        last = l == depth - 1
        m_c = jnp.broadcast_to(mod[l, bsz][None], (bsz, 6, d))
        modsel = jnp.pad(jnp.stack([m_c, mod[l, :bsz]], axis=1), ((0, 0), (0, 0), (0, 2), (0, 0)))

        rkv, lg, qkv, gates, op_f, op_r, v_a, pe_f, pe_r = _in_proj_call(
            xs, modsel, norm1_g[l][None], w_in_b, rw_k_k[l][None],
            jnp.tile(na_q_g[l], heads_b)[None], jnp.tile(na_k_g[l], heads_b)[None], gs,
            rw_w0[l], rw_w_lora_b[l], rw_a0[l], rw_a_lora_b[l], rw_k_a[l][None],
            layer=l, n_ctx=n_ctx, d_a=d_a, d_b=d_b, lora_g=lora_g)
        y_f, y_r = _rwkv_call(op_f, op_r, v_a, pe_f, pe_r, n_ctx=n_ctx)
        yb = _attn_call(qkv, _bias_table(na_rpb[l]), n_ctx=n_ctx, with_ctx=not last)
        xs = _merge_call(xs, modsel, y_f, y_r, rkv, lg, yb, gates, rw_r_k[l].reshape(1, d_a),
                         rw_ln_g[l][None], rw_ln_b[l][None], rw_g_lora_b[l].astype(BF16),
                         proj_a[l].astype(BF16), proj_b[l].astype(BF16), w_out[l].astype(BF16), gs,
                         n_ctx=n_ctx, with_ctx=not last)
        moe_ctx = 0 if last else n_ctx
        tm = next(m for m in (1024, 768, 512, 256) if xs.shape[1] % m == 0)
        xs = _moe_call(xs, modsel, norm2_g[l][None], router_wt, router_b, w1_b, w3_b, w2_b,
                       layer=l, n_ctx=moe_ctx, tm=tm)
    return xs
```
